```python
import jax, jax.numpy as jnp
from jax import lax
import numpy as np


D_MODEL = 1024
BATCH = 8
SEQ = 8192
DEPTH = 1

MLA_HEADS = 8
MLA_Q_LORA = 256
MLA_KV_LORA = 128
MLA_NOPE = 64
MLA_ROPE = 32
MLA_QK = MLA_NOPE + MLA_ROPE
MLA_V = 64
ROPE_THETA = 10000.0
Q_BLOCK = 128
HG_HEADS = 4
HG_DK = 128
HG_DV = 128
HG_CHUNK = 64
MIX_WIDTH = MLA_HEADS * MLA_V + HG_HEADS * HG_DV
IN_SIZES = (MLA_Q_LORA, MLA_KV_LORA, MLA_ROPE,
            HG_HEADS * HG_DK, HG_HEADS * HG_DK, HG_HEADS * HG_DK,
            HG_HEADS * HG_DV, HG_HEADS * HG_DV)
N_IN = MLA_Q_LORA + MLA_KV_LORA + MLA_ROPE + 3 * HG_HEADS * HG_DK + 2 * HG_HEADS * HG_DV
N_GROUPS = 4
EXPERTS_PER_GROUP = 8
N_EXPERTS = N_GROUPS * EXPERTS_PER_GROUP
TOP_K = 2
D_EXPERT = 256
MOE_BLOCK = 256
EPS = 1e-6

kernel_name = "hybrid_mla_hgrn2_hiermoe_encoder"


def rms_norm(x, gain):
    xf = x.astype(jnp.float32)
    y = xf * lax.rsqrt(jnp.mean(xf * xf, axis=-1, keepdims=True) + EPS)
    return (y * gain.astype(jnp.float32)).astype(x.dtype)


def split_cols(t, sizes):
    out, off = [], 0
    for s in sizes:
        out.append(t[..., off:off + s])
        off += s
    return out


def rope(x, pos):
    half = x.shape[-1] // 2
    inv = 1.0 / (ROPE_THETA ** (jnp.arange(half, dtype=jnp.float32) / half))
    ang = pos.astype(jnp.float32)[:, None] * inv[None, :]
    cos = jnp.cos(ang)[:, None, :]
    sin = jnp.sin(ang)[:, None, :]
    xf = x.astype(jnp.float32)
    x1, x2 = xf[..., :half], xf[..., half:]
    return jnp.concatenate([x1 * cos - x2 * sin, x2 * cos + x1 * sin], -1).astype(x.dtype)


def mla_mixer(q_lat, kv_lat, k_rope, q_lat_norm, kv_lat_norm, w_uq, w_ukv, q_norm, k_norm):
    B, S, _ = q_lat.shape
    H = MLA_HEADS
    q = (rms_norm(q_lat, q_lat_norm) @ w_uq).reshape(B, S, H, MLA_QK)
    kv = (rms_norm(kv_lat, kv_lat_norm) @ w_ukv).reshape(B, S, H, MLA_NOPE + MLA_V)
    k_nope, v = kv[..., :MLA_NOPE], kv[..., MLA_NOPE:]
    k = jnp.concatenate([k_nope, jnp.broadcast_to(k_rope[:, :, None, :], (B, S, H, MLA_ROPE))], -1)
    q = rms_norm(q, q_norm)
    k = rms_norm(k, k_norm)
    pos = jnp.arange(S)
    q = jnp.concatenate([q[..., :MLA_NOPE], rope(q[..., MLA_NOPE:], pos)], -1)
    k = jnp.concatenate([k[..., :MLA_NOPE], rope(k[..., MLA_NOPE:], pos)], -1)
    q = q * (MLA_QK ** -0.5)
    qb = q.reshape(B, S // Q_BLOCK, Q_BLOCK, H, MLA_QK).transpose(1, 0, 2, 3, 4)

    def attend(q_blk):
        s = jnp.einsum('bqhd,bkhd->bhqk', q_blk, k, preferred_element_type=jnp.float32)
        p = jax.nn.softmax(s, axis=-1).astype(v.dtype)
        return jnp.einsum('bhqk,bkhd->bqhd', p, v)

    o = lax.map(attend, qb)
    return o.transpose(1, 0, 2, 3, 4).reshape(B, S, H * MLA_V)


def hgrn2_mixer(hq, f_fwd, f_bwd, hi, hg, lb_logits, layer, out_norm):
    B, S, _ = hq.shape
    H, C = HG_HEADS, HG_CHUNK
    NC = S // C
    lb = jnp.cumsum(jax.nn.softmax(lb_logits.astype(jnp.float32), axis=0), axis=0)[layer]
    lb = lb[:, None, None, :]
    z = jnp.stack([f_fwd, jnp.flip(f_bwd, 1)]).astype(jnp.float32)
    f = lb + (1.0 - lb) * jax.nn.sigmoid(z)
    log_f = jnp.log(f)
    k = (1.0 - lb) * jax.nn.sigmoid(-z)
    qf = jax.nn.silu(hq.astype(jnp.float32))
    vf = hi.astype(jnp.float32)
    q = jnp.stack([qf, jnp.flip(qf, 1)])
    v = jnp.stack([vf, jnp.flip(vf, 1)])

    def to_chunks(t, d):
        return t.reshape(2, B, NC, C, H, d).transpose(2, 0, 1, 4, 3, 5)

    qc, kc, vc, gc = to_chunks(q, HG_DK), to_chunks(k, HG_DK), to_chunks(v, HG_DV), to_chunks(log_f, HG_DK)
    causal_in_scan = jnp.tril(jnp.ones((C, C), dtype=bool))[:, :, None]

    def step(state, inp):
        q_t, k_t, v_t, g_t = inp
        b = jnp.cumsum(g_t, axis=-2)
        o_inter = jnp.einsum('zbhtd,zbhde->zbhte', q_t * jnp.exp(b), state)
        diff = b[..., :, None, :] - b[..., None, :, :]
        decay = jnp.exp(jnp.where(causal_in_scan, diff, -jnp.inf))
        a = jnp.einsum('zbhtd,zbhsd,zbhtsd->zbhts', q_t, k_t, decay)
        o_intra = jnp.einsum('zbhts,zbhse->zbhte', a, v_t)
        b_last = b[..., -1:, :]
        new_state = jnp.exp(b_last[..., 0, :])[..., None] * state + \
            jnp.einsum('zbhsd,zbhse->zbhde', k_t * jnp.exp(b_last - b), v_t)
        return new_state, o_inter + o_intra

    s0 = jnp.zeros((2, B, H, HG_DK, HG_DV), jnp.float32)
    _, o = lax.scan(step, s0, (qc, kc, vc, gc))
    o = o.transpose(1, 2, 0, 4, 3, 5).reshape(2, B, S, H, HG_DV)
    o = o[0] + jnp.flip(o[1], axis=1)
    o = rms_norm(o, out_norm) * jax.nn.silu(hg.reshape(B, S, H, HG_DV).astype(jnp.float32))
    return o.reshape(B, S, H * HG_DV).astype(hq.dtype)


def hier_moe(h, w_group, b_group, w_router, b_router, w_gate, w_up, w_down):
    B, S, D = h.shape
    T = B * S
    M = T * TOP_K
    x = h.reshape(T, D)
    g_prob = jax.nn.softmax((x @ w_group).astype(jnp.float32) + b_group.astype(jnp.float32), axis=-1)
    g_w, g_idx = lax.top_k(g_prob, 1)
    e_logits = ((x @ w_router).astype(jnp.float32) + b_router.astype(jnp.float32)).reshape(T, N_GROUPS, EXPERTS_PER_GROUP)
    e_logits = jnp.take_along_axis(e_logits, g_idx[:, :, None], axis=1)[:, 0]
    e_w, e_idx = lax.top_k(jax.nn.softmax(e_logits, axis=-1), TOP_K)
    e_w = e_w / jnp.sum(e_w, axis=-1, keepdims=True)
    gate = (g_w * e_w).astype(h.dtype).reshape(-1)
    eid = (g_idx * EXPERTS_PER_GROUP + e_idx).reshape(-1)
    tok = jnp.repeat(jnp.arange(T, dtype=jnp.int32), TOP_K)
    order = jnp.argsort(eid)
    eid_s, tok_s, gate_s = eid[order], tok[order], gate[order]
    counts = jnp.bincount(eid, length=N_EXPERTS)
    starts = jnp.cumsum(counts) - counts
    padded = (counts + MOE_BLOCK - 1) // MOE_BLOCK * MOE_BLOCK
    pend = jnp.cumsum(padded)
    pstart = pend - padded
    dest = pstart[eid_s] + (jnp.arange(M) - starts[eid_s])
    P = -(-M // MOE_BLOCK) * MOE_BLOCK + N_EXPERTS * MOE_BLOCK
    NB = P // MOE_BLOCK
    tok_buf = jnp.zeros((P,), jnp.int32).at[dest].set(tok_s)
    gate_buf = jnp.zeros((P,), h.dtype).at[dest].set(gate_s)
    block_e = jnp.minimum(jnp.searchsorted(pend, jnp.arange(NB) * MOE_BLOCK, side='right'), N_EXPERTS - 1)
    xb = x[tok_buf].reshape(NB, MOE_BLOCK, D)

    def expert_block(args):
        x_blk, e = args
        return (jax.nn.silu(x_blk @ w_gate[e]) * (x_blk @ w_up[e])) @ w_down[e]

    yb = lax.map(expert_block, (xb, block_e))
    y = jnp.zeros((T, D), h.dtype).at[tok_buf].add(yb.reshape(P, D) * gate_buf[:, None])
    return y.reshape(B, S, D)


def setup_inputs(seed: int = 0) -> dict:
    key = jax.random.key(seed)
    ks = jax.random.split(key, 24)
    f32 = jnp.float32
    L = DEPTH

    def nrm(k, shape, fan_in):
        return jax.random.normal(k, shape, f32) * (fan_in ** -0.5)

    def gain(k, shape):
        return 1.0 + 0.02 * jax.random.normal(k, shape, f32)

    return {
        "x": jax.random.normal(ks[0], (BATCH, SEQ, D_MODEL), f32),
        "norm_mix": gain(ks[1], (L, D_MODEL)),
        "w_in": nrm(ks[2], (L, D_MODEL, N_IN), D_MODEL),
        "q_lat_norm": gain(ks[3], (L, MLA_Q_LORA)),
        "kv_lat_norm": gain(ks[4], (L, MLA_KV_LORA)),
        "w_uq": nrm(ks[5], (L, MLA_Q_LORA, MLA_HEADS * MLA_QK), MLA_Q_LORA),
        "w_ukv": nrm(ks[6], (L, MLA_KV_LORA, MLA_HEADS * (MLA_NOPE + MLA_V)), MLA_KV_LORA),
        "q_norm": gain(ks[7], (L, MLA_QK)),
        "k_norm": gain(ks[8], (L, MLA_QK)),
        "lb_logits": 0.1 * jax.random.normal(ks[9], (DEPTH + 1, 2, HG_HEADS * HG_DK), f32),
        "hg_out_norm": gain(ks[10], (L, HG_DV)),
        "w_out": nrm(ks[11], (L, MIX_WIDTH, D_MODEL), MIX_WIDTH),
        "norm_ffn": gain(ks[12], (L, D_MODEL)),
        "w_group": nrm(ks[13], (L, D_MODEL, N_GROUPS), D_MODEL),
        "b_group": 0.01 * jax.random.normal(ks[14], (L, N_GROUPS), f32),
        "w_router": nrm(ks[15], (L, D_MODEL, N_EXPERTS), D_MODEL),
        "b_router": 0.01 * jax.random.normal(ks[16], (L, N_EXPERTS), f32),
        "w_gate": nrm(ks[17], (L, N_EXPERTS, D_MODEL, D_EXPERT), D_MODEL),
        "w_up": nrm(ks[18], (L, N_EXPERTS, D_MODEL, D_EXPERT), D_MODEL),
        "w_down": nrm(ks[19], (L, N_EXPERTS, D_EXPERT, D_MODEL), D_EXPERT),
    }


def reference(x, norm_mix, w_in, q_lat_norm, kv_lat_norm, w_uq, w_ukv, q_norm, k_norm,
              lb_logits, hg_out_norm, w_out, norm_ffn, w_group, b_group, w_router, b_router,
              w_gate, w_up, w_down):
    h = x
    for l in range(DEPTH):
        n = rms_norm(h, norm_mix[l])
        proj = n @ w_in[l]
        q_lat, kv_lat, k_rope, hq, f_fwd, f_bwd, hi, hg = split_cols(proj, IN_SIZES)
        a = mla_mixer(q_lat, kv_lat, k_rope, q_lat_norm[l], kv_lat_norm[l], w_uq[l], w_ukv[l],
                      q_norm[l], k_norm[l])
        r = hgrn2_mixer(hq, f_fwd, f_bwd, hi, hg, lb_logits, l, hg_out_norm[l])
        h = h + jnp.concatenate([a, r], axis=-1) @ w_out[l]
        h = h + hier_moe(rms_norm(h, norm_ffn[l]), w_group[l], b_group[l], w_router[l], b_router[l],
                         w_gate[l], w_up[l], w_down[l])
    return h
```

```python
import functools
import math

import jax
import jax.numpy as jnp
from jax import lax
from jax.experimental import pallas as pl
from jax.experimental.pallas import tpu as pltpu

F32 = jnp.float32
BF16 = jnp.bfloat16
I32 = jnp.int32

EPS = 1e-6
LANES = 128
VMEM_LIMIT = 48 * 1024 * 1024

MLA_HEADS = 8
MLA_NOPE = 64
MLA_ROPE = 32
MLA_QK = MLA_NOPE + MLA_ROPE
MLA_V = 64
ROPE_THETA = 10000.0
HG_HEADS = 4
HG_DK = 128
HG_DV = 128
HG_CHUNK = 64
N_GROUPS = 4
EXPERTS_PER_GROUP = 8
N_EXPERTS = N_GROUPS * EXPERTS_PER_GROUP
MOE_BLOCK = 256

NT_DIMS = (((1,), (1,)), ((), ()))
TN_DIMS = (((0,), (0,)), ((), ()))


def _params(sem, **kw):
    return pltpu.CompilerParams(dimension_semantics=sem, vmem_limit_bytes=VMEM_LIMIT, **kw)


def _full(shape):
    n = len(shape)
    return pl.BlockSpec(shape, lambda *_: (0,) * n)


def _inproj_kernel(x_ref, g_ref, wmla_ref, wh_ref, qlg_ref, kvlg_ref, wuq_ref, wuk_ref, wuv_ref,
                   qng_ref, kng_ref, cos_ref, sa_ref, sb_ref,
                   q_out, k_out, v_out, hq_out, fz_out, hi_out, hg_out, *, q_scale):
    x = x_ref[...]
    ms = jnp.mean(x * x, axis=-1, keepdims=True)
    n = (x * lax.rsqrt(ms + EPS) * g_ref[...]).astype(BF16)

    hw = hq_out.shape[1]
    hq_out[...] = jnp.dot(n, wh_ref[:, 0 * hw:1 * hw], preferred_element_type=F32).astype(BF16)
    fz_out[0] = jnp.dot(n, wh_ref[:, 1 * hw:2 * hw], preferred_element_type=F32).astype(BF16)
    fz_out[1] = jnp.dot(n, wh_ref[:, 2 * hw:3 * hw], preferred_element_type=F32).astype(BF16)
    hi_out[...] = jnp.dot(n, wh_ref[:, 3 * hw:4 * hw], preferred_element_type=F32).astype(BF16)
    hg_out[...] = jnp.dot(n, wh_ref[:, 4 * hw:5 * hw], preferred_element_type=F32).astype(BF16)

    lat = jnp.dot(n, wmla_ref[...], preferred_element_type=F32)
    ql = lat[:, 0:256]
    kvl = lat[:, 256:384]
    kr = lat[:, 384:512]
    qn = (ql * lax.rsqrt(jnp.mean(ql * ql, axis=-1, keepdims=True) + EPS) * qlg_ref[...]).astype(BF16)
    kvn = (kvl * lax.rsqrt(jnp.mean(kvl * kvl, axis=-1, keepdims=True) + EPS) * kvlg_ref[...]).astype(BF16)
    q_all = jnp.dot(qn, wuq_ref[...], preferred_element_type=F32)
    k_all = jnp.dot(kvn, wuk_ref[...], preferred_element_type=F32)
    v_out[...] = jnp.dot(kvn, wuv_ref[...], preferred_element_type=F32).astype(BF16)

    cos = cos_ref[...]
    sa = sa_ref[...]
    sb = sb_ref[...]
    qng = qng_ref[...]
    kng = kng_ref[...]

    def qk_norm_rope(t, gain):
        tn = t * lax.rsqrt(jnp.sum(t * t, axis=-1, keepdims=True) * (1.0 / MLA_QK) + EPS) * gain
        up = pltpu.roll(tn, LANES - MLA_ROPE // 2, axis=1)
        dn = pltpu.roll(tn, MLA_ROPE // 2, axis=1)
        return tn * cos + up * sa + dn * sb

    for h in range(MLA_HEADS):
        sl = slice(h * LANES, (h + 1) * LANES)
        q_out[:, sl] = (qk_norm_rope(q_all[:, sl], qng) * q_scale).astype(BF16)
        k_out[:, sl] = qk_norm_rope(k_all[:, sl] + kr, kng).astype(BF16)


def _inproj(x2, seq, norm_mix, w_in, q_lat_norm, kv_lat_norm, w_uq, w_ukv, q_norm, k_norm, tm=512):
    T, D = x2.shape
    tm = min(tm, seq)
    H = MLA_HEADS
    ql, kvl = w_uq.shape[0], w_ukv.shape[0]
    hw = HG_HEADS * HG_DK
    o_kr = ql + kvl
    o_h = o_kr + MLA_ROPE
    zeros = lambda c: jnp.zeros((D, c), F32)
    w_mla = jnp.concatenate([w_in[:, :o_kr], zeros(MLA_NOPE), w_in[:, o_kr:o_h],
                             zeros(LANES - MLA_QK)], axis=1).astype(BF16)
    w_h = w_in[:, o_h:].astype(BF16)
    pad = LANES - MLA_QK
    wuq = jnp.pad(w_uq.reshape(ql, H, MLA_QK), ((0, 0), (0, 0), (0, pad))).reshape(ql, H * LANES).astype(BF16)
    wkv = w_ukv.reshape(kvl, H, MLA_NOPE + MLA_V)
    wuk = jnp.pad(wkv[:, :, :MLA_NOPE], ((0, 0), (0, 0), (0, LANES - MLA_NOPE))).reshape(kvl, H * LANES).astype(BF16)
    wuv = wkv[:, :, MLA_NOPE:].reshape(kvl, H * MLA_V).astype(BF16)
    qng = jnp.pad(q_norm, (0, pad)).reshape(1, LANES)
    kng = jnp.pad(k_norm, (0, pad)).reshape(1, LANES)

    half = MLA_ROPE // 2
    inv = 1.0 / (ROPE_THETA ** (jnp.arange(half, dtype=F32) / half))
    ang = jnp.arange(seq, dtype=F32)[:, None] * inv[None, :]
    cos, sin = jnp.cos(ang), jnp.sin(ang)
    z = lambda c: jnp.zeros((seq, c), F32)
    cos_t = jnp.concatenate([jnp.ones((seq, MLA_NOPE), F32), cos, cos, z(pad)], axis=1)
    sa_t = jnp.concatenate([z(MLA_NOPE), -sin, z(half), z(pad)], axis=1)
    sb_t = jnp.concatenate([z(MLA_NOPE), z(half), sin, z(pad)], axis=1)

    nseq = seq // tm
    row = lambda w: pl.BlockSpec((tm, w), lambda i: (i, 0))
    tab = pl.BlockSpec((tm, LANES), lambda i: (i % nseq, 0))
    q_scale = (MLA_QK ** -0.5) * math.log2(math.e)
    outs = pl.pallas_call(
        functools.partial(_inproj_kernel, q_scale=q_scale),
        grid=(T // tm,),
        in_specs=[row(D), _full((1, D)), _full(w_mla.shape), _full(w_h.shape), _full((1, ql)), _full((1, kvl)),
                  _full(wuq.shape), _full(wuk.shape), _full(wuv.shape), _full((1, LANES)), _full((1, LANES)),
                  tab, tab, tab],
        out_specs=[row(H * LANES), row(H * LANES), row(H * MLA_V), row(hw),
                   pl.BlockSpec((2, tm, hw), lambda i: (0, i, 0)), row(hw), row(hw)],
        out_shape=[jax.ShapeDtypeStruct((T, H * LANES), BF16), jax.ShapeDtypeStruct((T, H * LANES), BF16),
                   jax.ShapeDtypeStruct((T, H * MLA_V), BF16), jax.ShapeDtypeStruct((T, hw), BF16),
                   jax.ShapeDtypeStruct((2, T, hw), BF16), jax.ShapeDtypeStruct((T, hw), BF16),
                   jax.ShapeDtypeStruct((T, hw), BF16)],
        compiler_params=_params(("parallel",)),
        name="inproj",
    )(x2, norm_mix.reshape(1, D), w_mla, w_h, q_lat_norm.reshape(1, ql), kv_lat_norm.reshape(1, kvl),
      wuq, wuk, wuv, qng, kng, cos_t, sa_t, sb_t)
    return outs


def _attn_kernel(q_ref, k_ref, v_ref, o_ref, *, tk):
    tq = q_ref.shape[0]
    nk = k_ref.shape[0] // tk
    outs = []
    for j in range(2):
        sl = slice(j * LANES, (j + 1) * LANES)
        q = q_ref[:, sl]

        def body(c, carry, sl=sl, q=q):
            m, l, acc = carry
            r0 = pl.multiple_of(c * tk, tk)
            ks = k_ref[pl.ds(r0, tk), sl]
            vs = v_ref[pl.ds(r0, tk), :]
            s = lax.dot_general(q, ks, NT_DIMS, preferred_element_type=F32)
            m_new = jnp.maximum(m, jnp.max(s, axis=-1, keepdims=True))
            p = jnp.exp2(s - m_new)
            alpha = jnp.exp2(m - m_new)
            l = alpha * l + jnp.sum(p, axis=-1, keepdims=True)
            acc = alpha * acc + jnp.dot(p.astype(BF16), vs, preferred_element_type=F32)
            return m_new, l, acc

        init = (jnp.full((tq, 1), -jnp.inf, F32), jnp.zeros((tq, 1), F32), jnp.zeros((tq, LANES), F32))
        m, l, acc = lax.fori_loop(0, nk, body, init)
        outs.append(acc / l)
    lane = lax.broadcasted_iota(I32, (tq, LANES), 1)
    o_ref[...] = jnp.where(lane < MLA_V, outs[0], outs[1]).astype(o_ref.dtype)


def _attention(q3, k3, v3, tq=512, tk=512):
    B, S, _ = q3.shape
    tq, tk = min(tq, S), min(tk, S)
    hp = MLA_HEADS // 2
    return pl.pallas_call(
        functools.partial(_attn_kernel, tk=tk),
        grid=(B, hp, S // tq),
        in_specs=[pl.BlockSpec((None, tq, 2 * LANES), lambda b, h, i: (b, i, h)),
                  pl.BlockSpec((None, S, 2 * LANES), lambda b, h, i: (b, 0, h)),
                  pl.BlockSpec((None, S, 2 * MLA_V), lambda b, h, i: (b, 0, h))],
        out_specs=pl.BlockSpec((None, tq, 2 * MLA_V), lambda b, h, i: (b, i, h)),
        out_shape=jax.ShapeDtypeStruct((B, S, MLA_HEADS * MLA_V), BF16),
        compiler_params=_params(("parallel", "parallel", "arbitrary")),
        name="attention",
    )(q3, k3, v3)


def _hgrn_kernel(hq_ref, z_ref, hi_ref, lb_ref, tri_ref, o_ref, st_ref):
    d = pl.program_id(1)
    C = HG_CHUNK
    nc = hq_ref.shape[0] // C

    @pl.when(pl.program_id(2) == 0)
    def _():
        st_ref[...] = jnp.zeros_like(st_ref)

    one_m_lb = 1.0 - lb_ref[...]
    tri = tri_ref[...]
    keep = tri > 0
    fwd = d == 0

    def chunk(ci, carry):
        c = jnp.where(fwd, ci, nc - 1 - ci)
        r0 = pl.multiple_of(c * C, C)
        z = z_ref[pl.ds(r0, C), :].astype(F32)
        kk = one_m_lb * jax.nn.sigmoid(-z)
        g = jnp.log(1.0 - kk)
        g_hi = g.astype(BF16)
        g_lo = (g - g_hi.astype(F32)).astype(BF16)
        b = jnp.dot(tri, g_hi, preferred_element_type=F32) + jnp.dot(tri, g_lo, preferred_element_type=F32)
        b_edge = jnp.where(fwd, b[C - 1:C, :], b[0:1, :])
        hq = hq_ref[pl.ds(r0, C), :].astype(F32)
        qh = (hq * jax.nn.sigmoid(hq) * jnp.exp(b)).astype(BF16)
        kt = (kk * jnp.exp(-b)).astype(BF16)
        ks = (kk * jnp.exp(b_edge - b)).astype(BF16)
        dec = jnp.exp(b_edge)
        v = hi_ref[pl.ds(r0, C), :]
        for h in range(HG_HEADS):
            sl = slice(h * HG_DK, (h + 1) * HG_DK)
            a = lax.dot_general(qh[:, sl], kt[:, sl], NT_DIMS, preferred_element_type=F32)
            a = jnp.where(keep, a, 0.0).astype(BF16)
            st = st_ref[h]
            o = jnp.dot(a, v[:, sl], preferred_element_type=F32)
            o = o + lax.dot_general(qh[:, sl], st.astype(BF16), NT_DIMS, preferred_element_type=F32)
            o_ref[pl.ds(r0, C), sl] = o.astype(o_ref.dtype)
            st_ref[h] = st * dec[:, sl] + lax.dot_general(v[:, sl], ks[:, sl], TN_DIMS,
                                                          preferred_element_type=F32)
        return carry

    lax.fori_loop(0, nc, chunk, 0)


def _hgrn(hq3, fz4, hi3, lb, tl=512):
    B, S, W = hq3.shape
    tl = min(tl, S)
    nt = S // tl
    C = HG_CHUNK
    r = lax.broadcasted_iota(I32, (C, C), 0)
    c = lax.broadcasted_iota(I32, (C, C), 1)
    tri = jnp.stack([r >= c, r <= c]).astype(BF16)
    tile = lambda b, d, i: (b, i + d * (nt - 1 - 2 * i), 0)
    return pl.pallas_call(
        _hgrn_kernel,
        grid=(B, 2, nt),
        in_specs=[pl.BlockSpec((None, tl, W), tile),
                  pl.BlockSpec((None, None, tl, W), lambda b, d, i: (d, b, i + d * (nt - 1 - 2 * i), 0)),
                  pl.BlockSpec((None, tl, W), tile),
                  pl.BlockSpec((None, 1, W), lambda b, d, i: (d, 0, 0)),
                  pl.BlockSpec((None, C, C), lambda b, d, i: (d, 0, 0))],
        out_specs=pl.BlockSpec((None, None, tl, W), lambda b, d, i: (d, b, i + d * (nt - 1 - 2 * i), 0)),
        out_shape=jax.ShapeDtypeStruct((2, B, S, W), BF16),
        scratch_shapes=[pltpu.VMEM((HG_HEADS, HG_DV, HG_DK), F32)],
        compiler_params=_params(("parallel", "parallel", "arbitrary")),
        name="hgrn",
    )(hq3, fz4, hi3, lb, tri)


def _outproj_kernel(x_ref, a_ref, o_ref, hg_ref, ong_ref, wa_ref, wr_ref, g2_ref, wrt_ref, brt_ref,
                    h1_out, n2_out, route_out):
    o = o_ref[0].astype(F32) + o_ref[1].astype(F32)
    hg = hg_ref[...].astype(F32)
    gate = hg * jax.nn.sigmoid(hg)
    ong = ong_ref[...]
    parts = []
    for h in range(HG_HEADS):
        sl = slice(h * HG_DV, (h + 1) * HG_DV)
        oh = o[:, sl]
        parts.append((oh * lax.rsqrt(jnp.mean(oh * oh, axis=-1, keepdims=True) + EPS) * ong * gate[:, sl]).astype(BF16))
    r = jnp.concatenate(parts, axis=1)
    h1 = x_ref[...] + jnp.dot(a_ref[...], wa_ref[...], preferred_element_type=F32)
    h1 = h1 + jnp.dot(r, wr_ref[...], preferred_element_type=F32)
    h1_out[...] = h1
    n2 = h1 * lax.rsqrt(jnp.mean(h1 * h1, axis=-1, keepdims=True) + EPS) * g2_ref[...]
    n2_out[...] = n2

    logits = jnp.dot(n2, wrt_ref[...], preferred_element_type=F32, precision=lax.Precision.HIGHEST) + brt_ref[...]
    tm = logits.shape[0]
    lane = lax.broadcasted_iota(I32, (tm, LANES), 1)
    ninf = -jnp.inf
    is_g = lane < N_GROUPS
    gl = jnp.where(is_g, logits, ninf)
    gmax = jnp.max(gl, axis=-1, keepdims=True)
    gidx = jnp.min(jnp.where(gl == gmax, lane, LANES), axis=-1, keepdims=True)
    g_w = 1.0 / jnp.sum(jnp.where(is_g, jnp.exp(logits - gmax), 0.0), axis=-1, keepdims=True)
    lo = N_GROUPS + EXPERTS_PER_GROUP * gidx
    el = jnp.where((lane >= lo) & (lane < lo + EXPERTS_PER_GROUP), logits, ninf)
    m1 = jnp.max(el, axis=-1, keepdims=True)
    i1 = jnp.min(jnp.where(el == m1, lane, LANES), axis=-1, keepdims=True)
    el2 = jnp.where(lane == i1, ninf, el)
    m2 = jnp.max(el2, axis=-1, keepdims=True)
    i2 = jnp.min(jnp.where(el2 == m2, lane, LANES), axis=-1, keepdims=True)
    t = jnp.exp(m2 - m1)
    w1 = 1.0 / (1.0 + t)
    w2 = t / (1.0 + t)
    e1 = (i1 - N_GROUPS).astype(F32)
    e2 = (i2 - N_GROUPS).astype(F32)
    route = jnp.where(lane == 0, e1, jnp.where(lane == 1, e2, jnp.where(lane == 2, g_w * w1,
                      jnp.where(lane == 3, g_w * w2, 0.0))))
    route_out[...] = route


def _outproj(x2, a2, o3, hg2, hg_out_norm, w_out, norm_ffn, w_group, b_group, w_router, b_router, tm=512):
    T, D = x2.shape
    tm = min(tm, T)
    wa = w_out[:MLA_HEADS * MLA_V].astype(BF16)
    wr = w_out[MLA_HEADS * MLA_V:].astype(BF16)
    npad = LANES - N_GROUPS - N_EXPERTS
    wrt = jnp.concatenate([w_group, w_router, jnp.zeros((D, npad), F32)], axis=1)
    brt = jnp.concatenate([b_group, b_router, jnp.zeros((npad,), F32)]).reshape(1, LANES)
    row = lambda w: pl.BlockSpec((tm, w), lambda i: (i, 0))
    hw = HG_HEADS * HG_DV
    return pl.pallas_call(
        _outproj_kernel,
        grid=(T // tm,),
        in_specs=[row(D), row(a2.shape[1]), pl.BlockSpec((2, tm, hw), lambda i: (0, i, 0)), row(hw),
                  _full((1, HG_DV)), _full(wa.shape), _full(wr.shape), _full((1, D)), _full(wrt.shape),
                  _full((1, LANES))],
        out_specs=[row(D), row(D), row(LANES)],
        out_shape=[jax.ShapeDtypeStruct((T, D), F32), jax.ShapeDtypeStruct((T, D), F32),
                   jax.ShapeDtypeStruct((T, LANES), F32)],
        compiler_params=_params(("parallel",)),
        name="outproj",
    )(x2, a2, o3, hg2, hg_out_norm.reshape(1, HG_DV), wa, wr, norm_ffn.reshape(1, D), wrt, brt)


def _rank_kernel(route_ref, tri_ref, upper_ref, dest_out, pend_out, cnt_ref, base_ref):
    p = pl.program_id(0)
    i = pl.program_id(1)
    tm = route_ref.shape[0]
    lane = lax.broadcasted_iota(I32, (tm, LANES), 1)
    route = route_ref[...]
    e1 = route[:, 0:1].astype(I32)
    e2 = route[:, 1:2].astype(I32)
    is1 = lane == e1
    is2 = lane == e2
    onehot = jnp.where(is1 | is2, 1.0, 0.0)
    colsum = jnp.sum(onehot, axis=0, keepdims=True)

    @pl.when((p == 0) & (i == 0))
    def _():
        cnt_ref[...] = jnp.zeros_like(cnt_ref)

    @pl.when(p == 0)
    def _():
        cnt_ref[...] += colsum

    @pl.when((p == 1) & (i == 0))
    def _():
        cnt = cnt_ref[...]
        padded = jnp.ceil(cnt * (1.0 / MOE_BLOCK)) * MOE_BLOCK
        start = jnp.dot(jnp.broadcast_to(padded, (8, LANES)), upper_ref[...], preferred_element_type=F32,
                        precision=lax.Precision.HIGHEST)[0:1]
        base_ref[...] = start
        pend_out[...] = jnp.broadcast_to(start + padded, (8, LANES))

    @pl.when(p == 1)
    def _():
        before = jnp.dot(tri_ref[...], onehot.astype(BF16), preferred_element_type=F32)
        pos = base_ref[...] + before
        d1 = jnp.sum(jnp.where(is1, pos, 0.0), axis=-1, keepdims=True)
        d2 = jnp.sum(jnp.where(is2, pos, 0.0), axis=-1, keepdims=True)
        dest_out[...] = jnp.where(lane == 0, d1, jnp.where(lane == 1, d2, 0.0)).astype(I32)
        base_ref[...] += colsum


def _rank(route, tm=512):
    T = route.shape[0]
    tm = min(tm, T)
    r = lax.broadcasted_iota(I32, (tm, tm), 0)
    c = lax.broadcasted_iota(I32, (tm, tm), 1)
    tri = (r > c).astype(BF16)
    ru = lax.broadcasted_iota(I32, (LANES, LANES), 0)
    cu = lax.broadcasted_iota(I32, (LANES, LANES), 1)
    upper = (ru < cu).astype(F32)
    return pl.pallas_call(
        _rank_kernel,
        grid=(2, T // tm),
        in_specs=[pl.BlockSpec((tm, LANES), lambda p, i: (i, 0)), _full((tm, tm)), _full((LANES, LANES))],
        out_specs=[pl.BlockSpec((tm, LANES), lambda p, i: (i * p, 0)), _full((8, LANES))],
        out_shape=[jax.ShapeDtypeStruct((T, LANES), I32), jax.ShapeDtypeStruct((8, LANES), F32)],
        scratch_shapes=[pltpu.VMEM((1, LANES), F32), pltpu.VMEM((1, LANES), F32)],
        compiler_params=_params(("arbitrary", "arbitrary")),
        name="rank",
    )(route, tri, upper)


def _dispatch_kernel(dest_ref, n2_ref, xs_in, xs_out, sem):
    del xs_in
    tm = n2_ref.shape[0]

    def issue(r, carry):
        for k in range(2):
            dst = dest_ref[2 * r + k]
            pltpu.make_async_copy(n2_ref.at[pl.ds(r, 1)], xs_out.at[pl.ds(dst, 1)], sem).start()
        return carry

    lax.fori_loop(0, tm, issue, 0, unroll=8)
    for _ in range(2):
        pltpu.make_async_copy(n2_ref, xs_out.at[pl.ds(0, tm)], sem).wait()


def _dispatch(n2, dest_flat, n_rows, tm=512):
    T, D = n2.shape
    tm = min(tm, T)
    xs0 = jnp.zeros((n_rows, D), n2.dtype)
    return pl.pallas_call(
        _dispatch_kernel,
        grid=(T // tm,),
        in_specs=[pl.BlockSpec((2 * tm,), lambda i: (i,), memory_space=pltpu.SMEM),
                  pl.BlockSpec((tm, D), lambda i: (i, 0)),
                  pl.BlockSpec(memory_space=pl.ANY)],
        out_specs=pl.BlockSpec(memory_space=pl.ANY),
        out_shape=jax.ShapeDtypeStruct((n_rows, D), n2.dtype),
        scratch_shapes=[pltpu.SemaphoreType.DMA(())],
        input_output_aliases={2: 0},
        compiler_params=_params(("arbitrary",), has_side_effects=True),
        name="dispatch",
    )(dest_flat, n2, xs0)


def _expert_kernel(be_ref, nused_ref, x_ref, wg_ref, wu_ref, wd_ref, y_ref):
    i = pl.program_id(0)

    @pl.when(i < nused_ref[0])
    def _():
        x = x_ref[...].astype(BF16)
        g = jnp.dot(x, wg_ref[...], preferred_element_type=F32)
        u = jnp.dot(x, wu_ref[...], preferred_element_type=F32)
        hmid = (g * jax.nn.sigmoid(g) * u).astype(BF16)
        y_ref[...] = jnp.dot(hmid, wd_ref[...], preferred_element_type=F32)

    @pl.when(i >= nused_ref[0])
    def _():
        y_ref[...] = jnp.zeros_like(y_ref)


def _experts(xs, block_e, nused, w_gate, w_up, w_down):
    P, D = xs.shape
    nb = P // MOE_BLOCK
    de = w_gate.shape[2]
    grid_spec = pltpu.PrefetchScalarGridSpec(
        num_scalar_prefetch=2,
        grid=(nb,),
        in_specs=[pl.BlockSpec((MOE_BLOCK, D), lambda i, be, nu: (i, 0)),
                  pl.BlockSpec((None, D, de), lambda i, be, nu: (be[i], 0, 0)),
                  pl.BlockSpec((None, D, de), lambda i, be, nu: (be[i], 0, 0)),
                  pl.BlockSpec((None, de, D), lambda i, be, nu: (be[i], 0, 0))],
        out_specs=pl.BlockSpec((MOE_BLOCK, D), lambda i, be, nu: (i, 0)),
    )
    return pl.pallas_call(
        _expert_kernel,
        grid_spec=grid_spec,
        out_shape=jax.ShapeDtypeStruct((P, D), F32),
        compiler_params=_params(("arbitrary",)),
        name="experts",
    )(block_e, nused, xs, w_gate.astype(BF16), w_up.astype(BF16), w_down.astype(BF16))


def _combine_kernel(dest_ref, h1_ref, route_ref, ys_ref, out_ref, ybuf, sem):
    tm = h1_ref.shape[0]

    def issue(r, carry):
        for k in range(2):
            src = dest_ref[2 * r + k]
            pltpu.make_async_copy(ys_ref.at[pl.ds(src, 1)], ybuf.at[k, pl.ds(r, 1)], sem).start()
        return carry

    lax.fori_loop(0, tm, issue, 0, unroll=8)
    for k in range(2):
        pltpu.make_async_copy(ys_ref.at[pl.ds(0, tm)], ybuf.at[k], sem).wait()
    route = route_ref[...]
    out_ref[...] = h1_ref[...] + route[:, 2:3] * ybuf[0] + route[:, 3:4] * ybuf[1]


def _combine(h1, route, dest_flat, ys, tm=512):
    T, D = h1.shape
    tm = min(tm, T)
    return pl.pallas_call(
        _combine_kernel,
        grid=(T // tm,),
        in_specs=[pl.BlockSpec((2 * tm,), lambda i: (i,), memory_space=pltpu.SMEM),
                  pl.BlockSpec((tm, D), lambda i: (i, 0)),
                  pl.BlockSpec((tm, LANES), lambda i: (i, 0)),
                  pl.BlockSpec(memory_space=pl.ANY)],
        out_specs=pl.BlockSpec((tm, D), lambda i: (i, 0)),
        out_shape=jax.ShapeDtypeStruct((T, D), F32),
        scratch_shapes=[pltpu.VMEM((2, tm, D), F32), pltpu.SemaphoreType.DMA(())],
        compiler_params=_params(("arbitrary",)),
        name="combine",
    )(dest_flat, h1, route, ys)


def _mixers(x, norm_mix, w_in, q_lat_norm, kv_lat_norm, w_uq, w_ukv, q_norm, k_norm, lb_logits, layer):
    B, S, D = x.shape
    T = B * S
    q, k, v, hq, fz, hi, hg = _inproj(x.reshape(T, D), S, norm_mix, w_in, q_lat_norm, kv_lat_norm,
                                      w_uq, w_ukv, q_norm, k_norm)
    a = _attention(q.reshape(B, S, -1), k.reshape(B, S, -1), v.reshape(B, S, -1))
    lb = jnp.cumsum(jax.nn.softmax(lb_logits.astype(F32), axis=0), axis=0)[layer]
    hw = hq.shape[1]
    o = _hgrn(hq.reshape(B, S, hw), fz.reshape(2, B, S, hw), hi.reshape(B, S, hw), lb.reshape(2, 1, hw))
    return a.reshape(T, -1), o.reshape(2, T, hw), hg


def _moe(h1, n2, route, w_gate, w_up, w_down):
    T, D = h1.shape
    dest, pend = _rank(route)
    dest_flat = dest[:, :2].reshape(-1)
    n_rows = -(-(2 * T) // MOE_BLOCK) * MOE_BLOCK + N_EXPERTS * MOE_BLOCK
    nb = n_rows // MOE_BLOCK
    pend_i = pend[0, :N_EXPERTS].astype(I32)
    block_e = jnp.minimum(jnp.searchsorted(pend_i, jnp.arange(nb, dtype=I32) * MOE_BLOCK, side='right'),
                          N_EXPERTS - 1).astype(I32)
    nused = (pend_i[N_EXPERTS - 1:] // MOE_BLOCK).astype(I32)
    xs = _dispatch(n2, dest_flat, n_rows)
    ys = _experts(xs, block_e, nused, w_gate, w_up, w_down)
    return _combine(h1, route, dest_flat, ys)


def kernel(x, norm_mix, w_in, q_lat_norm, kv_lat_norm, w_uq, w_ukv, q_norm, k_norm, lb_logits, hg_out_norm,
           w_out, norm_ffn, w_group, b_group, w_router, b_router, w_gate, w_up, w_down):
    B, S, D = x.shape
    h = x
    for l in range(norm_mix.shape[0]):
        h2 = h.reshape(B * S, D)
        a, o, hg = _mixers(h, norm_mix[l], w_in[l], q_lat_norm[l], kv_lat_norm[l], w_uq[l], w_ukv[l],
                           q_norm[l], k_norm[l], lb_logits, l)
        h1, n2, route = _outproj(h2, a, o, hg, hg_out_norm[l], w_out[l], norm_ffn[l], w_group[l], b_group[l],
                                 w_router[l], b_router[l])
        h = _moe(h1, n2, route, w_gate[l], w_up[l], w_down[l]).reshape(B, S, D)
    return h
```

```python
import functools
import math

import jax
import jax.numpy as jnp
from jax import lax
from jax.experimental import pallas as pl
from jax.experimental.pallas import tpu as pltpu

F32 = jnp.float32
BF16 = jnp.bfloat16
I32 = jnp.int32

EPS = 1e-6
LANES = 128
VMEM_LIMIT = 48 * 1024 * 1024

MLA_HEADS = 8
MLA_NOPE = 64
MLA_ROPE = 32
MLA_QK = MLA_NOPE + MLA_ROPE
MLA_V = 64
ROPE_THETA = 10000.0
HG_HEADS = 4
HG_DK = 128
HG_DV = 128
HG_CHUNK = 64
N_GROUPS = 4
EXPERTS_PER_GROUP = 8
N_EXPERTS = N_GROUPS * EXPERTS_PER_GROUP
MOE_BLOCK = 256

NT_DIMS = (((1,), (1,)), ((), ()))
TN_DIMS = (((0,), (0,)), ((), ()))


def _params(sem, **kw):
    return pltpu.CompilerParams(dimension_semantics=sem, vmem_limit_bytes=VMEM_LIMIT, **kw)


def _full(shape):
    n = len(shape)
    return pl.BlockSpec(shape, lambda *_: (0,) * n)


def _inproj_kernel(x_ref, g_ref, wlatT_ref, wkr_ref, wh_ref, qlgc_ref, kvlgc_ref, kvlg_ref, wuqT_ref, wuk_ref,
                   wuvT_ref, qngc_ref, kng_ref, vonec_ref, cosT_ref, sinT_ref, cos_ref, sa_ref, sb_ref,
                   qT_out, k_out, vT_out, hq_out, fz_out, hi_out, hg_out, *, q_scale):
    x = x_ref[...]
    ms = jnp.mean(x * x, axis=-1, keepdims=True)
    n = (x * lax.rsqrt(ms + EPS) * g_ref[...]).astype(BF16)

    hw = hq_out.shape[1]
    hq_out[...] = jnp.dot(n, wh_ref[:, 0 * hw:1 * hw], preferred_element_type=F32).astype(BF16)
    fz_out[0] = jnp.dot(n, wh_ref[:, 1 * hw:2 * hw], preferred_element_type=F32).astype(BF16)
    fz_out[1] = jnp.dot(n, wh_ref[:, 2 * hw:3 * hw], preferred_element_type=F32).astype(BF16)
    hi_out[...] = jnp.dot(n, wh_ref[:, 3 * hw:4 * hw], preferred_element_type=F32).astype(BF16)
    hg_out[...] = jnp.dot(n, wh_ref[:, 4 * hw:5 * hw], preferred_element_type=F32).astype(BF16)

    ql = qlgc_ref.shape[0]
    kvl = kvlgc_ref.shape[0]
    half = MLA_ROPE // 2

    latT = lax.dot_general(wlatT_ref[...], n, NT_DIMS, preferred_element_type=F32)
    qlT = latT[0:ql]
    kvlT = latT[ql:ql + kvl]
    qnT = (qlT * lax.rsqrt(jnp.mean(qlT * qlT, axis=0, keepdims=True) + EPS) * qlgc_ref[...]).astype(BF16)
    kvnT = (kvlT * lax.rsqrt(jnp.mean(kvlT * kvlT, axis=0, keepdims=True) + EPS) * kvlgc_ref[...]).astype(BF16)
    q_allT = jnp.dot(wuqT_ref[...], qnT, preferred_element_type=F32)
    vT_out[...] = (jnp.dot(wuvT_ref[...], kvnT, preferred_element_type=F32) + vonec_ref[...]).astype(BF16)

    cosT = cosT_ref[...]
    sinT = sinT_ref[...]
    qngc = qngc_ref[...]
    for h in range(MLA_HEADS):
        t = q_allT[h * LANES:(h + 1) * LANES]
        tn = t * lax.rsqrt(jnp.sum(t * t, axis=0, keepdims=True) * (1.0 / MLA_QK) + EPS) * qngc
        x1 = tn[MLA_NOPE:MLA_NOPE + half]
        x2 = tn[MLA_NOPE + half:MLA_QK]
        rot = jnp.concatenate([tn[0:MLA_NOPE], x1 * cosT - x2 * sinT, x2 * cosT + x1 * sinT, tn[MLA_QK:LANES]],
                              axis=0)
        qT_out[h * LANES:(h + 1) * LANES, :] = (rot * q_scale).astype(BF16)

    latk = jnp.dot(n, wkr_ref[...], preferred_element_type=F32)
    kvl_r = latk[:, 0:kvl]
    kr = latk[:, kvl:kvl + LANES]
    kvn = (kvl_r * lax.rsqrt(jnp.mean(kvl_r * kvl_r, axis=-1, keepdims=True) + EPS) * kvlg_ref[...]).astype(BF16)
    k_all = jnp.dot(kvn, wuk_ref[...], preferred_element_type=F32)
    cos = cos_ref[...]
    sa = sa_ref[...]
    sb = sb_ref[...]
    kng = kng_ref[...]
    for h in range(MLA_HEADS):
        sl = slice(h * LANES, (h + 1) * LANES)
        t = k_all[:, sl] + kr
        tn = t * lax.rsqrt(jnp.sum(t * t, axis=-1, keepdims=True) * (1.0 / MLA_QK) + EPS) * kng
        up = pltpu.roll(tn, LANES - half, axis=1)
        dn = pltpu.roll(tn, half, axis=1)
        k_out[:, sl] = (tn * cos + up * sa + dn * sb).astype(BF16)


def _inproj(x2, seq, norm_mix, w_in, q_lat_norm, kv_lat_norm, w_uq, w_ukv, q_norm, k_norm, tm=512):
    T, D = x2.shape
    tm = min(tm, seq)
    H = MLA_HEADS
    ql, kvl = w_uq.shape[0], w_ukv.shape[0]
    hw = HG_HEADS * HG_DK
    o_kr = ql + kvl
    o_h = o_kr + MLA_ROPE
    zeros = lambda c: jnp.zeros((D, c), F32)
    w_latT = w_in[:, :o_kr].T.astype(BF16)
    w_kr = jnp.concatenate([w_in[:, ql:o_kr], zeros(MLA_NOPE), w_in[:, o_kr:o_h],
                            zeros(LANES - MLA_QK)], axis=1).astype(BF16)
    w_h = w_in[:, o_h:].astype(BF16)
    pad = LANES - MLA_QK
    wuqT = jnp.pad(w_uq.reshape(ql, H, MLA_QK), ((0, 0), (0, 0), (0, pad))).reshape(ql, H * LANES).T.astype(BF16)
    wkv = w_ukv.reshape(kvl, H, MLA_NOPE + MLA_V)
    wuk = jnp.pad(wkv[:, :, :MLA_NOPE], ((0, 0), (0, 0), (0, LANES - MLA_NOPE))).reshape(kvl, H * LANES).astype(BF16)
    wuvT = jnp.pad(wkv[:, :, MLA_NOPE:], ((0, 0), (0, 0), (0, LANES - MLA_V))).reshape(kvl, H * LANES).T.astype(BF16)
    vonec = jnp.tile(jnp.concatenate([jnp.zeros((MLA_V,), F32), jnp.ones((LANES - MLA_V,), F32)]), H)
    vonec = vonec.reshape(H * LANES, 1)
    qngc = jnp.pad(q_norm, (0, pad)).reshape(LANES, 1)
    kng = jnp.pad(k_norm, (0, pad)).reshape(1, LANES)

    half = MLA_ROPE // 2
    inv = 1.0 / (ROPE_THETA ** (jnp.arange(half, dtype=F32) / half))
    ang = jnp.arange(seq, dtype=F32)[:, None] * inv[None, :]
    cos, sin = jnp.cos(ang), jnp.sin(ang)
    z = lambda c: jnp.zeros((seq, c), F32)
    cos_t = jnp.concatenate([jnp.ones((seq, MLA_NOPE), F32), cos, cos, z(pad)], axis=1)
    sa_t = jnp.concatenate([z(MLA_NOPE), -sin, z(half), z(pad)], axis=1)
    sb_t = jnp.concatenate([z(MLA_NOPE), z(half), sin, z(pad)], axis=1)

    nseq = seq // tm
    row = lambda w: pl.BlockSpec((tm, w), lambda i: (i, 0))
    col = lambda r: pl.BlockSpec((r, tm), lambda i: (0, i))
    tab = pl.BlockSpec((tm, LANES), lambda i: (i % nseq, 0))
    tabT = pl.BlockSpec((half, tm), lambda i: (0, i % nseq))
    q_scale = (MLA_QK ** -0.5) * math.log2(math.e)
    outs = pl.pallas_call(
        functools.partial(_inproj_kernel, q_scale=q_scale),
        grid=(T // tm,),
        in_specs=[row(D), _full((1, D)), _full(w_latT.shape), _full(w_kr.shape), _full(w_h.shape),
                  _full((ql, 1)), _full((kvl, 1)), _full((1, kvl)), _full(wuqT.shape), _full(wuk.shape),
                  _full(wuvT.shape), _full((LANES, 1)), _full((1, LANES)), _full((H * LANES, 1)),
                  tabT, tabT, tab, tab, tab],
        out_specs=[col(H * LANES), row(H * LANES), col(H * LANES), row(hw),
                   pl.BlockSpec((2, tm, hw), lambda i: (0, i, 0)), row(hw), row(hw)],
        out_shape=[jax.ShapeDtypeStruct((H * LANES, T), BF16), jax.ShapeDtypeStruct((T, H * LANES), BF16),
                   jax.ShapeDtypeStruct((H * LANES, T), BF16), jax.ShapeDtypeStruct((T, hw), BF16),
                   jax.ShapeDtypeStruct((2, T, hw), BF16), jax.ShapeDtypeStruct((T, hw), BF16),
                   jax.ShapeDtypeStruct((T, hw), BF16)],
        compiler_params=_params(("parallel",)),
        name="inproj",
    )(x2, norm_mix.reshape(1, D), w_latT, w_kr, w_h, q_lat_norm.reshape(ql, 1), kv_lat_norm.reshape(kvl, 1),
      kv_lat_norm.reshape(1, kvl), wuqT, wuk, wuvT, qngc, kng, vonec, cos.T, sin.T, cos_t, sa_t, sb_t)
    return outs


def _attn_kernel(qT_ref, k_ref, vT_ref, o_ref, s00, s01, s10, s11, *, tk):
    tq = qT_ref.shape[1]
    nk = k_ref.shape[0] // tk
    qTs = [qT_ref[j * LANES:(j + 1) * LANES, :] for j in range(2)]
    s_bufs = ((s00, s01), (s10, s11))

    def scores(j, slot, c):
        r0 = pl.multiple_of(c * tk, tk)
        sT = jnp.dot(k_ref[pl.ds(r0, tk), j * LANES:(j + 1) * LANES], qTs[j], preferred_element_type=F32)
        s_bufs[j][slot][...] = sT
        return jnp.max(sT, axis=0, keepdims=True)

    def absorb(j, slot, c, m, acc, mx):
        r0 = pl.multiple_of(c * tk, tk)
        m_new = jnp.maximum(m, mx)
        pT = jnp.exp2(s_bufs[j][slot][...] - m_new).astype(BF16)
        pv = jnp.dot(vT_ref[j * LANES:(j + 1) * LANES, pl.ds(r0, tk)], pT, preferred_element_type=F32)
        return m_new, jnp.exp2(m - m_new) * acc + pv

    def step(c, slot, state, prefetch):
        new = []
        for j in range(2):
            m, acc, mx = state[j]
            m, acc = absorb(j, slot, c, m, acc, mx)
            mx = scores(j, 1 - slot, c + 1) if prefetch else mx
            new.append((m, acc, mx))
        return tuple(new)

    def pair(i, state):
        c = 2 * i
        return step(c + 1, 1, step(c, 0, state, True), True)

    state = tuple((jnp.full((1, tq), -jnp.inf, F32), jnp.zeros((LANES, tq), F32), scores(j, 0, 0))
                  for j in range(2))
    state = lax.fori_loop(0, nk // 2 - 1, pair, state)
    state = step(nk - 1, 1, step(nk - 2, 0, state, True), False)
    acc0, acc1 = state[0][1], state[1][1]
    oT = jnp.concatenate([acc0[0:MLA_V] / acc0[MLA_V:MLA_V + 1], acc1[0:MLA_V] / acc1[MLA_V:MLA_V + 1]], axis=0)
    o_ref[...] = oT.T.astype(o_ref.dtype)


def _attention(qT, k3, vT, tq=512, tk=512):
    B, S, _ = k3.shape
    tq, tk = min(tq, S), min(tk, S)
    nq = S // tq
    hp = MLA_HEADS // 2
    return pl.pallas_call(
        functools.partial(_attn_kernel, tk=tk),
        grid=(B, hp, nq),
        in_specs=[pl.BlockSpec((2 * LANES, tq), lambda b, h, i: (h, b * nq + i)),
                  pl.BlockSpec((None, S, 2 * LANES), lambda b, h, i: (b, 0, h)),
                  pl.BlockSpec((2 * LANES, S), lambda b, h, i: (h, b))],
        out_specs=pl.BlockSpec((None, tq, 2 * MLA_V), lambda b, h, i: (b, i, h)),
        out_shape=jax.ShapeDtypeStruct((B, S, MLA_HEADS * MLA_V), BF16),
        scratch_shapes=[pltpu.VMEM((tk, tq), F32)] * 4,
        compiler_params=_params(("parallel", "parallel", "arbitrary")),
        name="attention",
    )(qT, k3, vT)


def _hgrn_kernel(hq_ref, z_ref, hi_ref, lb_ref, tri_ref, o_ref, st_ref):
    d = pl.program_id(1)
    C = HG_CHUNK
    nc = hq_ref.shape[0] // C

    @pl.when(pl.program_id(2) == 0)
    def _():
        st_ref[...] = jnp.zeros_like(st_ref)

    one_m_lb = 1.0 - lb_ref[...]
    tri = tri_ref[...]
    keep = tri > 0
    fwd = d == 0

    def chunk(ci, carry):
        c = jnp.where(fwd, ci, nc - 1 - ci)
        r0 = pl.multiple_of(c * C, C)
        z = z_ref[pl.ds(r0, C), :].astype(F32)
        kk = one_m_lb * jax.nn.sigmoid(-z)
        g = jnp.log(1.0 - kk)
        g_hi = g.astype(BF16)
        g_lo = (g - g_hi.astype(F32)).astype(BF16)
        b = jnp.dot(tri, g_hi, preferred_element_type=F32) + jnp.dot(tri, g_lo, preferred_element_type=F32)
        b_edge = jnp.where(fwd, b[C - 1:C, :], b[0:1, :])
        hq = hq_ref[pl.ds(r0, C), :].astype(F32)
        qh = (hq * jax.nn.sigmoid(hq) * jnp.exp(b)).astype(BF16)
        kt = (kk * jnp.exp(-b)).astype(BF16)
        ks = (kk * jnp.exp(b_edge - b)).astype(BF16)
        dec = jnp.exp(b_edge)
        v = hi_ref[pl.ds(r0, C), :]
        for h in range(HG_HEADS):
            sl = slice(h * HG_DK, (h + 1) * HG_DK)
            a = lax.dot_general(qh[:, sl], kt[:, sl], NT_DIMS, preferred_element_type=F32)
            a = jnp.where(keep, a, 0.0).astype(BF16)
            st = st_ref[h]
            o = jnp.dot(a, v[:, sl], preferred_element_type=F32)
            o = o + lax.dot_general(qh[:, sl], st.astype(BF16), NT_DIMS, preferred_element_type=F32)
            o_ref[pl.ds(r0, C), sl] = o.astype(o_ref.dtype)
            st_ref[h] = st * dec[:, sl] + lax.dot_general(v[:, sl], ks[:, sl], TN_DIMS,
                                                          preferred_element_type=F32)
        return carry

    lax.fori_loop(0, nc, chunk, 0)


def _hgrn(hq3, fz4, hi3, lb, tl=512):
    B, S, W = hq3.shape
    tl = min(tl, S)
    nt = S // tl
    C = HG_CHUNK
    r = lax.broadcasted_iota(I32, (C, C), 0)
    c = lax.broadcasted_iota(I32, (C, C), 1)
    tri = jnp.stack([r >= c, r <= c]).astype(BF16)
    tile = lambda b, d, i: (b, i + d * (nt - 1 - 2 * i), 0)
    return pl.pallas_call(
        _hgrn_kernel,
        grid=(B, 2, nt),
        in_specs=[pl.BlockSpec((None, tl, W), tile),
                  pl.BlockSpec((None, None, tl, W), lambda b, d, i: (d, b, i + d * (nt - 1 - 2 * i), 0)),
                  pl.BlockSpec((None, tl, W), tile),
                  pl.BlockSpec((None, 1, W), lambda b, d, i: (d, 0, 0)),
                  pl.BlockSpec((None, C, C), lambda b, d, i: (d, 0, 0))],
        out_specs=pl.BlockSpec((None, None, tl, W), lambda b, d, i: (d, b, i + d * (nt - 1 - 2 * i), 0)),
        out_shape=jax.ShapeDtypeStruct((2, B, S, W), BF16),
        scratch_shapes=[pltpu.VMEM((HG_HEADS, HG_DV, HG_DK), F32)],
        compiler_params=_params(("parallel", "parallel", "arbitrary")),
        name="hgrn",
    )(hq3, fz4, hi3, lb, tri)


def _outproj_kernel(x_ref, a_ref, o_ref, hg_ref, ong_ref, wa_ref, wr_ref, g2_ref, wrt_ref, brt_ref,
                    h1_out, n2_out, route_out):
    o = o_ref[0].astype(F32) + o_ref[1].astype(F32)
    hg = hg_ref[...].astype(F32)
    gate = hg * jax.nn.sigmoid(hg)
    ong = ong_ref[...]
    parts = []
    for h in range(HG_HEADS):
        sl = slice(h * HG_DV, (h + 1) * HG_DV)
        oh = o[:, sl]
        parts.append((oh * lax.rsqrt(jnp.mean(oh * oh, axis=-1, keepdims=True) + EPS) * ong * gate[:, sl]).astype(BF16))
    r = jnp.concatenate(parts, axis=1)
    h1 = x_ref[...] + jnp.dot(a_ref[...], wa_ref[...], preferred_element_type=F32)
    h1 = h1 + jnp.dot(r, wr_ref[...], preferred_element_type=F32)
    h1_out[...] = h1
    n2 = h1 * lax.rsqrt(jnp.mean(h1 * h1, axis=-1, keepdims=True) + EPS) * g2_ref[...]
    n2_out[...] = n2

    logits = jnp.dot(n2, wrt_ref[...], preferred_element_type=F32, precision=lax.Precision.HIGHEST) + brt_ref[...]
    tm = logits.shape[0]
    lane = lax.broadcasted_iota(I32, (tm, LANES), 1)
    ninf = -jnp.inf
    is_g = lane < N_GROUPS
    gl = jnp.where(is_g, logits, ninf)
    gmax = jnp.max(gl, axis=-1, keepdims=True)
    gidx = jnp.min(jnp.where(gl == gmax, lane, LANES), axis=-1, keepdims=True)
    g_w = 1.0 / jnp.sum(jnp.where(is_g, jnp.exp(logits - gmax), 0.0), axis=-1, keepdims=True)
    lo = N_GROUPS + EXPERTS_PER_GROUP * gidx
    el = jnp.where((lane >= lo) & (lane < lo + EXPERTS_PER_GROUP), logits, ninf)
    m1 = jnp.max(el, axis=-1, keepdims=True)
    i1 = jnp.min(jnp.where(el == m1, lane, LANES), axis=-1, keepdims=True)
    el2 = jnp.where(lane == i1, ninf, el)
    m2 = jnp.max(el2, axis=-1, keepdims=True)
    i2 = jnp.min(jnp.where(el2 == m2, lane, LANES), axis=-1, keepdims=True)
    t = jnp.exp(m2 - m1)
    w1 = 1.0 / (1.0 + t)
    w2 = t / (1.0 + t)
    e1 = (i1 - N_GROUPS).astype(F32)
    e2 = (i2 - N_GROUPS).astype(F32)
    route = jnp.where(lane == 0, e1, jnp.where(lane == 1, e2, jnp.where(lane == 2, g_w * w1,
                      jnp.where(lane == 3, g_w * w2, 0.0))))
    route_out[...] = route


def _outproj(x2, a2, o3, hg2, hg_out_norm, w_out, norm_ffn, w_group, b_group, w_router, b_router, tm=512):
    T, D = x2.shape
    tm = min(tm, T)
    wa = w_out[:MLA_HEADS * MLA_V].astype(BF16)
    wr = w_out[MLA_HEADS * MLA_V:].astype(BF16)
    npad = LANES - N_GROUPS - N_EXPERTS
    wrt = jnp.concatenate([w_group, w_router, jnp.zeros((D, npad), F32)], axis=1)
    brt = jnp.concatenate([b_group, b_router, jnp.zeros((npad,), F32)]).reshape(1, LANES)
    row = lambda w: pl.BlockSpec((tm, w), lambda i: (i, 0))
    hw = HG_HEADS * HG_DV
    return pl.pallas_call(
        _outproj_kernel,
        grid=(T // tm,),
        in_specs=[row(D), row(a2.shape[1]), pl.BlockSpec((2, tm, hw), lambda i: (0, i, 0)), row(hw),
                  _full((1, HG_DV)), _full(wa.shape), _full(wr.shape), _full((1, D)), _full(wrt.shape),
                  _full((1, LANES))],
        out_specs=[row(D), row(D), row(LANES)],
        out_shape=[jax.ShapeDtypeStruct((T, D), F32), jax.ShapeDtypeStruct((T, D), F32),
                   jax.ShapeDtypeStruct((T, LANES), F32)],
        compiler_params=_params(("parallel",)),
        name="outproj",
    )(x2, a2, o3, hg2, hg_out_norm.reshape(1, HG_DV), wa, wr, norm_ffn.reshape(1, D), wrt, brt)


def _rank_kernel(route_ref, tri_ref, upper_ref, dest_out, pend_out, cnt_ref, base_ref):
    p = pl.program_id(0)
    i = pl.program_id(1)
    tm = route_ref.shape[0]
    lane = lax.broadcasted_iota(I32, (tm, LANES), 1)
    route = route_ref[...]
    e1 = route[:, 0:1].astype(I32)
    e2 = route[:, 1:2].astype(I32)
    is1 = lane == e1
    is2 = lane == e2
    onehot = jnp.where(is1 | is2, 1.0, 0.0)
    colsum = jnp.sum(onehot, axis=0, keepdims=True)

    @pl.when((p == 0) & (i == 0))
    def _():
        cnt_ref[...] = jnp.zeros_like(cnt_ref)

    @pl.when(p == 0)
    def _():
        cnt_ref[...] += colsum

    @pl.when((p == 1) & (i == 0))
    def _():
        cnt = cnt_ref[...]
        padded = jnp.ceil(cnt * (1.0 / MOE_BLOCK)) * MOE_BLOCK
        start = jnp.dot(jnp.broadcast_to(padded, (8, LANES)), upper_ref[...], preferred_element_type=F32,
                        precision=lax.Precision.HIGHEST)[0:1]
        base_ref[...] = start
        pend_out[...] = jnp.broadcast_to(start + padded, (8, LANES))

    @pl.when(p == 1)
    def _():
        before = jnp.dot(tri_ref[...], onehot.astype(BF16), preferred_element_type=F32)
        pos = base_ref[...] + before
        d1 = jnp.sum(jnp.where(is1, pos, 0.0), axis=-1, keepdims=True)
        d2 = jnp.sum(jnp.where(is2, pos, 0.0), axis=-1, keepdims=True)
        dest_out[...] = jnp.where(lane == 0, d1, jnp.where(lane == 1, d2, 0.0)).astype(I32)
        base_ref[...] += colsum


def _rank(route, tm=512):
    T = route.shape[0]
    tm = min(tm, T)
    r = lax.broadcasted_iota(I32, (tm, tm), 0)
    c = lax.broadcasted_iota(I32, (tm, tm), 1)
    tri = (r > c).astype(BF16)
    ru = lax.broadcasted_iota(I32, (LANES, LANES), 0)
    cu = lax.broadcasted_iota(I32, (LANES, LANES), 1)
    upper = (ru < cu).astype(F32)
    return pl.pallas_call(
        _rank_kernel,
        grid=(2, T // tm),
        in_specs=[pl.BlockSpec((tm, LANES), lambda p, i: (i, 0)), _full((tm, tm)), _full((LANES, LANES))],
        out_specs=[pl.BlockSpec((tm, LANES), lambda p, i: (i * p, 0)), _full((8, LANES))],
        out_shape=[jax.ShapeDtypeStruct((T, LANES), I32), jax.ShapeDtypeStruct((8, LANES), F32)],
        scratch_shapes=[pltpu.VMEM((1, LANES), F32), pltpu.VMEM((1, LANES), F32)],
        compiler_params=_params(("arbitrary", "arbitrary")),
        name="rank",
    )(route, tri, upper)


def _dispatch_kernel(dest_ref, n2_ref, xs_in, xs_out, sem):
    del xs_in
    tm = n2_ref.shape[0]

    def issue(r, carry):
        for k in range(2):
            dst = dest_ref[2 * r + k]
            pltpu.make_async_copy(n2_ref.at[pl.ds(r, 1)], xs_out.at[pl.ds(dst, 1)], sem).start()
        return carry

    lax.fori_loop(0, tm, issue, 0, unroll=8)
    for _ in range(2):
        pltpu.make_async_copy(n2_ref, xs_out.at[pl.ds(0, tm)], sem).wait()


def _dispatch(n2, dest_flat, n_rows, tm=512):
    T, D = n2.shape
    tm = min(tm, T)
    xs0 = jnp.zeros((n_rows, D), n2.dtype)
    return pl.pallas_call(
        _dispatch_kernel,
        grid=(T // tm,),
        in_specs=[pl.BlockSpec((2 * tm,), lambda i: (i,), memory_space=pltpu.SMEM),
                  pl.BlockSpec((tm, D), lambda i: (i, 0)),
                  pl.BlockSpec(memory_space=pl.ANY)],
        out_specs=pl.BlockSpec(memory_space=pl.ANY),
        out_shape=jax.ShapeDtypeStruct((n_rows, D), n2.dtype),
        scratch_shapes=[pltpu.SemaphoreType.DMA(())],
        input_output_aliases={2: 0},
        compiler_params=_params(("arbitrary",), has_side_effects=True),
        name="dispatch",
    )(dest_flat, n2, xs0)


def _expert_kernel(be_ref, nused_ref, x_ref, wg_ref, wu_ref, wd_ref, y_ref):
    i = pl.program_id(0)

    @pl.when(i < nused_ref[0])
    def _():
        x = x_ref[...].astype(BF16)
        g = jnp.dot(x, wg_ref[...], preferred_element_type=F32)
        u = jnp.dot(x, wu_ref[...], preferred_element_type=F32)
        hmid = (g * jax.nn.sigmoid(g) * u).astype(BF16)
        y_ref[...] = jnp.dot(hmid, wd_ref[...], preferred_element_type=F32)

    @pl.when(i >= nused_ref[0])
    def _():
        y_ref[...] = jnp.zeros_like(y_ref)


def _experts(xs, block_e, nused, w_gate, w_up, w_down):
    P, D = xs.shape
    nb = P // MOE_BLOCK
    de = w_gate.shape[2]
    grid_spec = pltpu.PrefetchScalarGridSpec(
        num_scalar_prefetch=2,
        grid=(nb,),
        in_specs=[pl.BlockSpec((MOE_BLOCK, D), lambda i, be, nu: (i, 0)),
                  pl.BlockSpec((None, D, de), lambda i, be, nu: (be[i], 0, 0)),
                  pl.BlockSpec((None, D, de), lambda i, be, nu: (be[i], 0, 0)),
                  pl.BlockSpec((None, de, D), lambda i, be, nu: (be[i], 0, 0))],
        out_specs=pl.BlockSpec((MOE_BLOCK, D), lambda i, be, nu: (i, 0)),
    )
    return pl.pallas_call(
        _expert_kernel,
        grid_spec=grid_spec,
        out_shape=jax.ShapeDtypeStruct((P, D), F32),
        compiler_params=_params(("arbitrary",)),
        name="experts",
    )(block_e, nused, xs, w_gate.astype(BF16), w_up.astype(BF16), w_down.astype(BF16))


def _combine_kernel(dest_ref, h1_ref, route_ref, ys_ref, out_ref, ybuf, sem):
    tm = h1_ref.shape[0]

    def issue(r, carry):
        for k in range(2):
            src = dest_ref[2 * r + k]
            pltpu.make_async_copy(ys_ref.at[pl.ds(src, 1)], ybuf.at[k, pl.ds(r, 1)], sem).start()
        return carry

    lax.fori_loop(0, tm, issue, 0, unroll=8)
    for k in range(2):
        pltpu.make_async_copy(ys_ref.at[pl.ds(0, tm)], ybuf.at[k], sem).wait()
    route = route_ref[...]
    out_ref[...] = h1_ref[...] + route[:, 2:3] * ybuf[0] + route[:, 3:4] * ybuf[1]


def _combine(h1, route, dest_flat, ys, tm=512):
    T, D = h1.shape
    tm = min(tm, T)
    return pl.pallas_call(
        _combine_kernel,
        grid=(T // tm,),
        in_specs=[pl.BlockSpec((2 * tm,), lambda i: (i,), memory_space=pltpu.SMEM),
                  pl.BlockSpec((tm, D), lambda i: (i, 0)),
                  pl.BlockSpec((tm, LANES), lambda i: (i, 0)),
                  pl.BlockSpec(memory_space=pl.ANY)],
        out_specs=pl.BlockSpec((tm, D), lambda i: (i, 0)),
        out_shape=jax.ShapeDtypeStruct((T, D), F32),
        scratch_shapes=[pltpu.VMEM((2, tm, D), F32), pltpu.SemaphoreType.DMA(())],
        compiler_params=_params(("arbitrary",)),
        name="combine",
    )(dest_flat, h1, route, ys)


def _mixers(x, norm_mix, w_in, q_lat_norm, kv_lat_norm, w_uq, w_ukv, q_norm, k_norm, lb_logits, layer):
    B, S, D = x.shape
    T = B * S
    q, k, v, hq, fz, hi, hg = _inproj(x.reshape(T, D), S, norm_mix, w_in, q_lat_norm, kv_lat_norm,
                                      w_uq, w_ukv, q_norm, k_norm)
    a = _attention(q, k.reshape(B, S, -1), v)
    lb = jnp.cumsum(jax.nn.softmax(lb_logits.astype(F32), axis=0), axis=0)[layer]
    hw = hq.shape[1]
    o = _hgrn(hq.reshape(B, S, hw), fz.reshape(2, B, S, hw), hi.reshape(B, S, hw), lb.reshape(2, 1, hw))
    return a.reshape(T, -1), o.reshape(2, T, hw), hg


def _moe(h1, n2, route, w_gate, w_up, w_down):
    T, D = h1.shape
    dest, pend = _rank(route)
    dest_flat = dest[:, :2].reshape(-1)
    n_rows = -(-(2 * T) // MOE_BLOCK) * MOE_BLOCK + N_EXPERTS * MOE_BLOCK
    nb = n_rows // MOE_BLOCK
    pend_i = pend[0, :N_EXPERTS].astype(I32)
    block_row0 = jnp.arange(nb, dtype=I32) * MOE_BLOCK
    block_e = jnp.minimum(jnp.sum((pend_i[None, :] <= block_row0[:, None]).astype(I32), axis=1), N_EXPERTS - 1)
    nused = (pend_i[N_EXPERTS - 1:] // MOE_BLOCK).astype(I32)
    xs = _dispatch(n2, dest_flat, n_rows)
    ys = _experts(xs, block_e, nused, w_gate, w_up, w_down)
    return _combine(h1, route, dest_flat, ys)


def kernel(x, norm_mix, w_in, q_lat_norm, kv_lat_norm, w_uq, w_ukv, q_norm, k_norm, lb_logits, hg_out_norm,
           w_out, norm_ffn, w_group, b_group, w_router, b_router, w_gate, w_up, w_down):
    B, S, D = x.shape
    h = x
    for l in range(norm_mix.shape[0]):
        h2 = h.reshape(B * S, D)
        a, o, hg = _mixers(h, norm_mix[l], w_in[l], q_lat_norm[l], kv_lat_norm[l], w_uq[l], w_ukv[l],
                           q_norm[l], k_norm[l], lb_logits, l)
        h1, n2, route = _outproj(h2, a, o, hg, hg_out_norm[l], w_out[l], norm_ffn[l], w_group[l], b_group[l],
                                 w_router[l], b_router[l])
        h = _moe(h1, n2, route, w_gate[l], w_up[l], w_down[l]).reshape(B, S, D)
    return h
```

```python
import functools
import math

import jax
import jax.numpy as jnp
from jax import lax
from jax.experimental import pallas as pl
from jax.experimental.pallas import tpu as pltpu

F32 = jnp.float32
BF16 = jnp.bfloat16
I32 = jnp.int32

EPS = 1e-6
LANES = 128
VMEM_LIMIT = 48 * 1024 * 1024

MLA_HEADS = 8
MLA_NOPE = 64
MLA_ROPE = 32
MLA_QK = MLA_NOPE + MLA_ROPE
MLA_V = 64
ROPE_THETA = 10000.0
HG_HEADS = 4
HG_DK = 128
HG_DV = 128
HG_CHUNK = 64
HG_SAFE_EXPONENT = 60.0
N_GROUPS = 4
EXPERTS_PER_GROUP = 8
N_EXPERTS = N_GROUPS * EXPERTS_PER_GROUP
MOE_BLOCK = 512

NT_DIMS = (((1,), (1,)), ((), ()))
TN_DIMS = (((0,), (0,)), ((), ()))


def _params(sem, **kw):
    return pltpu.CompilerParams(dimension_semantics=sem, vmem_limit_bytes=VMEM_LIMIT, **kw)


def _full(shape):
    n = len(shape)
    return pl.BlockSpec(shape, lambda *_: (0,) * n)


def _inproj_kernel(x_ref, g_ref, wlatT_ref, wkr_ref, wh_ref, qlgc_ref, kvlgc_ref, kvlg_ref, wuqT_ref, wuk_ref,
                   wuvT_ref, qngc_ref, kng_ref, vonec_ref, cosT_ref, sinT_ref, cos_ref, sa_ref, sb_ref,
                   qT_out, k_out, vT_out, hq_out, fz_out, hi_out, hg_out, *, q_scale):
    x = x_ref[...]
    ms = jnp.mean(x * x, axis=-1, keepdims=True)
    n = (x * lax.rsqrt(ms + EPS) * g_ref[...]).astype(BF16)

    hw = hq_out.shape[1]
    ql = qlgc_ref.shape[0]
    kvl = kvlgc_ref.shape[0]
    half = MLA_ROPE // 2

    def wide(j):
        return jnp.dot(n, wh_ref[:, j * hw:(j + 1) * hw], preferred_element_type=F32).astype(BF16)

    latT = lax.dot_general(wlatT_ref[...], n, NT_DIMS, preferred_element_type=F32)
    latk = jnp.dot(n, wkr_ref[...], preferred_element_type=F32)
    hq_out[...] = wide(0)
    fz_out[0] = wide(1)
    qlT = latT[0:ql]
    kvlT = latT[ql:ql + kvl]
    qnT = (qlT * lax.rsqrt(jnp.mean(qlT * qlT, axis=0, keepdims=True) + EPS) * qlgc_ref[...]).astype(BF16)
    kvnT = (kvlT * lax.rsqrt(jnp.mean(kvlT * kvlT, axis=0, keepdims=True) + EPS) * kvlgc_ref[...]).astype(BF16)
    q_allT = jnp.dot(wuqT_ref[...], qnT, preferred_element_type=F32)
    vT_out[...] = (jnp.dot(wuvT_ref[...], kvnT, preferred_element_type=F32) + vonec_ref[...]).astype(BF16)
    kvl_r = latk[:, 0:kvl]
    kr = latk[:, kvl:kvl + LANES]
    kvn = (kvl_r * lax.rsqrt(jnp.mean(kvl_r * kvl_r, axis=-1, keepdims=True) + EPS) * kvlg_ref[...]).astype(BF16)
    k_all = jnp.dot(kvn, wuk_ref[...], preferred_element_type=F32)
    fz_out[1] = wide(2)
    hi_out[...] = wide(3)
    hg_out[...] = wide(4)

    cosT = cosT_ref[...]
    sinT = sinT_ref[...]
    qngc = qngc_ref[...]
    for h in range(MLA_HEADS):
        t = q_allT[h * LANES:(h + 1) * LANES]
        tn = t * lax.rsqrt(jnp.sum(t * t, axis=0, keepdims=True) * (1.0 / MLA_QK) + EPS) * qngc
        x1 = tn[MLA_NOPE:MLA_NOPE + half]
        x2 = tn[MLA_NOPE + half:MLA_QK]
        rot = jnp.concatenate([tn[0:MLA_NOPE], x1 * cosT - x2 * sinT, x2 * cosT + x1 * sinT, tn[MLA_QK:LANES]],
                              axis=0)
        qT_out[h * LANES:(h + 1) * LANES, :] = (rot * q_scale).astype(BF16)

    kng = kng_ref[...]
    krg = kr * kng
    up = pltpu.roll(krg, LANES - half, axis=1)
    dn = pltpu.roll(krg, half, axis=1)
    kr_rot = krg * cos_ref[...] + up * sa_ref[...] + dn * sb_ref[...]
    kr_ssq = jnp.sum(kr * kr, axis=-1, keepdims=True)
    for h in range(MLA_HEADS):
        sl = slice(h * LANES, (h + 1) * LANES)
        t = k_all[:, sl]
        r = lax.rsqrt((jnp.sum(t * t, axis=-1, keepdims=True) + kr_ssq) * (1.0 / MLA_QK) + EPS)
        k_out[:, sl] = (r * (t * kng + kr_rot)).astype(BF16)


def _inproj(x2, seq, norm_mix, w_in, q_lat_norm, kv_lat_norm, w_uq, w_ukv, q_norm, k_norm, tm=512):
    T, D = x2.shape
    tm = min(tm, seq)
    H = MLA_HEADS
    ql, kvl = w_uq.shape[0], w_ukv.shape[0]
    hw = HG_HEADS * HG_DK
    o_kr = ql + kvl
    o_h = o_kr + MLA_ROPE
    zeros = lambda c: jnp.zeros((D, c), F32)
    w_latT = w_in[:, :o_kr].T.astype(BF16)
    w_kr = jnp.concatenate([w_in[:, ql:o_kr], zeros(MLA_NOPE), w_in[:, o_kr:o_h],
                            zeros(LANES - MLA_QK)], axis=1).astype(BF16)
    w_h = w_in[:, o_h:].astype(BF16)
    pad = LANES - MLA_QK
    wuqT = jnp.pad(w_uq.reshape(ql, H, MLA_QK), ((0, 0), (0, 0), (0, pad))).reshape(ql, H * LANES).T.astype(BF16)
    wkv = w_ukv.reshape(kvl, H, MLA_NOPE + MLA_V)
    wuk = jnp.pad(wkv[:, :, :MLA_NOPE], ((0, 0), (0, 0), (0, LANES - MLA_NOPE))).reshape(kvl, H * LANES).astype(BF16)
    wuvT = jnp.pad(wkv[:, :, MLA_NOPE:], ((0, 0), (0, 0), (0, LANES - MLA_V))).reshape(kvl, H * LANES).T.astype(BF16)
    vonec = jnp.tile(jnp.concatenate([jnp.zeros((MLA_V,), F32), jnp.ones((LANES - MLA_V,), F32)]), H)
    vonec = vonec.reshape(H * LANES, 1)
    qngc = jnp.pad(q_norm, (0, pad)).reshape(LANES, 1)
    kng = jnp.pad(k_norm, (0, pad)).reshape(1, LANES)

    half = MLA_ROPE // 2
    inv = 1.0 / (ROPE_THETA ** (jnp.arange(half, dtype=F32) / half))
    ang = jnp.arange(seq, dtype=F32)[:, None] * inv[None, :]
    cos, sin = jnp.cos(ang), jnp.sin(ang)
    z = lambda c: jnp.zeros((seq, c), F32)
    cos_t = jnp.concatenate([jnp.ones((seq, MLA_NOPE), F32), cos, cos, z(pad)], axis=1)
    sa_t = jnp.concatenate([z(MLA_NOPE), -sin, z(half), z(pad)], axis=1)
    sb_t = jnp.concatenate([z(MLA_NOPE), z(half), sin, z(pad)], axis=1)

    nseq = seq // tm
    row = lambda w: pl.BlockSpec((tm, w), lambda i: (i, 0))
    col = lambda r: pl.BlockSpec((r, tm), lambda i: (0, i))
    tab = pl.BlockSpec((tm, LANES), lambda i: (i % nseq, 0))
    tabT = pl.BlockSpec((half, tm), lambda i: (0, i % nseq))
    q_scale = (MLA_QK ** -0.5) * math.log2(math.e)
    outs = pl.pallas_call(
        functools.partial(_inproj_kernel, q_scale=q_scale),
        grid=(T // tm,),
        in_specs=[row(D), _full((1, D)), _full(w_latT.shape), _full(w_kr.shape), _full(w_h.shape),
                  _full((ql, 1)), _full((kvl, 1)), _full((1, kvl)), _full(wuqT.shape), _full(wuk.shape),
                  _full(wuvT.shape), _full((LANES, 1)), _full((1, LANES)), _full((H * LANES, 1)),
                  tabT, tabT, tab, tab, tab],
        out_specs=[col(H * LANES), row(H * LANES), col(H * LANES), row(hw),
                   pl.BlockSpec((2, tm, hw), lambda i: (0, i, 0)), row(hw), row(hw)],
        out_shape=[jax.ShapeDtypeStruct((H * LANES, T), BF16), jax.ShapeDtypeStruct((T, H * LANES), BF16),
                   jax.ShapeDtypeStruct((H * LANES, T), BF16), jax.ShapeDtypeStruct((T, hw), BF16),
                   jax.ShapeDtypeStruct((2, T, hw), BF16), jax.ShapeDtypeStruct((T, hw), BF16),
                   jax.ShapeDtypeStruct((T, hw), BF16)],
        compiler_params=_params(("parallel",)),
        name="inproj",
    )(x2, norm_mix.reshape(1, D), w_latT, w_kr, w_h, q_lat_norm.reshape(ql, 1), kv_lat_norm.reshape(kvl, 1),
      kv_lat_norm.reshape(1, kvl), wuqT, wuk, wuvT, qngc, kng, vonec, cos.T, sin.T, cos_t, sa_t, sb_t)
    return outs


def _attn_kernel(qT_ref, k_ref, vT_ref, o_ref, s00, s01, s10, s11, *, tk):
    tq = qT_ref.shape[1]
    nk = k_ref.shape[0] // tk
    qTs = [qT_ref[j * LANES:(j + 1) * LANES, :] for j in range(2)]
    s_bufs = ((s00, s01), (s10, s11))

    def scores(j, slot, c):
        r0 = pl.multiple_of(c * tk, tk)
        sT = jnp.dot(k_ref[pl.ds(r0, tk), j * LANES:(j + 1) * LANES], qTs[j], preferred_element_type=F32)
        s_bufs[j][slot][...] = sT
        return jnp.max(sT, axis=0, keepdims=True)

    def absorb(j, slot, c, m, acc, mx):
        r0 = pl.multiple_of(c * tk, tk)
        m_new = jnp.maximum(m, mx)
        pT = jnp.exp2(s_bufs[j][slot][...] - m_new).astype(BF16)
        pv = jnp.dot(vT_ref[j * LANES:(j + 1) * LANES, pl.ds(r0, tk)], pT, preferred_element_type=F32)
        return m_new, jnp.exp2(m - m_new) * acc + pv

    def step(c, slot, state, prefetch):
        mx_next = [scores(j, 1 - slot, c + 1) if prefetch else state[j][2] for j in range(2)]
        new = []
        for j in range(2):
            m, acc, mx = state[j]
            m, acc = absorb(j, slot, c, m, acc, mx)
            new.append((m, acc, mx_next[j]))
        return tuple(new)

    def pair(i, state):
        c = 2 * i
        return step(c + 1, 1, step(c, 0, state, True), True)

    state = tuple((jnp.full((1, tq), -jnp.inf, F32), jnp.zeros((LANES, tq), F32), scores(j, 0, 0))
                  for j in range(2))
    state = lax.fori_loop(0, nk // 2 - 1, pair, state)
    state = step(nk - 1, 1, step(nk - 2, 0, state, True), False)
    acc0, acc1 = state[0][1], state[1][1]
    oT = jnp.concatenate([acc0[0:MLA_V] / acc0[MLA_V:MLA_V + 1], acc1[0:MLA_V] / acc1[MLA_V:MLA_V + 1]], axis=0)
    o_ref[...] = oT.T.astype(o_ref.dtype)


def _attention(qT, k3, vT, tq=512, tk=512):
    B, S, _ = k3.shape
    tq, tk = min(tq, S), min(tk, S)
    nq = S // tq
    hp = MLA_HEADS // 2
    return pl.pallas_call(
        functools.partial(_attn_kernel, tk=tk),
        grid=(B, hp, nq),
        in_specs=[pl.BlockSpec((2 * LANES, tq), lambda b, h, i: (h, b * nq + i)),
                  pl.BlockSpec((None, S, 2 * LANES), lambda b, h, i: (b, 0, h)),
                  pl.BlockSpec((2 * LANES, S), lambda b, h, i: (h, b))],
        out_specs=pl.BlockSpec((None, tq, 2 * MLA_V), lambda b, h, i: (b, i, h)),
        out_shape=jax.ShapeDtypeStruct((B, S, MLA_HEADS * MLA_V), BF16),
        scratch_shapes=[pltpu.VMEM((tk, tq), F32)] * 4,
        compiler_params=_params(("parallel", "parallel", "arbitrary")),
        name="attention",
    )(qT, k3, vT)


def _hgrn_kernel(hq_ref, z_ref, hi_ref, lb_ref, tri_ref, o_ref, st_ref, kk_scr, b_scr, edge_scr, qh_scr, a_scr,
                 u_scr, qf_scr, of_scr):
    d = pl.program_id(1)
    C = HG_CHUNK
    tl = hq_ref.shape[0]
    nc = tl // C
    heads = [slice(h * HG_DK, (h + 1) * HG_DK) for h in range(HG_HEADS)]

    @pl.when(pl.program_id(2) == 0)
    def _():
        st_ref[...] = jnp.zeros_like(st_ref)

    one_m_lb = 1.0 - lb_ref[...]
    tri = tri_ref[...]
    keep = tri > 0
    fwd = d == 0

    worst = jnp.zeros_like(one_m_lb)
    for c in range(nc):
        rows = slice(c * C, (c + 1) * C)
        kk = one_m_lb * jax.nn.sigmoid(-z_ref[rows, :].astype(F32))
        g = jnp.log(1.0 - kk)
        g_hi = g.astype(BF16)
        g_lo = (g - g_hi.astype(F32)).astype(BF16)
        b = jnp.dot(tri, g_hi, preferred_element_type=F32) + jnp.dot(tri, g_lo, preferred_element_type=F32)
        b_edge = jnp.where(fwd, b[C - 1:C, :], b[0:1, :])
        kk_scr[rows, :] = kk
        b_scr[rows, :] = b
        edge_scr[c] = b_edge
        worst = jnp.maximum(worst, -b_edge)
    safe = jnp.max(worst) < HG_SAFE_EXPONENT

    @pl.when(safe)
    def _():
        for c in range(nc):
            rows = slice(c * C, (c + 1) * C)
            kk = kk_scr[rows, :]
            b = b_scr[rows, :]
            hq = hq_ref[rows, :].astype(F32)
            qh = (hq * jax.nn.sigmoid(hq) * jnp.exp(b)).astype(BF16)
            kt = (kk * jnp.exp(-b)).astype(BF16)
            ks = (kk * jnp.exp(edge_scr[c] - b)).astype(BF16)
            v = hi_ref[rows, :]
            qh_scr[rows, :] = qh
            for h, sl in enumerate(heads):
                a = lax.dot_general(qh[:, sl], kt[:, sl], NT_DIMS, preferred_element_type=F32)
                a_scr[c, h] = jnp.where(keep, a, 0.0).astype(BF16)
                u_scr[c, h] = lax.dot_general(v[:, sl], ks[:, sl], TN_DIMS, preferred_element_type=F32)
        st = [st_ref[h] for h in range(HG_HEADS)]
        for p in range(nc):
            c = jnp.where(fwd, p, nc - 1 - p)
            r0 = pl.multiple_of(c * C, C)
            dec = jnp.exp(edge_scr[c])
            qh = qh_scr[pl.ds(r0, C), :]
            v = hi_ref[pl.ds(r0, C), :]
            for h, sl in enumerate(heads):
                o = jnp.dot(a_scr[c, h], v[:, sl], preferred_element_type=F32)
                o = o + lax.dot_general(qh[:, sl], st[h].astype(BF16), NT_DIMS, preferred_element_type=F32)
                o_ref[pl.ds(r0, C), sl] = o.astype(o_ref.dtype)
                st[h] = st[h] * dec[:, sl] + u_scr[c, h]
        for h in range(HG_HEADS):
            st_ref[h] = st[h]

    @pl.when(jnp.logical_not(safe))
    def _():
        hq = hq_ref[...].astype(F32)
        qf_scr[...] = hq * jax.nn.sigmoid(hq)
        first = lax.broadcasted_iota(I32, (16, HG_DK), 0) == 0

        def row(i, carry):
            t = jnp.where(fwd, i, tl - 1 - i)
            kk = kk_scr[pl.ds(t, 1), :]
            f = 1.0 - kk
            q = qf_scr[pl.ds(t, 1), :]
            g0 = pl.multiple_of((t // 16) * 16, 16)
            v = hi_ref[pl.ds(g0, 16), :].astype(F32)
            v = jnp.sum(jnp.where(lax.broadcasted_iota(I32, v.shape, 0) == t % 16, v, 0.0), axis=0, keepdims=True)
            outs = []
            for h, sl in enumerate(heads):
                pad = lambda x: jnp.where(first, jnp.broadcast_to(x[:, sl], (16, HG_DK)), 0.0).astype(BF16)
                st = st_ref[h] * f[:, sl] + lax.dot_general(pad(v), pad(kk), TN_DIMS, preferred_element_type=F32)
                st_ref[h] = st
                outs.append(lax.dot_general(pad(q), st.astype(BF16), NT_DIMS, preferred_element_type=F32)[0:1])
            of_scr[pl.ds(t, 1), :] = jnp.concatenate(outs, axis=1)
            return carry

        lax.fori_loop(0, tl, row, 0)
        o_ref[...] = of_scr[...].astype(o_ref.dtype)


def _hgrn(hq3, fz4, hi3, lb, tl=512):
    B, S, W = hq3.shape
    tl = min(tl, S)
    nt = S // tl
    C = HG_CHUNK
    r = lax.broadcasted_iota(I32, (C, C), 0)
    c = lax.broadcasted_iota(I32, (C, C), 1)
    tri = jnp.stack([r >= c, r <= c]).astype(BF16)
    tile = lambda b, d, i: (b, i + d * (nt - 1 - 2 * i), 0)
    return pl.pallas_call(
        _hgrn_kernel,
        grid=(B, 2, nt),
        in_specs=[pl.BlockSpec((None, tl, W), tile),
                  pl.BlockSpec((None, None, tl, W), lambda b, d, i: (d, b, i + d * (nt - 1 - 2 * i), 0)),
                  pl.BlockSpec((None, tl, W), tile),
                  pl.BlockSpec((None, 1, W), lambda b, d, i: (d, 0, 0)),
                  pl.BlockSpec((None, C, C), lambda b, d, i: (d, 0, 0))],
        out_specs=pl.BlockSpec((None, None, tl, W), lambda b, d, i: (d, b, i + d * (nt - 1 - 2 * i), 0)),
        out_shape=jax.ShapeDtypeStruct((2, B, S, W), BF16),
        scratch_shapes=[pltpu.VMEM((HG_HEADS, HG_DV, HG_DK), F32),
                        pltpu.VMEM((tl, W), F32), pltpu.VMEM((tl, W), F32),
                        pltpu.VMEM((tl // C, 1, W), F32),
                        pltpu.VMEM((tl, W), BF16),
                        pltpu.VMEM((tl // C, HG_HEADS, C, C), BF16),
                        pltpu.VMEM((tl // C, HG_HEADS, HG_DV, HG_DK), F32),
                        pltpu.VMEM((tl, W), F32), pltpu.VMEM((tl, W), F32)],
        compiler_params=_params(("parallel", "parallel", "arbitrary")),
        name="hgrn",
    )(hq3, fz4, hi3, lb, tri)


def _outproj_kernel(x_ref, a_ref, o_ref, hg_ref, ong_ref, wa_ref, wr_ref, g2_ref, wrt_ref, brt_ref,
                    h1_out, n2_out, route_out):
    o = o_ref[0].astype(F32) + o_ref[1].astype(F32)
    hg = hg_ref[...].astype(F32)
    gate = hg * jax.nn.sigmoid(hg)
    ong = ong_ref[...]
    parts = []
    for h in range(HG_HEADS):
        sl = slice(h * HG_DV, (h + 1) * HG_DV)
        oh = o[:, sl]
        parts.append((oh * lax.rsqrt(jnp.mean(oh * oh, axis=-1, keepdims=True) + EPS) * ong * gate[:, sl]).astype(BF16))
    r = jnp.concatenate(parts, axis=1)
    h1 = x_ref[...] + jnp.dot(a_ref[...], wa_ref[...], preferred_element_type=F32)
    h1 = h1 + jnp.dot(r, wr_ref[...], preferred_element_type=F32)
    h1_out[...] = h1
    n2 = h1 * lax.rsqrt(jnp.mean(h1 * h1, axis=-1, keepdims=True) + EPS) * g2_ref[...]
    n2_out[...] = n2

    n2_hi = n2.astype(BF16)
    n2_lo = (n2 - n2_hi.astype(F32)).astype(BF16)
    l_hi = jnp.dot(n2_hi, wrt_ref[...], preferred_element_type=F32)
    l_lo = jnp.dot(n2_lo, wrt_ref[:, 0:LANES], preferred_element_type=F32)
    logits = l_hi[:, 0:LANES] + l_hi[:, LANES:2 * LANES] + l_lo + brt_ref[...]
    tm = logits.shape[0]
    lane = lax.broadcasted_iota(I32, (tm, LANES), 1)
    ninf = -jnp.inf
    is_g = lane < N_GROUPS
    gl = jnp.where(is_g, logits, ninf)
    gmax = jnp.max(gl, axis=-1, keepdims=True)
    gidx = jnp.min(jnp.where(gl == gmax, lane, LANES), axis=-1, keepdims=True)
    g_w = 1.0 / jnp.sum(jnp.where(is_g, jnp.exp(logits - gmax), 0.0), axis=-1, keepdims=True)
    lo = N_GROUPS + EXPERTS_PER_GROUP * gidx
    el = jnp.where((lane >= lo) & (lane < lo + EXPERTS_PER_GROUP), logits, ninf)
    m1 = jnp.max(el, axis=-1, keepdims=True)
    i1 = jnp.min(jnp.where(el == m1, lane, LANES), axis=-1, keepdims=True)
    el2 = jnp.where(lane == i1, ninf, el)
    m2 = jnp.max(el2, axis=-1, keepdims=True)
    i2 = jnp.min(jnp.where(el2 == m2, lane, LANES), axis=-1, keepdims=True)
    t = jnp.exp(m2 - m1)
    w1 = 1.0 / (1.0 + t)
    w2 = t / (1.0 + t)
    e1 = (i1 - N_GROUPS).astype(F32)
    e2 = (i2 - N_GROUPS).astype(F32)
    route = jnp.where(lane == 0, e1, jnp.where(lane == 1, e2, jnp.where(lane == 2, g_w * w1,
                      jnp.where(lane == 3, g_w * w2, 0.0))))
    route_out[...] = route


def _outproj(x2, a2, o3, hg2, hg_out_norm, w_out, norm_ffn, w_group, b_group, w_router, b_router, tm=512):
    T, D = x2.shape
    tm = min(tm, T)
    wa = w_out[:MLA_HEADS * MLA_V].astype(BF16)
    wr = w_out[MLA_HEADS * MLA_V:].astype(BF16)
    npad = LANES - N_GROUPS - N_EXPERTS
    wrt = jnp.concatenate([w_group, w_router, jnp.zeros((D, npad), F32)], axis=1)
    wrt_hi = wrt.astype(BF16)
    wrt = jnp.concatenate([wrt_hi, (wrt - wrt_hi.astype(F32)).astype(BF16)], axis=1)
    brt = jnp.concatenate([b_group, b_router, jnp.zeros((npad,), F32)]).reshape(1, LANES)
    row = lambda w: pl.BlockSpec((tm, w), lambda i: (i, 0))
    hw = HG_HEADS * HG_DV
    return pl.pallas_call(
        _outproj_kernel,
        grid=(T // tm,),
        in_specs=[row(D), row(a2.shape[1]), pl.BlockSpec((2, tm, hw), lambda i: (0, i, 0)), row(hw),
                  _full((1, HG_DV)), _full(wa.shape), _full(wr.shape), _full((1, D)), _full(wrt.shape),
                  _full((1, LANES))],
        out_specs=[row(D), row(D), row(LANES)],
        out_shape=[jax.ShapeDtypeStruct((T, D), F32), jax.ShapeDtypeStruct((T, D), F32),
                   jax.ShapeDtypeStruct((T, LANES), F32)],
        compiler_params=_params(("parallel",)),
        name="outproj",
    )(x2, a2, o3, hg2, hg_out_norm.reshape(1, HG_DV), wa, wr, norm_ffn.reshape(1, D), wrt, brt)


def _rank_kernel(route_ref, tri_ref, upper_ref, dest_out, pend_out, cnt_ref, base_ref):
    p = pl.program_id(0)
    i = pl.program_id(1)
    tm = route_ref.shape[0]
    lane = lax.broadcasted_iota(I32, (tm, LANES), 1)
    route = route_ref[...]
    e1 = route[:, 0:1].astype(I32)
    e2 = route[:, 1:2].astype(I32)
    is1 = lane == e1
    is2 = lane == e2
    onehot = jnp.where(is1 | is2, 1.0, 0.0)
    colsum = jnp.sum(onehot, axis=0, keepdims=True)

    @pl.when((p == 0) & (i == 0))
    def _():
        cnt_ref[...] = jnp.zeros_like(cnt_ref)

    @pl.when(p == 0)
    def _():
        cnt_ref[...] += colsum

    @pl.when((p == 1) & (i == 0))
    def _():
        cnt = cnt_ref[...]
        padded = jnp.ceil(cnt * (1.0 / MOE_BLOCK)) * MOE_BLOCK
        start = jnp.dot(jnp.broadcast_to(padded, (8, LANES)), upper_ref[...], preferred_element_type=F32,
                        precision=lax.Precision.HIGHEST)[0:1]
        base_ref[...] = start
        pend_out[...] = jnp.broadcast_to(start + padded, (8, LANES))

    @pl.when(p == 1)
    def _():
        before = jnp.dot(tri_ref[...], onehot.astype(BF16), preferred_element_type=F32)
        pos = base_ref[...] + before
        d1 = jnp.sum(jnp.where(is1, pos, 0.0), axis=-1, keepdims=True)
        d2 = jnp.sum(jnp.where(is2, pos, 0.0), axis=-1, keepdims=True)
        dest_out[...] = jnp.where(lane == 0, d1, jnp.where(lane == 1, d2, 0.0)).astype(I32)
        base_ref[...] += colsum


def _rank(route, tm=512):
    T = route.shape[0]
    tm = min(tm, T)
    r = lax.broadcasted_iota(I32, (tm, tm), 0)
    c = lax.broadcasted_iota(I32, (tm, tm), 1)
    tri = (r > c).astype(BF16)
    ru = lax.broadcasted_iota(I32, (LANES, LANES), 0)
    cu = lax.broadcasted_iota(I32, (LANES, LANES), 1)
    upper = (ru < cu).astype(F32)
    return pl.pallas_call(
        _rank_kernel,
        grid=(2, T // tm),
        in_specs=[pl.BlockSpec((tm, LANES), lambda p, i: (i, 0)), _full((tm, tm)), _full((LANES, LANES))],
        out_specs=[pl.BlockSpec((tm, LANES), lambda p, i: (i * p, 0)), _full((8, LANES))],
        out_shape=[jax.ShapeDtypeStruct((T, LANES), I32), jax.ShapeDtypeStruct((8, LANES), F32)],
        scratch_shapes=[pltpu.VMEM((1, LANES), F32), pltpu.VMEM((1, LANES), F32)],
        compiler_params=_params(("arbitrary", "arbitrary")),
        name="rank",
    )(route, tri, upper)


def _dispatch_kernel(dest_ref, n2_ref, xs_in, xs_out, sem):
    del xs_in
    tm = n2_ref.shape[0]

    def issue(r, carry):
        for k in range(2):
            dst = dest_ref[2 * r + k]
            pltpu.make_async_copy(n2_ref.at[pl.ds(r, 1)], xs_out.at[pl.ds(dst, 1)], sem).start(priority=k)
        return carry

    lax.fori_loop(0, tm, issue, 0, unroll=8)
    for _ in range(2):
        pltpu.make_async_copy(n2_ref, xs_out.at[pl.ds(0, tm)], sem).wait()


def _dispatch(n2, dest_flat, n_rows, tm=512):
    T, D = n2.shape
    tm = min(tm, T)
    xs0 = jnp.zeros((n_rows, D), n2.dtype)
    return pl.pallas_call(
        _dispatch_kernel,
        grid=(T // tm,),
        in_specs=[pl.BlockSpec((2 * tm,), lambda i: (i,), memory_space=pltpu.SMEM),
                  pl.BlockSpec((tm, D), lambda i: (i, 0)),
                  pl.BlockSpec(memory_space=pl.ANY)],
        out_specs=pl.BlockSpec(memory_space=pl.ANY),
        out_shape=jax.ShapeDtypeStruct((n_rows, D), n2.dtype),
        scratch_shapes=[pltpu.SemaphoreType.DMA(())],
        input_output_aliases={2: 0},
        compiler_params=_params(("arbitrary",), has_side_effects=True),
        name="dispatch",
    )(dest_flat, n2, xs0)


def _expert_kernel(be_ref, nused_ref, x_ref, wgu_ref, wd_ref, y_ref):
    i = pl.program_id(0)
    de = wd_ref.shape[0]
    half = x_ref.shape[0] // 2

    @pl.when(i < nused_ref[0])
    def _():
        gu = [jnp.dot(x_ref[r * half:(r + 1) * half, :].astype(BF16), wgu_ref[...], preferred_element_type=F32)
              for r in range(2)]
        for r in range(2):
            g, u = gu[r][:, 0:de], gu[r][:, de:2 * de]
            hmid = (g * jax.nn.sigmoid(g) * u).astype(BF16)
            y_ref[r * half:(r + 1) * half, :] = jnp.dot(hmid, wd_ref[...], preferred_element_type=F32)

    @pl.when(i >= nused_ref[0])
    def _():
        y_ref[...] = jnp.zeros_like(y_ref)


def _experts(xs, block_e, nused, w_gate, w_up, w_down):
    P, D = xs.shape
    nb = P // MOE_BLOCK
    de = w_gate.shape[2]
    grid_spec = pltpu.PrefetchScalarGridSpec(
        num_scalar_prefetch=2,
        grid=(nb,),
        in_specs=[pl.BlockSpec((MOE_BLOCK, D), lambda i, be, nu: (i, 0)),
                  pl.BlockSpec((None, D, 2 * de), lambda i, be, nu: (be[i], 0, 0)),
                  pl.BlockSpec((None, de, D), lambda i, be, nu: (be[i], 0, 0))],
        out_specs=pl.BlockSpec((MOE_BLOCK, D), lambda i, be, nu: (i, 0)),
    )
    return pl.pallas_call(
        _expert_kernel,
        grid_spec=grid_spec,
        out_shape=jax.ShapeDtypeStruct((P, D), F32),
        compiler_params=_params(("arbitrary",)),
        name="experts",
    )(block_e, nused, xs, jnp.concatenate([w_gate, w_up], axis=2).astype(BF16), w_down.astype(BF16))


def _combine_kernel(dest_ref, dest_next_ref, h1_ref, route_ref, ys_ref, out_ref, ybuf, sems):
    i = pl.program_id(0)
    nt = pl.num_programs(0)
    tm = h1_ref.shape[0]
    slot = i % 2

    def gather(idx_ref, s):
        def issue(r, carry):
            for k in range(2):
                src = idx_ref[2 * r + k]
                pltpu.make_async_copy(ys_ref.at[pl.ds(src, 1)], ybuf.at[s, k, pl.ds(r, 1)],
                                      sems.at[s]).start(priority=k)
            return carry
        lax.fori_loop(0, tm, issue, 0, unroll=8)

    @pl.when(i == 0)
    def _():
        gather(dest_ref, 0)

    @pl.when(i + 1 < nt)
    def _():
        gather(dest_next_ref, 1 - slot)

    for k in range(2):
        pltpu.make_async_copy(ys_ref.at[pl.ds(0, tm)], ybuf.at[slot, k], sems.at[slot]).wait()
    route = route_ref[...]
    out_ref[...] = h1_ref[...] + route[:, 2:3] * ybuf[slot, 0] + route[:, 3:4] * ybuf[slot, 1]


def _combine(h1, route, dest_flat, ys, tm=512):
    T, D = h1.shape
    tm = min(tm, T)
    nt = T // tm
    return pl.pallas_call(
        _combine_kernel,
        grid=(nt,),
        in_specs=[pl.BlockSpec((2 * tm,), lambda i: (i,), memory_space=pltpu.SMEM),
                  pl.BlockSpec((2 * tm,), lambda i: (jnp.minimum(i + 1, nt - 1),), memory_space=pltpu.SMEM),
                  pl.BlockSpec((tm, D), lambda i: (i, 0)),
                  pl.BlockSpec((tm, LANES), lambda i: (i, 0)),
                  pl.BlockSpec(memory_space=pl.ANY)],
        out_specs=pl.BlockSpec((tm, D), lambda i: (i, 0)),
        out_shape=jax.ShapeDtypeStruct((T, D), F32),
        scratch_shapes=[pltpu.VMEM((2, 2, tm, D), F32), pltpu.SemaphoreType.DMA((2,))],
        compiler_params=_params(("arbitrary",)),
        name="combine",
    )(dest_flat, dest_flat, h1, route, ys)


def _mixers(x, norm_mix, w_in, q_lat_norm, kv_lat_norm, w_uq, w_ukv, q_norm, k_norm, lb_logits, layer):
    B, S, D = x.shape
    T = B * S
    q, k, v, hq, fz, hi, hg = _inproj(x.reshape(T, D), S, norm_mix, w_in, q_lat_norm, kv_lat_norm,
                                      w_uq, w_ukv, q_norm, k_norm)
    a = _attention(q, k.reshape(B, S, -1), v)
    lb = jnp.cumsum(jax.nn.softmax(lb_logits.astype(F32), axis=0), axis=0)[layer]
    hw = hq.shape[1]
    o = _hgrn(hq.reshape(B, S, hw), fz.reshape(2, B, S, hw), hi.reshape(B, S, hw), lb.reshape(2, 1, hw))
    return a.reshape(T, -1), o.reshape(2, T, hw), hg


def _moe(h1, n2, route, w_gate, w_up, w_down):
    T, D = h1.shape
    dest, pend = _rank(route)
    dest_flat = dest[:, :2].reshape(-1)
    n_rows = -(-(2 * T) // MOE_BLOCK) * MOE_BLOCK + N_EXPERTS * MOE_BLOCK
    nb = n_rows // MOE_BLOCK
    pend_i = pend[0, :N_EXPERTS].astype(I32)
    block_row0 = jnp.arange(nb, dtype=I32) * MOE_BLOCK
    block_e = jnp.minimum(jnp.sum((pend_i[None, :] <= block_row0[:, None]).astype(I32), axis=1), N_EXPERTS - 1)
    nused = (pend_i[N_EXPERTS - 1:] // MOE_BLOCK).astype(I32)
    xs = _dispatch(n2, dest_flat, n_rows)
    ys = _experts(xs, block_e, nused, w_gate, w_up, w_down)
    return _combine(h1, route, dest_flat, ys)


def kernel(x, norm_mix, w_in, q_lat_norm, kv_lat_norm, w_uq, w_ukv, q_norm, k_norm, lb_logits, hg_out_norm,
           w_out, norm_ffn, w_group, b_group, w_router, b_router, w_gate, w_up, w_down):
    B, S, D = x.shape
    h = x
    for l in range(norm_mix.shape[0]):
        h2 = h.reshape(B * S, D)
        a, o, hg = _mixers(h, norm_mix[l], w_in[l], q_lat_norm[l], kv_lat_norm[l], w_uq[l], w_ukv[l],
                           q_norm[l], k_norm[l], lb_logits, l)
        h1, n2, route = _outproj(h2, a, o, hg, hg_out_norm[l], w_out[l], norm_ffn[l], w_group[l], b_group[l],
                                 w_router[l], b_router[l])
        h = _moe(h1, n2, route, w_gate[l], w_up[l], w_down[l]).reshape(B, S, D)
    return h
```

```python
import functools
import math

import jax
import jax.numpy as jnp
from jax import lax
from jax.experimental import pallas as pl
from jax.experimental.pallas import tpu as pltpu

F32 = jnp.float32
BF16 = jnp.bfloat16
I32 = jnp.int32

EPS = 1e-6
LANES = 128
VMEM_LIMIT = 48 * 1024 * 1024

MLA_HEADS = 8
MLA_NOPE = 64
MLA_ROPE = 32
MLA_QK = MLA_NOPE + MLA_ROPE
MLA_V = 64
ROPE_THETA = 10000.0
HG_HEADS = 4
HG_DK = 128
HG_DV = 128
HG_CHUNK = 64
HG_SAFE_EXPONENT = 60.0
N_GROUPS = 4
EXPERTS_PER_GROUP = 8
N_EXPERTS = N_GROUPS * EXPERTS_PER_GROUP
MOE_BLOCK = 512
RUN_ROWS = 8

NT_DIMS = (((1,), (1,)), ((), ()))
TN_DIMS = (((0,), (0,)), ((), ()))


def _params(sem, **kw):
    return pltpu.CompilerParams(dimension_semantics=sem, vmem_limit_bytes=VMEM_LIMIT, **kw)


def _full(shape):
    n = len(shape)
    return pl.BlockSpec(shape, lambda *_: (0,) * n)


def _inproj_kernel(x_ref, g_ref, wlatT_ref, wkr_ref, wh_ref, qlgc_ref, kvlgc_ref, kvlg_ref, wuqT_ref, wuk_ref,
                   wuvT_ref, qngc_ref, kng_ref, vonec_ref, cosT_ref, sinT_ref, cos_ref, sa_ref, sb_ref,
                   qT_out, k_out, vT_out, hq_out, fz_out, hi_out, hg_out, *, q_scale):
    x = x_ref[...]
    ms = jnp.mean(x * x, axis=-1, keepdims=True)
    n = (x * lax.rsqrt(ms + EPS) * g_ref[...]).astype(BF16)

    hw = hq_out.shape[1]
    ql = qlgc_ref.shape[0]
    kvl = kvlgc_ref.shape[0]
    half = MLA_ROPE // 2

    def wide(j):
        return jnp.dot(n, wh_ref[:, j * hw:(j + 1) * hw], preferred_element_type=F32).astype(BF16)

    latT = lax.dot_general(wlatT_ref[...], n, NT_DIMS, preferred_element_type=F32)
    latk = jnp.dot(n, wkr_ref[...], preferred_element_type=F32)
    hq_out[...] = wide(0)
    fz_out[0] = wide(1)
    qlT = latT[0:ql]
    kvlT = latT[ql:ql + kvl]
    qnT = (qlT * lax.rsqrt(jnp.mean(qlT * qlT, axis=0, keepdims=True) + EPS) * qlgc_ref[...]).astype(BF16)
    kvnT = (kvlT * lax.rsqrt(jnp.mean(kvlT * kvlT, axis=0, keepdims=True) + EPS) * kvlgc_ref[...]).astype(BF16)
    q_allT = jnp.dot(wuqT_ref[...], qnT, preferred_element_type=F32)
    vT_out[...] = (jnp.dot(wuvT_ref[...], kvnT, preferred_element_type=F32) + vonec_ref[...]).astype(BF16)
    kvl_r = latk[:, 0:kvl]
    kr = latk[:, kvl:kvl + LANES]
    kvn = (kvl_r * lax.rsqrt(jnp.mean(kvl_r * kvl_r, axis=-1, keepdims=True) + EPS) * kvlg_ref[...]).astype(BF16)
    k_all = jnp.dot(kvn, wuk_ref[...], preferred_element_type=F32)
    fz_out[1] = wide(2)
    hi_out[...] = wide(3)
    hg_out[...] = wide(4)

    cosT = cosT_ref[...]
    sinT = sinT_ref[...]
    qngc = qngc_ref[...]
    for h in range(MLA_HEADS):
        t = q_allT[h * LANES:(h + 1) * LANES]
        tn = t * lax.rsqrt(jnp.sum(t * t, axis=0, keepdims=True) * (1.0 / MLA_QK) + EPS) * qngc
        x1 = tn[MLA_NOPE:MLA_NOPE + half]
        x2 = tn[MLA_NOPE + half:MLA_QK]
        rot = jnp.concatenate([tn[0:MLA_NOPE], x1 * cosT - x2 * sinT, x2 * cosT + x1 * sinT, tn[MLA_QK:LANES]],
                              axis=0)
        qT_out[h * LANES:(h + 1) * LANES, :] = (rot * q_scale).astype(BF16)

    kng = kng_ref[...]
    krg = kr * kng
    up = pltpu.roll(krg, LANES - half, axis=1)
    dn = pltpu.roll(krg, half, axis=1)
    kr_rot = krg * cos_ref[...] + up * sa_ref[...] + dn * sb_ref[...]
    kr_ssq = jnp.sum(kr * kr, axis=-1, keepdims=True)
    for h in range(MLA_HEADS):
        sl = slice(h * LANES, (h + 1) * LANES)
        t = k_all[:, sl]
        r = lax.rsqrt((jnp.sum(t * t, axis=-1, keepdims=True) + kr_ssq) * (1.0 / MLA_QK) + EPS)
        k_out[:, sl] = (r * (t * kng + kr_rot)).astype(BF16)


def _inproj(x2, seq, norm_mix, w_in, q_lat_norm, kv_lat_norm, w_uq, w_ukv, q_norm, k_norm, tm=512):
    T, D = x2.shape
    tm = min(tm, seq)
    H = MLA_HEADS
    ql, kvl = w_uq.shape[0], w_ukv.shape[0]
    hw = HG_HEADS * HG_DK
    o_kr = ql + kvl
    o_h = o_kr + MLA_ROPE
    zeros = lambda c: jnp.zeros((D, c), F32)
    w_latT = w_in[:, :o_kr].T.astype(BF16)
    w_kr = jnp.concatenate([w_in[:, ql:o_kr], zeros(MLA_NOPE), w_in[:, o_kr:o_h],
                            zeros(LANES - MLA_QK)], axis=1).astype(BF16)
    w_h = w_in[:, o_h:].astype(BF16)
    pad = LANES - MLA_QK
    wuqT = jnp.pad(w_uq.reshape(ql, H, MLA_QK), ((0, 0), (0, 0), (0, pad))).reshape(ql, H * LANES).T.astype(BF16)
    wkv = w_ukv.reshape(kvl, H, MLA_NOPE + MLA_V)
    wuk = jnp.pad(wkv[:, :, :MLA_NOPE], ((0, 0), (0, 0), (0, LANES - MLA_NOPE))).reshape(kvl, H * LANES).astype(BF16)
    wuvT = jnp.pad(wkv[:, :, MLA_NOPE:], ((0, 0), (0, 0), (0, LANES - MLA_V))).reshape(kvl, H * LANES).T.astype(BF16)
    vonec = jnp.tile(jnp.concatenate([jnp.zeros((MLA_V,), F32), jnp.ones((LANES - MLA_V,), F32)]), H)
    vonec = vonec.reshape(H * LANES, 1)
    qngc = jnp.pad(q_norm, (0, pad)).reshape(LANES, 1)
    kng = jnp.pad(k_norm, (0, pad)).reshape(1, LANES)

    half = MLA_ROPE // 2
    inv = 1.0 / (ROPE_THETA ** (jnp.arange(half, dtype=F32) / half))
    ang = jnp.arange(seq, dtype=F32)[:, None] * inv[None, :]
    cos, sin = jnp.cos(ang), jnp.sin(ang)
    z = lambda c: jnp.zeros((seq, c), F32)
    cos_t = jnp.concatenate([jnp.ones((seq, MLA_NOPE), F32), cos, cos, z(pad)], axis=1)
    sa_t = jnp.concatenate([z(MLA_NOPE), -sin, z(half), z(pad)], axis=1)
    sb_t = jnp.concatenate([z(MLA_NOPE), z(half), sin, z(pad)], axis=1)

    nseq = seq // tm
    row = lambda w: pl.BlockSpec((tm, w), lambda i: (i, 0))
    col = lambda r: pl.BlockSpec((r, tm), lambda i: (0, i))
    tab = pl.BlockSpec((tm, LANES), lambda i: (i % nseq, 0))
    tabT = pl.BlockSpec((half, tm), lambda i: (0, i % nseq))
    q_scale = (MLA_QK ** -0.5) * math.log2(math.e)
    outs = pl.pallas_call(
        functools.partial(_inproj_kernel, q_scale=q_scale),
        grid=(T // tm,),
        in_specs=[row(D), _full((1, D)), _full(w_latT.shape), _full(w_kr.shape), _full(w_h.shape),
                  _full((ql, 1)), _full((kvl, 1)), _full((1, kvl)), _full(wuqT.shape), _full(wuk.shape),
                  _full(wuvT.shape), _full((LANES, 1)), _full((1, LANES)), _full((H * LANES, 1)),
                  tabT, tabT, tab, tab, tab],
        out_specs=[col(H * LANES), row(H * LANES), col(H * LANES), row(hw),
                   pl.BlockSpec((2, tm, hw), lambda i: (0, i, 0)), row(hw), row(hw)],
        out_shape=[jax.ShapeDtypeStruct((H * LANES, T), BF16), jax.ShapeDtypeStruct((T, H * LANES), BF16),
                   jax.ShapeDtypeStruct((H * LANES, T), BF16), jax.ShapeDtypeStruct((T, hw), BF16),
                   jax.ShapeDtypeStruct((2, T, hw), BF16), jax.ShapeDtypeStruct((T, hw), BF16),
                   jax.ShapeDtypeStruct((T, hw), BF16)],
        compiler_params=_params(("parallel",)),
        name="inproj",
    )(x2, norm_mix.reshape(1, D), w_latT, w_kr, w_h, q_lat_norm.reshape(ql, 1), kv_lat_norm.reshape(kvl, 1),
      kv_lat_norm.reshape(1, kvl), wuqT, wuk, wuvT, qngc, kng, vonec, cos.T, sin.T, cos_t, sa_t, sb_t)
    return outs


def _attn_kernel(qT_ref, k_ref, vT_ref, o_ref, s00, s01, s10, s11, *, tk):
    tq = qT_ref.shape[1]
    nk = k_ref.shape[0] // tk
    qTs = [qT_ref[j * LANES:(j + 1) * LANES, :] for j in range(2)]
    s_bufs = ((s00, s01), (s10, s11))

    def scores(j, slot, c):
        r0 = pl.multiple_of(c * tk, tk)
        sT = jnp.dot(k_ref[pl.ds(r0, tk), j * LANES:(j + 1) * LANES], qTs[j], preferred_element_type=F32)
        s_bufs[j][slot][...] = sT
        return jnp.max(sT, axis=0, keepdims=True)

    def absorb(j, slot, c, m, acc, mx):
        r0 = pl.multiple_of(c * tk, tk)
        m_new = jnp.maximum(m, mx)
        pT = jnp.exp2(s_bufs[j][slot][...] - m_new).astype(BF16)
        pv = jnp.dot(vT_ref[j * LANES:(j + 1) * LANES, pl.ds(r0, tk)], pT, preferred_element_type=F32)
        return m_new, jnp.exp2(m - m_new) * acc + pv

    def step(c, slot, state, prefetch):
        mx_next = [scores(j, 1 - slot, c + 1) if prefetch else state[j][2] for j in range(2)]
        new = []
        for j in range(2):
            m, acc, mx = state[j]
            m, acc = absorb(j, slot, c, m, acc, mx)
            new.append((m, acc, mx_next[j]))
        return tuple(new)

    def pair(i, state):
        c = 2 * i
        return step(c + 1, 1, step(c, 0, state, True), True)

    state = tuple((jnp.full((1, tq), -jnp.inf, F32), jnp.zeros((LANES, tq), F32), scores(j, 0, 0))
                  for j in range(2))
    state = lax.fori_loop(0, nk // 2 - 1, pair, state)
    state = step(nk - 1, 1, step(nk - 2, 0, state, True), False)
    acc0, acc1 = state[0][1], state[1][1]
    oT = jnp.concatenate([acc0[0:MLA_V] / acc0[MLA_V:MLA_V + 1], acc1[0:MLA_V] / acc1[MLA_V:MLA_V + 1]], axis=0)
    o_ref[...] = oT.T.astype(o_ref.dtype)


def _attention(qT, k3, vT, tq=512, tk=512):
    B, S, _ = k3.shape
    tq, tk = min(tq, S), min(tk, S)
    nq = S // tq
    hp = MLA_HEADS // 2
    return pl.pallas_call(
        functools.partial(_attn_kernel, tk=tk),
        grid=(B, hp, nq),
        in_specs=[pl.BlockSpec((2 * LANES, tq), lambda b, h, i: (h, b * nq + i)),
                  pl.BlockSpec((None, S, 2 * LANES), lambda b, h, i: (b, 0, h)),
                  pl.BlockSpec((2 * LANES, S), lambda b, h, i: (h, b))],
        out_specs=pl.BlockSpec((None, tq, 2 * MLA_V), lambda b, h, i: (b, i, h)),
        out_shape=jax.ShapeDtypeStruct((B, S, MLA_HEADS * MLA_V), BF16),
        scratch_shapes=[pltpu.VMEM((tk, tq), F32)] * 4,
        compiler_params=_params(("parallel", "parallel", "arbitrary")),
        name="attention",
    )(qT, k3, vT)


def _hgrn_kernel(hq_ref, z_ref, hi_ref, lb_ref, tri_ref, o_ref, st_ref, kk_scr, b_scr, edge_scr, qh_scr, a_scr,
                 u_scr, qf_scr, of_scr):
    d = pl.program_id(1)
    C = HG_CHUNK
    tl = hq_ref.shape[0]
    nc = tl // C
    heads = [slice(h * HG_DK, (h + 1) * HG_DK) for h in range(HG_HEADS)]

    @pl.when(pl.program_id(2) == 0)
    def _():
        st_ref[...] = jnp.zeros_like(st_ref)

    one_m_lb = 1.0 - lb_ref[...]
    tri = tri_ref[...]
    keep = tri > 0
    fwd = d == 0

    worst = jnp.zeros_like(one_m_lb)
    for c in range(nc):
        rows = slice(c * C, (c + 1) * C)
        kk = one_m_lb * jax.nn.sigmoid(-z_ref[rows, :].astype(F32))
        g = jnp.log(1.0 - kk)
        g_hi = g.astype(BF16)
        g_lo = (g - g_hi.astype(F32)).astype(BF16)
        b = jnp.dot(tri, g_hi, preferred_element_type=F32) + jnp.dot(tri, g_lo, preferred_element_type=F32)
        b_edge = jnp.where(fwd, b[C - 1:C, :], b[0:1, :])
        kk_scr[rows, :] = kk
        b_scr[rows, :] = b
        edge_scr[c] = b_edge
        worst = jnp.maximum(worst, -b_edge)
    safe = jnp.max(worst) < HG_SAFE_EXPONENT

    @pl.when(safe)
    def _():
        for c in range(nc):
            rows = slice(c * C, (c + 1) * C)
            kk = kk_scr[rows, :]
            b = b_scr[rows, :]
            hq = hq_ref[rows, :].astype(F32)
            qh = (hq * jax.nn.sigmoid(hq) * jnp.exp(b)).astype(BF16)
            kt = (kk * jnp.exp(-b)).astype(BF16)
            ks = (kk * jnp.exp(edge_scr[c] - b)).astype(BF16)
            v = hi_ref[rows, :]
            qh_scr[rows, :] = qh
            for h, sl in enumerate(heads):
                a = lax.dot_general(qh[:, sl], kt[:, sl], NT_DIMS, preferred_element_type=F32)
                a_scr[c, h] = jnp.where(keep, a, 0.0).astype(BF16)
                u_scr[c, h] = lax.dot_general(v[:, sl], ks[:, sl], TN_DIMS, preferred_element_type=F32)
        st = [st_ref[h] for h in range(HG_HEADS)]
        for p in range(nc):
            c = jnp.where(fwd, p, nc - 1 - p)
            r0 = pl.multiple_of(c * C, C)
            dec = jnp.exp(edge_scr[c])
            qh = qh_scr[pl.ds(r0, C), :]
            v = hi_ref[pl.ds(r0, C), :]
            for h, sl in enumerate(heads):
                o = jnp.dot(a_scr[c, h], v[:, sl], preferred_element_type=F32)
                o = o + lax.dot_general(qh[:, sl], st[h].astype(BF16), NT_DIMS, preferred_element_type=F32)
                o_ref[pl.ds(r0, C), sl] = o.astype(o_ref.dtype)
                st[h] = st[h] * dec[:, sl] + u_scr[c, h]
        for h in range(HG_HEADS):
            st_ref[h] = st[h]

    @pl.when(jnp.logical_not(safe))
    def _():
        hq = hq_ref[...].astype(F32)
        qf_scr[...] = hq * jax.nn.sigmoid(hq)
        first = lax.broadcasted_iota(I32, (16, HG_DK), 0) == 0

        def row(i, carry):
            t = jnp.where(fwd, i, tl - 1 - i)
            kk = kk_scr[pl.ds(t, 1), :]
            f = 1.0 - kk
            q = qf_scr[pl.ds(t, 1), :]
            g0 = pl.multiple_of((t // 16) * 16, 16)
            v = hi_ref[pl.ds(g0, 16), :].astype(F32)
            v = jnp.sum(jnp.where(lax.broadcasted_iota(I32, v.shape, 0) == t % 16, v, 0.0), axis=0, keepdims=True)
            outs = []
            for h, sl in enumerate(heads):
                pad = lambda x: jnp.where(first, jnp.broadcast_to(x[:, sl], (16, HG_DK)), 0.0).astype(BF16)
                st = st_ref[h] * f[:, sl] + lax.dot_general(pad(v), pad(kk), TN_DIMS, preferred_element_type=F32)
                st_ref[h] = st
                outs.append(lax.dot_general(pad(q), st.astype(BF16), NT_DIMS, preferred_element_type=F32)[0:1])
            of_scr[pl.ds(t, 1), :] = jnp.concatenate(outs, axis=1)
            return carry

        lax.fori_loop(0, tl, row, 0)
        o_ref[...] = of_scr[...].astype(o_ref.dtype)


def _hgrn(hq3, fz4, hi3, lb, tl=512):
    B, S, W = hq3.shape
    tl = min(tl, S)
    nt = S // tl
    C = HG_CHUNK
    r = lax.broadcasted_iota(I32, (C, C), 0)
    c = lax.broadcasted_iota(I32, (C, C), 1)
    tri = jnp.stack([r >= c, r <= c]).astype(BF16)
    tile = lambda b, d, i: (b, i + d * (nt - 1 - 2 * i), 0)
    return pl.pallas_call(
        _hgrn_kernel,
        grid=(B, 2, nt),
        in_specs=[pl.BlockSpec((None, tl, W), tile),
                  pl.BlockSpec((None, None, tl, W), lambda b, d, i: (d, b, i + d * (nt - 1 - 2 * i), 0)),
                  pl.BlockSpec((None, tl, W), tile),
                  pl.BlockSpec((None, 1, W), lambda b, d, i: (d, 0, 0)),
                  pl.BlockSpec((None, C, C), lambda b, d, i: (d, 0, 0))],
        out_specs=pl.BlockSpec((None, None, tl, W), lambda b, d, i: (d, b, i + d * (nt - 1 - 2 * i), 0)),
        out_shape=jax.ShapeDtypeStruct((2, B, S, W), BF16),
        scratch_shapes=[pltpu.VMEM((HG_HEADS, HG_DV, HG_DK), F32),
                        pltpu.VMEM((tl, W), F32), pltpu.VMEM((tl, W), F32),
                        pltpu.VMEM((tl // C, 1, W), F32),
                        pltpu.VMEM((tl, W), BF16),
                        pltpu.VMEM((tl // C, HG_HEADS, C, C), BF16),
                        pltpu.VMEM((tl // C, HG_HEADS, HG_DV, HG_DK), F32),
                        pltpu.VMEM((tl, W), F32), pltpu.VMEM((tl, W), F32)],
        compiler_params=_params(("parallel", "parallel", "arbitrary")),
        name="hgrn",
    )(hq3, fz4, hi3, lb, tri)


def _outproj_kernel(x_ref, a_ref, o_ref, hg_ref, ong_ref, wa_ref, wr_ref, g2_ref, wrt_ref, brt_ref,
                    h1_out, n2_out, route_out):
    o = o_ref[0].astype(F32) + o_ref[1].astype(F32)
    hg = hg_ref[...].astype(F32)
    gate = hg * jax.nn.sigmoid(hg)
    ong = ong_ref[...]
    parts = []
    for h in range(HG_HEADS):
        sl = slice(h * HG_DV, (h + 1) * HG_DV)
        oh = o[:, sl]
        parts.append((oh * lax.rsqrt(jnp.mean(oh * oh, axis=-1, keepdims=True) + EPS) * ong * gate[:, sl]).astype(BF16))
    r = jnp.concatenate(parts, axis=1)
    h1 = x_ref[...] + jnp.dot(a_ref[...], wa_ref[...], preferred_element_type=F32)
    h1 = h1 + jnp.dot(r, wr_ref[...], preferred_element_type=F32)
    h1_out[...] = h1
    n2 = h1 * lax.rsqrt(jnp.mean(h1 * h1, axis=-1, keepdims=True) + EPS) * g2_ref[...]
    n2_out[...] = n2

    n2_hi = n2.astype(BF16)
    n2_lo = (n2 - n2_hi.astype(F32)).astype(BF16)
    l_hi = jnp.dot(n2_hi, wrt_ref[...], preferred_element_type=F32)
    l_lo = jnp.dot(n2_lo, wrt_ref[:, 0:LANES], preferred_element_type=F32)
    logits = l_hi[:, 0:LANES] + l_hi[:, LANES:2 * LANES] + l_lo + brt_ref[...]
    tm = logits.shape[0]
    lane = lax.broadcasted_iota(I32, (tm, LANES), 1)
    ninf = -jnp.inf
    is_g = lane < N_GROUPS
    gl = jnp.where(is_g, logits, ninf)
    gmax = jnp.max(gl, axis=-1, keepdims=True)
    gidx = jnp.min(jnp.where(gl == gmax, lane, LANES), axis=-1, keepdims=True)
    g_w = 1.0 / jnp.sum(jnp.where(is_g, jnp.exp(logits - gmax), 0.0), axis=-1, keepdims=True)
    lo = N_GROUPS + EXPERTS_PER_GROUP * gidx
    el = jnp.where((lane >= lo) & (lane < lo + EXPERTS_PER_GROUP), logits, ninf)
    m1 = jnp.max(el, axis=-1, keepdims=True)
    i1 = jnp.min(jnp.where(el == m1, lane, LANES), axis=-1, keepdims=True)
    el2 = jnp.where(lane == i1, ninf, el)
    m2 = jnp.max(el2, axis=-1, keepdims=True)
    i2 = jnp.min(jnp.where(el2 == m2, lane, LANES), axis=-1, keepdims=True)
    t = jnp.exp(m2 - m1)
    w1 = 1.0 / (1.0 + t)
    w2 = t / (1.0 + t)
    e1 = (i1 - N_GROUPS).astype(F32)
    e2 = (i2 - N_GROUPS).astype(F32)
    route = jnp.where(lane == 0, e1, jnp.where(lane == 1, e2, jnp.where(lane == 2, g_w * w1,
                      jnp.where(lane == 3, g_w * w2, 0.0))))
    route_out[...] = route


def _outproj(x2, a2, o3, hg2, hg_out_norm, w_out, norm_ffn, w_group, b_group, w_router, b_router, tm=512):
    T, D = x2.shape
    tm = min(tm, T)
    wa = w_out[:MLA_HEADS * MLA_V].astype(BF16)
    wr = w_out[MLA_HEADS * MLA_V:].astype(BF16)
    npad = LANES - N_GROUPS - N_EXPERTS
    wrt = jnp.concatenate([w_group, w_router, jnp.zeros((D, npad), F32)], axis=1)
    wrt_hi = wrt.astype(BF16)
    wrt = jnp.concatenate([wrt_hi, (wrt - wrt_hi.astype(F32)).astype(BF16)], axis=1)
    brt = jnp.concatenate([b_group, b_router, jnp.zeros((npad,), F32)]).reshape(1, LANES)
    row = lambda w: pl.BlockSpec((tm, w), lambda i: (i, 0))
    hw = HG_HEADS * HG_DV
    return pl.pallas_call(
        _outproj_kernel,
        grid=(T // tm,),
        in_specs=[row(D), row(a2.shape[1]), pl.BlockSpec((2, tm, hw), lambda i: (0, i, 0)), row(hw),
                  _full((1, HG_DV)), _full(wa.shape), _full(wr.shape), _full((1, D)), _full(wrt.shape),
                  _full((1, LANES))],
        out_specs=[row(D), row(D), row(LANES)],
        out_shape=[jax.ShapeDtypeStruct((T, D), F32), jax.ShapeDtypeStruct((T, D), F32),
                   jax.ShapeDtypeStruct((T, LANES), F32)],
        compiler_params=_params(("parallel",)),
        name="outproj",
    )(x2, a2, o3, hg2, hg_out_norm.reshape(1, HG_DV), wa, wr, norm_ffn.reshape(1, D), wrt, brt)


def _plan_kernel(route_ref, tri_ref, upper_ref, lpos_out, lposT_out, runs_out, glob_out, tot_ref, base_ref):
    p = pl.program_id(0)
    i = pl.program_id(1)
    tm = route_ref.shape[0]
    lane = lax.broadcasted_iota(I32, (tm, LANES), 1)
    route = route_ref[...]
    is1 = lane == route[:, 0:1].astype(I32)
    is2 = lane == route[:, 1:2].astype(I32)
    onehot = jnp.where(is1 | is2, 1.0, 0.0)
    units = jnp.ceil(jnp.sum(onehot, axis=0, keepdims=True) * (1.0 / RUN_ROWS))
    sub = lax.broadcasted_iota(I32, (8, LANES), 0)
    rows3 = lambda a, b, c: jnp.where(sub == 0, a, jnp.where(sub == 1, b, jnp.where(sub == 2, c, 0.0)))

    @pl.when((p == 0) & (i == 0))
    def _():
        tot_ref[...] = jnp.zeros_like(tot_ref)

    @pl.when(p == 0)
    def _():
        tot_ref[...] += units

    @pl.when((p == 1) & (i == 0))
    def _():
        tot = tot_ref[...]
        block_units = MOE_BLOCK // RUN_ROWS
        padded = jnp.ceil(tot * (1.0 / block_units)) * block_units
        start = jnp.dot(jnp.broadcast_to(padded, (8, LANES)), upper_ref[...], preferred_element_type=F32,
                        precision=lax.Precision.HIGHEST)[0:1]
        base_ref[...] = start
        glob_out[...] = rows3(start + padded, start + tot, padded - tot).astype(I32)

    @pl.when(p == 1)
    def _():
        before = jnp.dot(tri_ref[...], onehot.astype(BF16), preferred_element_type=F32)
        local = jnp.dot(jnp.broadcast_to(units, (8, LANES)).astype(BF16), upper_ref[...].astype(BF16),
                        preferred_element_type=F32)[0:1] * RUN_ROWS
        pos = local + before
        p1 = jnp.sum(jnp.where(is1, pos, 0.0), axis=-1, keepdims=True)
        p2 = jnp.sum(jnp.where(is2, pos, 0.0), axis=-1, keepdims=True)
        slab = jnp.where(lane == 0, p1, jnp.where(lane == 1, p2, 0.0))
        lpos_out[...] = slab.astype(I32)
        lposT_out[...] = slab.T[0:8].astype(I32)
        runs_out[...] = rows3(local, units, base_ref[...] * RUN_ROWS).astype(I32)
        base_ref[...] += units


def _plan(route, tm):
    T = route.shape[0]
    nt = T // tm
    r = lax.broadcasted_iota(I32, (tm, tm), 0)
    c = lax.broadcasted_iota(I32, (tm, tm), 1)
    tri = (r > c).astype(BF16)
    ru = lax.broadcasted_iota(I32, (LANES, LANES), 0)
    cu = lax.broadcasted_iota(I32, (LANES, LANES), 1)
    upper = (ru < cu).astype(F32)
    return pl.pallas_call(
        _plan_kernel,
        grid=(2, nt),
        in_specs=[pl.BlockSpec((tm, LANES), lambda p, i: (i, 0)), _full((tm, tm)), _full((LANES, LANES))],
        out_specs=[pl.BlockSpec((tm, LANES), lambda p, i: (i * p, 0)),
                   pl.BlockSpec((8, tm), lambda p, i: (0, i * p)),
                   pl.BlockSpec((None, 8, LANES), lambda p, i: (i * p, 0, 0)),
                   _full((8, LANES))],
        out_shape=[jax.ShapeDtypeStruct((T, LANES), I32), jax.ShapeDtypeStruct((8, T), I32),
                   jax.ShapeDtypeStruct((nt, 8, LANES), I32), jax.ShapeDtypeStruct((8, LANES), I32)],
        scratch_shapes=[pltpu.VMEM((1, LANES), F32), pltpu.VMEM((1, LANES), F32)],
        compiler_params=_params(("arbitrary", "arbitrary")),
        name="plan",
    )(route, tri, upper)


def _for_each_run_unit(tile, start_a_ref, units_ref, start_b_ref, fn):
    def per_expert(e, total):
        j = tile * N_EXPERTS + e
        a0, b0, n = start_a_ref[j], start_b_ref[j], units_ref[j]

        def per_unit(u, carry):
            fn(pl.multiple_of(a0 + u * RUN_ROWS, RUN_ROWS), pl.multiple_of(b0 + u * RUN_ROWS, RUN_ROWS))
            return carry

        lax.fori_loop(0, n, per_unit, 0)
        return total + n

    return lax.fori_loop(0, N_EXPERTS, per_expert, 0)


def _dispatch_kernel(ls_ref, un_ref, gd_ref, ts_ref, tu_ref, lposT_ref, n2_ref, xs_out, xl, sem):
    i = pl.program_id(0)
    rows = xl.shape[0]
    tm = n2_ref.shape[0]
    lp = lposT_ref[...]
    r = lax.broadcasted_iota(I32, (rows, tm), 0)
    pick = jnp.where((r == lp[0:1, :]) | (r == lp[1:2, :]), 1.0, 0.0).astype(BF16)
    xl[...] = jnp.dot(pick, n2_ref[...].astype(BF16), preferred_element_type=F32)

    def unit_copy(src, dst):
        return pltpu.make_async_copy(xl.at[pl.ds(src, RUN_ROWS)], xs_out.at[pl.ds(dst, RUN_ROWS)], sem)

    def start_unit(src, dst):
        unit_copy(src, dst).start()

    def wait_unit(u, carry):
        unit_copy(0, 0).wait()
        return carry

    total = _for_each_run_unit(i, ls_ref, un_ref, gd_ref, start_unit)
    lax.fori_loop(0, total, wait_unit, 0)

    @pl.when(i == pl.num_programs(0) - 1)
    def _():
        xl[...] = jnp.zeros_like(xl)

        def zero_copy(dst, n):
            return pltpu.make_async_copy(xl.at[pl.ds(0, n)], xs_out.at[pl.ds(dst, n)], sem)

        def per_expert(e, total):
            d0, n = ts_ref[e], tu_ref[e]

            def per_unit(u, carry):
                zero_copy(pl.multiple_of(d0 + u * RUN_ROWS, RUN_ROWS), RUN_ROWS).start()
                return carry

            lax.fori_loop(0, n, per_unit, 0)
            return total + n

        def wait_unit_zero(u, carry):
            zero_copy(0, RUN_ROWS).wait()
            return carry

        lax.fori_loop(0, lax.fori_loop(0, N_EXPERTS, per_expert, 0), wait_unit_zero, 0)

        last = N_EXPERTS - 1
        first_unused = (ts_ref[last] + tu_ref[last] * RUN_ROWS) // MOE_BLOCK
        n_blocks = xs_out.shape[0] // MOE_BLOCK

        def start_block(b, carry):
            zero_copy(pl.multiple_of(b * MOE_BLOCK, MOE_BLOCK), MOE_BLOCK).start()
            return carry

        def wait_block(b, carry):
            zero_copy(0, MOE_BLOCK).wait()
            return carry

        lax.fori_loop(first_unused, n_blocks, start_block, 0)
        lax.fori_loop(first_unused, n_blocks, wait_block, 0)


def _dispatch(n2, lposT, tables, n_rows, tm):
    T, D = n2.shape
    local_rows = 2 * tm + N_EXPERTS * RUN_ROWS
    grid_spec = pltpu.PrefetchScalarGridSpec(
        num_scalar_prefetch=5,
        grid=(T // tm,),
        in_specs=[pl.BlockSpec((8, tm), lambda i, *_: (0, i)),
                  pl.BlockSpec((tm, D), lambda i, *_: (i, 0))],
        out_specs=pl.BlockSpec(memory_space=pl.ANY),
        scratch_shapes=[pltpu.VMEM((local_rows, D), F32), pltpu.SemaphoreType.DMA(())],
    )
    return pl.pallas_call(
        _dispatch_kernel,
        grid_spec=grid_spec,
        out_shape=jax.ShapeDtypeStruct((n_rows, D), F32),
        compiler_params=_params(("arbitrary",), has_side_effects=True),
        name="dispatch",
    )(*tables, lposT, n2)


def _expert_kernel(be_ref, nused_ref, x_ref, wgu_ref, wd_ref, y_ref):
    i = pl.program_id(0)
    de = wd_ref.shape[0]
    half = x_ref.shape[0] // 2

    @pl.when(i < nused_ref[0])
    def _():
        gu = [jnp.dot(x_ref[r * half:(r + 1) * half, :].astype(BF16), wgu_ref[...], preferred_element_type=F32)
              for r in range(2)]
        for r in range(2):
            g, u = gu[r][:, 0:de], gu[r][:, de:2 * de]
            hmid = (g * jax.nn.sigmoid(g) * u).astype(BF16)
            y_ref[r * half:(r + 1) * half, :] = jnp.dot(hmid, wd_ref[...], preferred_element_type=F32)

    @pl.when(i >= nused_ref[0])
    def _():
        y_ref[...] = jnp.zeros_like(y_ref)


def _experts(xs, block_e, nused, w_gate, w_up, w_down):
    P, D = xs.shape
    nb = P // MOE_BLOCK
    de = w_gate.shape[2]
    grid_spec = pltpu.PrefetchScalarGridSpec(
        num_scalar_prefetch=2,
        grid=(nb,),
        in_specs=[pl.BlockSpec((MOE_BLOCK, D), lambda i, be, nu: (jnp.minimum(i, nu[0] - 1), 0)),
                  pl.BlockSpec((None, D, 2 * de), lambda i, be, nu: (be[i], 0, 0)),
                  pl.BlockSpec((None, de, D), lambda i, be, nu: (be[i], 0, 0))],
        out_specs=pl.BlockSpec((MOE_BLOCK, D), lambda i, be, nu: (i, 0)),
    )
    return pl.pallas_call(
        _expert_kernel,
        grid_spec=grid_spec,
        out_shape=jax.ShapeDtypeStruct((P, D), F32),
        compiler_params=_params(("arbitrary",)),
        name="experts",
    )(block_e, nused, xs, jnp.concatenate([w_gate, w_up], axis=2).astype(BF16), w_down.astype(BF16))


def _combine_kernel(ls_ref, un_ref, gd_ref, lpos_ref, route_ref, h1_ref, ys_ref, out_ref, yl, sems):
    i = pl.program_id(0)
    nt = pl.num_programs(0)
    tm = h1_ref.shape[0]
    rows = yl.shape[1]
    slot = i % 2

    def unit_copy(s, src, dst):
        return pltpu.make_async_copy(ys_ref.at[pl.ds(src, RUN_ROWS)], yl.at[s, pl.ds(dst, RUN_ROWS)], sems.at[s])

    def gather(tile, s):
        def start_unit(src, dst):
            unit_copy(s, src, dst).start()
        _for_each_run_unit(tile, gd_ref, un_ref, ls_ref, start_unit)

    @pl.when(i == 0)
    def _():
        yl[...] = jnp.zeros_like(yl)
        gather(0, 0)

    @pl.when(i + 1 < nt)
    def _():
        gather(i + 1, 1 - slot)

    def count(e, total):
        return total + un_ref[i * N_EXPERTS + e]

    def wait_unit(u, carry):
        unit_copy(slot, 0, 0).wait()
        return carry

    lax.fori_loop(0, lax.fori_loop(0, N_EXPERTS, count, 0), wait_unit, 0)

    lp = lpos_ref[...]
    route = route_ref[...]
    r = lax.broadcasted_iota(I32, (tm, rows), 1)
    w = jnp.where(r == lp[:, 0:1], route[:, 2:3], 0.0) + jnp.where(r == lp[:, 1:2], route[:, 3:4], 0.0)
    out_ref[...] = h1_ref[...] + jnp.dot(w.astype(BF16), yl[slot].astype(BF16), preferred_element_type=F32)


def _combine(h1, route, lpos, tables, ys, tm):
    T, D = h1.shape
    local_rows = 2 * tm + N_EXPERTS * RUN_ROWS
    grid_spec = pltpu.PrefetchScalarGridSpec(
        num_scalar_prefetch=3,
        grid=(T // tm,),
        in_specs=[pl.BlockSpec((tm, LANES), lambda i, *_: (i, 0)),
                  pl.BlockSpec((tm, LANES), lambda i, *_: (i, 0)),
                  pl.BlockSpec((tm, D), lambda i, *_: (i, 0)),
                  pl.BlockSpec(memory_space=pl.ANY)],
        out_specs=pl.BlockSpec((tm, D), lambda i, *_: (i, 0)),
        scratch_shapes=[pltpu.VMEM((2, local_rows, D), F32), pltpu.SemaphoreType.DMA((2,))],
    )
    return pl.pallas_call(
        _combine_kernel,
        grid_spec=grid_spec,
        out_shape=jax.ShapeDtypeStruct((T, D), F32),
        compiler_params=_params(("arbitrary",)),
        name="combine",
    )(*tables, lpos, route, h1, ys)


def _mixers(x, norm_mix, w_in, q_lat_norm, kv_lat_norm, w_uq, w_ukv, q_norm, k_norm, lb_logits, layer):
    B, S, D = x.shape
    T = B * S
    q, k, v, hq, fz, hi, hg = _inproj(x.reshape(T, D), S, norm_mix, w_in, q_lat_norm, kv_lat_norm,
                                      w_uq, w_ukv, q_norm, k_norm)
    a = _attention(q, k.reshape(B, S, -1), v)
    lb = jnp.cumsum(jax.nn.softmax(lb_logits.astype(F32), axis=0), axis=0)[layer]
    hw = hq.shape[1]
    o = _hgrn(hq.reshape(B, S, hw), fz.reshape(2, B, S, hw), hi.reshape(B, S, hw), lb.reshape(2, 1, hw))
    return a.reshape(T, -1), o.reshape(2, T, hw), hg


def _moe(h1, n2, route, w_gate, w_up, w_down, tm=512):
    T, D = h1.shape
    tm = min(tm, T)
    nt = T // tm
    lpos, lposT, runs, glob = _plan(route, tm)
    per_run = lambda row: runs[:, row, :N_EXPERTS].reshape(-1)
    tables = (per_run(0), per_run(1), per_run(2))
    tails = (glob[1, :N_EXPERTS] * RUN_ROWS, glob[2, :N_EXPERTS])
    n_rows = -(-(2 * T + N_EXPERTS * RUN_ROWS * nt) // MOE_BLOCK) * MOE_BLOCK + N_EXPERTS * MOE_BLOCK
    nb = n_rows // MOE_BLOCK
    pend = glob[0, :N_EXPERTS] * RUN_ROWS
    block_row0 = jnp.arange(nb, dtype=I32) * MOE_BLOCK
    block_e = jnp.minimum(jnp.sum((pend[None, :] <= block_row0[:, None]).astype(I32), axis=1), N_EXPERTS - 1)
    nused = pend[N_EXPERTS - 1:] // MOE_BLOCK
    xs = _dispatch(n2, lposT, tables + tails, n_rows, tm)
    ys = _experts(xs, block_e, nused, w_gate, w_up, w_down)
    return _combine(h1, route, lpos, tables, ys, tm)


def kernel(x, norm_mix, w_in, q_lat_norm, kv_lat_norm, w_uq, w_ukv, q_norm, k_norm, lb_logits, hg_out_norm,
           w_out, norm_ffn, w_group, b_group, w_router, b_router, w_gate, w_up, w_down):
    B, S, D = x.shape
    h = x
    for l in range(norm_mix.shape[0]):
        h2 = h.reshape(B * S, D)
        a, o, hg = _mixers(h, norm_mix[l], w_in[l], q_lat_norm[l], kv_lat_norm[l], w_uq[l], w_ukv[l],
                           q_norm[l], k_norm[l], lb_logits, l)
        h1, n2, route = _outproj(h2, a, o, hg, hg_out_norm[l], w_out[l], norm_ffn[l], w_group[l], b_group[l],
                                 w_router[l], b_router[l])
        h = _moe(h1, n2, route, w_gate[l], w_up[l], w_down[l]).reshape(B, S, D)
    return h
```

```python
import functools
import math

import jax
import jax.numpy as jnp
from jax import lax
from jax.experimental import pallas as pl
from jax.experimental.pallas import tpu as pltpu

F32 = jnp.float32
BF16 = jnp.bfloat16
I32 = jnp.int32

EPS = 1e-6
LANES = 128
VMEM_LIMIT = 48 * 1024 * 1024

MLA_HEADS = 8
MLA_NOPE = 64
MLA_ROPE = 32
MLA_QK = MLA_NOPE + MLA_ROPE
MLA_V = 64
ROPE_THETA = 10000.0
ATTN_MAX_FIXED_SHIFT = 40.0
HG_HEADS = 4
HG_DK = 128
HG_DV = 128
HG_CHUNK = 64
HG_SAFE_EXPONENT = 60.0
N_GROUPS = 4
EXPERTS_PER_GROUP = 8
N_EXPERTS = N_GROUPS * EXPERTS_PER_GROUP
MOE_BLOCK = 512
RUN_ROWS = 8

NT_DIMS = (((1,), (1,)), ((), ()))
TN_DIMS = (((0,), (0,)), ((), ()))


def _params(sem, **kw):
    return pltpu.CompilerParams(dimension_semantics=sem, vmem_limit_bytes=VMEM_LIMIT, **kw)


def _full(shape):
    n = len(shape)
    return pl.BlockSpec(shape, lambda *_: (0,) * n)


def _inproj_kernel(x_ref, g_ref, wlatT_ref, wkr_ref, wh_ref, qlgc_ref, kvlgc_ref, kvlg_ref, wuqT_ref, wuk_ref,
                   wuvT_ref, qngc_ref, kng_ref, vonec_ref, cosT_ref, sinT_ref, cos_ref, sa_ref, sb_ref,
                   qT_out, k_out, vT_out, hq_out, fz_out, hi_out, hg_out, *, q_scale):
    x = x_ref[...]
    ms = jnp.mean(x * x, axis=-1, keepdims=True)
    n = (x * lax.rsqrt(ms + EPS) * g_ref[...]).astype(BF16)

    hw = hq_out.shape[1]
    ql = qlgc_ref.shape[0]
    kvl = kvlgc_ref.shape[0]
    half = MLA_ROPE // 2

    def wide(j):
        return jnp.dot(n, wh_ref[:, j * hw:(j + 1) * hw], preferred_element_type=F32).astype(BF16)

    latT = lax.dot_general(wlatT_ref[...], n, NT_DIMS, preferred_element_type=F32)
    latk = jnp.dot(n, wkr_ref[...], preferred_element_type=F32)
    hq_out[...] = wide(0)
    fz_out[0] = wide(1)
    qlT = latT[0:ql]
    kvlT = latT[ql:ql + kvl]
    qnT = (qlT * lax.rsqrt(jnp.mean(qlT * qlT, axis=0, keepdims=True) + EPS) * qlgc_ref[...]).astype(BF16)
    kvnT = (kvlT * lax.rsqrt(jnp.mean(kvlT * kvlT, axis=0, keepdims=True) + EPS) * kvlgc_ref[...]).astype(BF16)
    q_allT = jnp.dot(wuqT_ref[...], qnT, preferred_element_type=F32)
    vT_out[...] = (jnp.dot(wuvT_ref[...], kvnT, preferred_element_type=F32) + vonec_ref[...]).astype(BF16)
    kvl_r = latk[:, 0:kvl]
    kr = latk[:, kvl:kvl + LANES]
    kvn = (kvl_r * lax.rsqrt(jnp.mean(kvl_r * kvl_r, axis=-1, keepdims=True) + EPS) * kvlg_ref[...]).astype(BF16)
    k_all = jnp.dot(kvn, wuk_ref[...], preferred_element_type=F32)
    fz_out[1] = wide(2)
    hi_out[...] = wide(3)
    hg_out[...] = wide(4)

    cosT = cosT_ref[...]
    sinT = sinT_ref[...]
    qngc = qngc_ref[...]
    for h in range(MLA_HEADS):
        t = q_allT[h * LANES:(h + 1) * LANES]
        tn = t * lax.rsqrt(jnp.sum(t * t, axis=0, keepdims=True) * (1.0 / MLA_QK) + EPS) * qngc
        x1 = tn[MLA_NOPE:MLA_NOPE + half]
        x2 = tn[MLA_NOPE + half:MLA_QK]
        rot = jnp.concatenate([tn[0:MLA_NOPE], x1 * cosT - x2 * sinT, x2 * cosT + x1 * sinT, tn[MLA_QK:LANES]],
                              axis=0)
        qT_out[h * LANES:(h + 1) * LANES, :] = (rot * q_scale).astype(BF16)

    kng = kng_ref[...]
    krg = kr * kng
    up = pltpu.roll(krg, LANES - half, axis=1)
    dn = pltpu.roll(krg, half, axis=1)
    kr_rot = krg * cos_ref[...] + up * sa_ref[...] + dn * sb_ref[...]
    kr_ssq = jnp.sum(kr * kr, axis=-1, keepdims=True)
    for h in range(MLA_HEADS):
        sl = slice(h * LANES, (h + 1) * LANES)
        t = k_all[:, sl]
        r = lax.rsqrt((jnp.sum(t * t, axis=-1, keepdims=True) + kr_ssq) * (1.0 / MLA_QK) + EPS)
        k_out[:, sl] = (r * (t * kng + kr_rot)).astype(BF16)


def _inproj(x2, seq, norm_mix, w_in, q_lat_norm, kv_lat_norm, w_uq, w_ukv, q_norm, k_norm, tm=512):
    T, D = x2.shape
    tm = min(tm, seq)
    H = MLA_HEADS
    ql, kvl = w_uq.shape[0], w_ukv.shape[0]
    hw = HG_HEADS * HG_DK
    o_kr = ql + kvl
    o_h = o_kr + MLA_ROPE
    zeros = lambda c: jnp.zeros((D, c), F32)
    w_latT = w_in[:, :o_kr].T.astype(BF16)
    w_kr = jnp.concatenate([w_in[:, ql:o_kr], zeros(MLA_NOPE), w_in[:, o_kr:o_h],
                            zeros(LANES - MLA_QK)], axis=1).astype(BF16)
    w_h = w_in[:, o_h:].astype(BF16)
    pad = LANES - MLA_QK
    wuqT = jnp.pad(w_uq.reshape(ql, H, MLA_QK), ((0, 0), (0, 0), (0, pad))).reshape(ql, H * LANES).T.astype(BF16)
    wkv = w_ukv.reshape(kvl, H, MLA_NOPE + MLA_V)
    wuk = jnp.pad(wkv[:, :, :MLA_NOPE], ((0, 0), (0, 0), (0, LANES - MLA_NOPE))).reshape(kvl, H * LANES).astype(BF16)
    wuvT = jnp.pad(wkv[:, :, MLA_NOPE:], ((0, 0), (0, 0), (0, LANES - MLA_V))).reshape(kvl, H * LANES).T.astype(BF16)
    vonec = jnp.tile(jnp.concatenate([jnp.zeros((MLA_V,), F32), jnp.ones((LANES - MLA_V,), F32)]), H)
    vonec = vonec.reshape(H * LANES, 1)
    qngc = jnp.pad(q_norm, (0, pad)).reshape(LANES, 1)
    kng = jnp.pad(k_norm, (0, pad)).reshape(1, LANES)

    half = MLA_ROPE // 2
    inv = 1.0 / (ROPE_THETA ** (jnp.arange(half, dtype=F32) / half))
    ang = jnp.arange(seq, dtype=F32)[:, None] * inv[None, :]
    cos, sin = jnp.cos(ang), jnp.sin(ang)
    z = lambda c: jnp.zeros((seq, c), F32)
    cos_t = jnp.concatenate([jnp.ones((seq, MLA_NOPE), F32), cos, cos, z(pad)], axis=1)
    sa_t = jnp.concatenate([z(MLA_NOPE), -sin, z(half), z(pad)], axis=1)
    sb_t = jnp.concatenate([z(MLA_NOPE), z(half), sin, z(pad)], axis=1)

    nseq = seq // tm
    row = lambda w: pl.BlockSpec((tm, w), lambda i: (i, 0))
    col = lambda r: pl.BlockSpec((r, tm), lambda i: (0, i))
    tab = pl.BlockSpec((tm, LANES), lambda i: (i % nseq, 0))
    tabT = pl.BlockSpec((half, tm), lambda i: (0, i % nseq))
    q_scale = (MLA_QK ** -0.5) * math.log2(math.e)
    outs = pl.pallas_call(
        functools.partial(_inproj_kernel, q_scale=q_scale),
        grid=(T // tm,),
        in_specs=[row(D), _full((1, D)), _full(w_latT.shape), _full(w_kr.shape), _full(w_h.shape),
                  _full((ql, 1)), _full((kvl, 1)), _full((1, kvl)), _full(wuqT.shape), _full(wuk.shape),
                  _full(wuvT.shape), _full((LANES, 1)), _full((1, LANES)), _full((H * LANES, 1)),
                  tabT, tabT, tab, tab, tab],
        out_specs=[col(H * LANES), row(H * LANES), col(H * LANES), row(hw),
                   pl.BlockSpec((2, tm, hw), lambda i: (0, i, 0)), row(hw), row(hw)],
        out_shape=[jax.ShapeDtypeStruct((H * LANES, T), BF16), jax.ShapeDtypeStruct((T, H * LANES), BF16),
                   jax.ShapeDtypeStruct((H * LANES, T), BF16), jax.ShapeDtypeStruct((T, hw), BF16),
                   jax.ShapeDtypeStruct((2, T, hw), BF16), jax.ShapeDtypeStruct((T, hw), BF16),
                   jax.ShapeDtypeStruct((T, hw), BF16)],
        compiler_params=_params(("parallel",)),
        name="inproj",
    )(x2, norm_mix.reshape(1, D), w_latT, w_kr, w_h, q_lat_norm.reshape(ql, 1), kv_lat_norm.reshape(kvl, 1),
      kv_lat_norm.reshape(1, kvl), wuqT, wuk, wuvT, qngc, kng, vonec, cos.T, sin.T, cos_t, sa_t, sb_t)
    return outs


def _attn_kernel(qT_ref, k_ref, vT_ref, o_ref, s00, s01, s10, s11, *, tk):
    tq = qT_ref.shape[1]
    nk = k_ref.shape[0] // tk
    qTs = [qT_ref[j * LANES:(j + 1) * LANES, :] for j in range(2)]
    s_bufs = ((s00, s01), (s10, s11))

    def scores(j, slot, c):
        r0 = pl.multiple_of(c * tk, tk)
        sT = jnp.dot(k_ref[pl.ds(r0, tk), j * LANES:(j + 1) * LANES], qTs[j], preferred_element_type=F32)
        s_bufs[j][slot][...] = sT
        return jnp.max(sT, axis=0, keepdims=True)

    def absorb(j, slot, c, m, acc, mx):
        r0 = pl.multiple_of(c * tk, tk)
        m_new = jnp.maximum(m, mx)
        pT = jnp.exp2(s_bufs[j][slot][...] - m_new).astype(BF16)
        pv = jnp.dot(vT_ref[j * LANES:(j + 1) * LANES, pl.ds(r0, tk)], pT, preferred_element_type=F32)
        return m_new, jnp.exp2(m - m_new) * acc + pv

    def step(c, slot, state, prefetch):
        mx_next = [scores(j, 1 - slot, c + 1) if prefetch else state[j][2] for j in range(2)]
        new = []
        for j in range(2):
            m, acc, mx = state[j]
            m, acc = absorb(j, slot, c, m, acc, mx)
            new.append((m, acc, mx_next[j]))
        return tuple(new)

    def pair(i, state):
        c = 2 * i
        return step(c + 1, 1, step(c, 0, state, True), True)

    state = tuple((jnp.full((1, tq), -jnp.inf, F32), jnp.zeros((LANES, tq), F32), scores(j, 0, 0))
                  for j in range(2))
    state = lax.fori_loop(0, nk // 2 - 1, pair, state)
    state = step(nk - 1, 1, step(nk - 2, 0, state, True), False)
    acc0, acc1 = state[0][1], state[1][1]
    oT = jnp.concatenate([acc0[0:MLA_V] / acc0[MLA_V:MLA_V + 1], acc1[0:MLA_V] / acc1[MLA_V:MLA_V + 1]], axis=0)
    o_ref[...] = oT.T.astype(o_ref.dtype)


def _attn_bounded_kernel(shift_ref, qT_ref, k_ref, vT_ref, o_ref, p00, p01, p10, p11, *, tk):
    tq = qT_ref.shape[1]
    nk = k_ref.shape[0] // tk
    shift = shift_ref[0]
    qTs = [qT_ref[j * LANES:(j + 1) * LANES, :] for j in range(2)]
    p_bufs = ((p00, p01), (p10, p11))

    def probs(j, slot, c):
        r0 = pl.multiple_of(c * tk, tk)
        sT = jnp.dot(k_ref[pl.ds(r0, tk), j * LANES:(j + 1) * LANES], qTs[j], preferred_element_type=F32)
        p_bufs[j][slot][...] = jnp.exp2(sT - shift).astype(BF16)

    def absorb(j, slot, c, acc):
        r0 = pl.multiple_of(c * tk, tk)
        return acc + jnp.dot(vT_ref[j * LANES:(j + 1) * LANES, pl.ds(r0, tk)], p_bufs[j][slot][...],
                             preferred_element_type=F32)

    def step(c, slot, accs, prefetch):
        if prefetch:
            for j in range(2):
                probs(j, 1 - slot, c + 1)
        return tuple(absorb(j, slot, c, accs[j]) for j in range(2))

    def pair(i, accs):
        c = 2 * i
        return step(c + 1, 1, step(c, 0, accs, True), True)

    for j in range(2):
        probs(j, 0, 0)
    accs = tuple(jnp.zeros((LANES, tq), F32) for _ in range(2))
    accs = lax.fori_loop(0, nk // 2 - 1, pair, accs)
    acc0, acc1 = step(nk - 1, 1, step(nk - 2, 0, accs, True), False)
    oT = jnp.concatenate([acc0[0:MLA_V] / acc0[MLA_V:MLA_V + 1], acc1[0:MLA_V] / acc1[MLA_V:MLA_V + 1]], axis=0)
    o_ref[...] = oT.T.astype(o_ref.dtype)


def _attention(qT, k3, vT, score_bound, tq=512, tk=512):
    B, S, _ = k3.shape
    tq, tk = min(tq, S), min(tk, S)
    nq = S // tq
    hp = MLA_HEADS // 2
    specs = dict(
        grid=(B, hp, nq),
        in_specs=[pl.BlockSpec((2 * LANES, tq), lambda b, h, i: (h, b * nq + i)),
                  pl.BlockSpec((None, S, 2 * LANES), lambda b, h, i: (b, 0, h)),
                  pl.BlockSpec((2 * LANES, S), lambda b, h, i: (h, b))],
        out_specs=pl.BlockSpec((None, tq, 2 * MLA_V), lambda b, h, i: (b, i, h)),
        out_shape=jax.ShapeDtypeStruct((B, S, MLA_HEADS * MLA_V), BF16),
        compiler_params=_params(("parallel", "parallel", "arbitrary")),
    )

    def bounded():
        tkb = max(min(4 * tk, S // 2), tk)
        in_specs = [pl.BlockSpec(memory_space=pltpu.SMEM)] + specs["in_specs"]
        return pl.pallas_call(functools.partial(_attn_bounded_kernel, tk=tkb), name="attention_bounded",
                              scratch_shapes=[pltpu.VMEM((tkb, tq), BF16)] * 4,
                              **{**specs, "in_specs": in_specs})(score_bound.reshape(1), qT, k3, vT)

    def online():
        return pl.pallas_call(functools.partial(_attn_kernel, tk=tk), name="attention",
                              scratch_shapes=[pltpu.VMEM((tk, tq), F32)] * 4, **specs)(qT, k3, vT)

    return lax.cond(score_bound <= ATTN_MAX_FIXED_SHIFT, bounded, online)


def _hgrn_kernel(hq_ref, z_ref, hi_ref, lb_ref, tri_ref, o_ref, st_ref, kk_scr, b_scr, edge_scr, qh_scr, a_scr,
                 u_scr, qf_scr, of_scr):
    d = pl.program_id(1)
    C = HG_CHUNK
    tl = hq_ref.shape[0]
    nc = tl // C
    heads = [slice(h * HG_DK, (h + 1) * HG_DK) for h in range(HG_HEADS)]

    @pl.when(pl.program_id(2) == 0)
    def _():
        st_ref[...] = jnp.zeros_like(st_ref)

    one_m_lb = 1.0 - lb_ref[...]
    tri = tri_ref[...]
    keep = tri > 0
    fwd = d == 0

    worst = jnp.zeros_like(one_m_lb)
    for c in range(nc):
        rows = slice(c * C, (c + 1) * C)
        kk = one_m_lb * jax.nn.sigmoid(-z_ref[rows, :].astype(F32))
        g = jnp.log(1.0 - kk)
        g_hi = g.astype(BF16)
        g_lo = (g - g_hi.astype(F32)).astype(BF16)
        b = jnp.dot(tri, g_hi, preferred_element_type=F32) + jnp.dot(tri, g_lo, preferred_element_type=F32)
        b_edge = jnp.where(fwd, b[C - 1:C, :], b[0:1, :])
        kk_scr[rows, :] = kk
        b_scr[rows, :] = b
        edge_scr[c] = b_edge
        worst = jnp.maximum(worst, -b_edge)
    safe = jnp.max(worst) < HG_SAFE_EXPONENT

    @pl.when(safe)
    def _():
        for c in range(nc):
            rows = slice(c * C, (c + 1) * C)
            kk = kk_scr[rows, :]
            b = b_scr[rows, :]
            hq = hq_ref[rows, :].astype(F32)
            qh = (hq * jax.nn.sigmoid(hq) * jnp.exp(b)).astype(BF16)
            kt = (kk * jnp.exp(-b)).astype(BF16)
            ks = (kk * jnp.exp(edge_scr[c] - b)).astype(BF16)
            v = hi_ref[rows, :]
            qh_scr[rows, :] = qh
            for h, sl in enumerate(heads):
                a = lax.dot_general(qh[:, sl], kt[:, sl], NT_DIMS, preferred_element_type=F32)
                a_scr[c, h] = jnp.where(keep, a, 0.0).astype(BF16)
                u_scr[c, h] = lax.dot_general(v[:, sl], ks[:, sl], TN_DIMS, preferred_element_type=F32)
        st = [st_ref[h] for h in range(HG_HEADS)]
        for p in range(nc):
            c = jnp.where(fwd, p, nc - 1 - p)
            r0 = pl.multiple_of(c * C, C)
            dec = jnp.exp(edge_scr[c])
            qh = qh_scr[pl.ds(r0, C), :]
            v = hi_ref[pl.ds(r0, C), :]
            for h, sl in enumerate(heads):
                o = jnp.dot(a_scr[c, h], v[:, sl], preferred_element_type=F32)
                o = o + lax.dot_general(qh[:, sl], st[h].astype(BF16), NT_DIMS, preferred_element_type=F32)
                o_ref[pl.ds(r0, C), sl] = o.astype(o_ref.dtype)
                st[h] = st[h] * dec[:, sl] + u_scr[c, h]
        for h in range(HG_HEADS):
            st_ref[h] = st[h]

    @pl.when(jnp.logical_not(safe))
    def _():
        hq = hq_ref[...].astype(F32)
        qf_scr[...] = hq * jax.nn.sigmoid(hq)
        first = lax.broadcasted_iota(I32, (16, HG_DK), 0) == 0

        def row(i, carry):
            t = jnp.where(fwd, i, tl - 1 - i)
            kk = kk_scr[pl.ds(t, 1), :]
            f = 1.0 - kk
            q = qf_scr[pl.ds(t, 1), :]
            g0 = pl.multiple_of((t // 16) * 16, 16)
            v = hi_ref[pl.ds(g0, 16), :].astype(F32)
            v = jnp.sum(jnp.where(lax.broadcasted_iota(I32, v.shape, 0) == t % 16, v, 0.0), axis=0, keepdims=True)
            outs = []
            for h, sl in enumerate(heads):
                pad = lambda x: jnp.where(first, jnp.broadcast_to(x[:, sl], (16, HG_DK)), 0.0).astype(BF16)
                st = st_ref[h] * f[:, sl] + lax.dot_general(pad(v), pad(kk), TN_DIMS, preferred_element_type=F32)
                st_ref[h] = st
                outs.append(lax.dot_general(pad(q), st.astype(BF16), NT_DIMS, preferred_element_type=F32)[0:1])
            of_scr[pl.ds(t, 1), :] = jnp.concatenate(outs, axis=1)
            return carry

        lax.fori_loop(0, tl, row, 0)
        o_ref[...] = of_scr[...].astype(o_ref.dtype)


def _hgrn(hq3, fz4, hi3, lb, tl=512):
    B, S, W = hq3.shape
    tl = min(tl, S)
    nt = S // tl
    C = HG_CHUNK
    r = lax.broadcasted_iota(I32, (C, C), 0)
    c = lax.broadcasted_iota(I32, (C, C), 1)
    tri = jnp.stack([r >= c, r <= c]).astype(BF16)
    tile = lambda b, d, i: (b, i + d * (nt - 1 - 2 * i), 0)
    return pl.pallas_call(
        _hgrn_kernel,
        grid=(B, 2, nt),
        in_specs=[pl.BlockSpec((None, tl, W), tile),
                  pl.BlockSpec((None, None, tl, W), lambda b, d, i: (d, b, i + d * (nt - 1 - 2 * i), 0)),
                  pl.BlockSpec((None, tl, W), tile),
                  pl.BlockSpec((None, 1, W), lambda b, d, i: (d, 0, 0)),
                  pl.BlockSpec((None, C, C), lambda b, d, i: (d, 0, 0))],
        out_specs=pl.BlockSpec((None, None, tl, W), lambda b, d, i: (d, b, i + d * (nt - 1 - 2 * i), 0)),
        out_shape=jax.ShapeDtypeStruct((2, B, S, W), BF16),
        scratch_shapes=[pltpu.VMEM((HG_HEADS, HG_DV, HG_DK), F32),
                        pltpu.VMEM((tl, W), F32), pltpu.VMEM((tl, W), F32),
                        pltpu.VMEM((tl // C, 1, W), F32),
                        pltpu.VMEM((tl, W), BF16),
                        pltpu.VMEM((tl // C, HG_HEADS, C, C), BF16),
                        pltpu.VMEM((tl // C, HG_HEADS, HG_DV, HG_DK), F32),
                        pltpu.VMEM((tl, W), F32), pltpu.VMEM((tl, W), F32)],
        compiler_params=_params(("parallel", "parallel", "arbitrary")),
        name="hgrn",
    )(hq3, fz4, hi3, lb, tri)


def _outproj_kernel(x_ref, a_ref, o_ref, hg_ref, ong_ref, wa_ref, wr_ref, g2_ref, wrt_ref, brt_ref,
                    h1_out, n2_out, route_out):
    o = o_ref[0].astype(F32) + o_ref[1].astype(F32)
    hg = hg_ref[...].astype(F32)
    gate = hg * jax.nn.sigmoid(hg)
    ong = ong_ref[...]
    parts = []
    for h in range(HG_HEADS):
        sl = slice(h * HG_DV, (h + 1) * HG_DV)
        oh = o[:, sl]
        parts.append((oh * lax.rsqrt(jnp.mean(oh * oh, axis=-1, keepdims=True) + EPS) * ong * gate[:, sl]).astype(BF16))
    r = jnp.concatenate(parts, axis=1)
    h1 = x_ref[...] + jnp.dot(a_ref[...], wa_ref[...], preferred_element_type=F32)
    h1 = h1 + jnp.dot(r, wr_ref[...], preferred_element_type=F32)
    h1_out[...] = h1
    n2 = h1 * lax.rsqrt(jnp.mean(h1 * h1, axis=-1, keepdims=True) + EPS) * g2_ref[...]
    n2_out[...] = n2

    n2_hi = n2.astype(BF16)
    n2_lo = (n2 - n2_hi.astype(F32)).astype(BF16)
    l_hi = jnp.dot(n2_hi, wrt_ref[...], preferred_element_type=F32)
    l_lo = jnp.dot(n2_lo, wrt_ref[:, 0:LANES], preferred_element_type=F32)
    logits = l_hi[:, 0:LANES] + l_hi[:, LANES:2 * LANES] + l_lo + brt_ref[...]
    tm = logits.shape[0]
    lane = lax.broadcasted_iota(I32, (tm, LANES), 1)
    ninf = -jnp.inf
    is_g = lane < N_GROUPS
    gl = jnp.where(is_g, logits, ninf)
    gmax = jnp.max(gl, axis=-1, keepdims=True)
    gidx = jnp.min(jnp.where(gl == gmax, lane, LANES), axis=-1, keepdims=True)
    g_w = 1.0 / jnp.sum(jnp.where(is_g, jnp.exp(logits - gmax), 0.0), axis=-1, keepdims=True)
    lo = N_GROUPS + EXPERTS_PER_GROUP * gidx
    el = jnp.where((lane >= lo) & (lane < lo + EXPERTS_PER_GROUP), logits, ninf)
    m1 = jnp.max(el, axis=-1, keepdims=True)
    i1 = jnp.min(jnp.where(el == m1, lane, LANES), axis=-1, keepdims=True)
    el2 = jnp.where(lane == i1, ninf, el)
    m2 = jnp.max(el2, axis=-1, keepdims=True)
    i2 = jnp.min(jnp.where(el2 == m2, lane, LANES), axis=-1, keepdims=True)
    t = jnp.exp(m2 - m1)
    w1 = 1.0 / (1.0 + t)
    w2 = t / (1.0 + t)
    e1 = (i1 - N_GROUPS).astype(F32)
    e2 = (i2 - N_GROUPS).astype(F32)
    route = jnp.where(lane == 0, e1, jnp.where(lane == 1, e2, jnp.where(lane == 2, g_w * w1,
                      jnp.where(lane == 3, g_w * w2, 0.0))))
    route_out[...] = route


def _outproj(x2, a2, o3, hg2, hg_out_norm, w_out, norm_ffn, w_group, b_group, w_router, b_router, tm=512):
    T, D = x2.shape
    tm = min(tm, T)
    wa = w_out[:MLA_HEADS * MLA_V].astype(BF16)
    wr = w_out[MLA_HEADS * MLA_V:].astype(BF16)
    npad = LANES - N_GROUPS - N_EXPERTS
    wrt = jnp.concatenate([w_group, w_router, jnp.zeros((D, npad), F32)], axis=1)
    wrt_hi = wrt.astype(BF16)
    wrt = jnp.concatenate([wrt_hi, (wrt - wrt_hi.astype(F32)).astype(BF16)], axis=1)
    brt = jnp.concatenate([b_group, b_router, jnp.zeros((npad,), F32)]).reshape(1, LANES)
    row = lambda w: pl.BlockSpec((tm, w), lambda i: (i, 0))
    hw = HG_HEADS * HG_DV
    return pl.pallas_call(
        _outproj_kernel,
        grid=(T // tm,),
        in_specs=[row(D), row(a2.shape[1]), pl.BlockSpec((2, tm, hw), lambda i: (0, i, 0)), row(hw),
                  _full((1, HG_DV)), _full(wa.shape), _full(wr.shape), _full((1, D)), _full(wrt.shape),
                  _full((1, LANES))],
        out_specs=[row(D), row(D), row(LANES)],
        out_shape=[jax.ShapeDtypeStruct((T, D), F32), jax.ShapeDtypeStruct((T, D), F32),
                   jax.ShapeDtypeStruct((T, LANES), F32)],
        compiler_params=_params(("parallel",)),
        name="outproj",
    )(x2, a2, o3, hg2, hg_out_norm.reshape(1, HG_DV), wa, wr, norm_ffn.reshape(1, D), wrt, brt)


def _plan_kernel(route_ref, tri_ref, upper_ref, lpos_out, lposT_out, runs_out, glob_out, tot_ref, base_ref):
    p = pl.program_id(0)
    i = pl.program_id(1)
    tm = route_ref.shape[0]
    lane = lax.broadcasted_iota(I32, (tm, LANES), 1)
    route = route_ref[...]
    is1 = lane == route[:, 0:1].astype(I32)
    is2 = lane == route[:, 1:2].astype(I32)
    onehot = jnp.where(is1 | is2, 1.0, 0.0)
    units = jnp.ceil(jnp.sum(onehot, axis=0, keepdims=True) * (1.0 / RUN_ROWS))
    sub = lax.broadcasted_iota(I32, (8, LANES), 0)
    rows3 = lambda a, b, c: jnp.where(sub == 0, a, jnp.where(sub == 1, b, jnp.where(sub == 2, c, 0.0)))

    @pl.when((p == 0) & (i == 0))
    def _():
        tot_ref[...] = jnp.zeros_like(tot_ref)

    @pl.when(p == 0)
    def _():
        tot_ref[...] += units

    @pl.when((p == 1) & (i == 0))
    def _():
        tot = tot_ref[...]
        block_units = MOE_BLOCK // RUN_ROWS
        padded = jnp.ceil(tot * (1.0 / block_units)) * block_units
        start = jnp.dot(jnp.broadcast_to(padded, (8, LANES)), upper_ref[...], preferred_element_type=F32,
                        precision=lax.Precision.HIGHEST)[0:1]
        base_ref[...] = start
        glob_out[...] = rows3(start + padded, start + tot, padded - tot).astype(I32)

    @pl.when(p == 1)
    def _():
        before = jnp.dot(tri_ref[...], onehot.astype(BF16), preferred_element_type=F32)
        local = jnp.dot(jnp.broadcast_to(units, (8, LANES)).astype(BF16), upper_ref[...].astype(BF16),
                        preferred_element_type=F32)[0:1] * RUN_ROWS
        pos = local + before
        p1 = jnp.sum(jnp.where(is1, pos, 0.0), axis=-1, keepdims=True)
        p2 = jnp.sum(jnp.where(is2, pos, 0.0), axis=-1, keepdims=True)
        slab = jnp.where(lane == 0, p1, jnp.where(lane == 1, p2, 0.0))
        lpos_out[...] = slab.astype(I32)
        lposT_out[...] = slab.T[0:8].astype(I32)
        runs_out[...] = rows3(local, units, base_ref[...] * RUN_ROWS).astype(I32)
        base_ref[...] += units


def _plan(route, tm):
    T = route.shape[0]
    nt = T // tm
    r = lax.broadcasted_iota(I32, (tm, tm), 0)
    c = lax.broadcasted_iota(I32, (tm, tm), 1)
    tri = (r > c).astype(BF16)
    ru = lax.broadcasted_iota(I32, (LANES, LANES), 0)
    cu = lax.broadcasted_iota(I32, (LANES, LANES), 1)
    upper = (ru < cu).astype(F32)
    return pl.pallas_call(
        _plan_kernel,
        grid=(2, nt),
        in_specs=[pl.BlockSpec((tm, LANES), lambda p, i: (i, 0)), _full((tm, tm)), _full((LANES, LANES))],
        out_specs=[pl.BlockSpec((tm, LANES), lambda p, i: (i * p, 0)),
                   pl.BlockSpec((8, tm), lambda p, i: (0, i * p)),
                   pl.BlockSpec((None, 8, LANES), lambda p, i: (i * p, 0, 0)),
                   _full((8, LANES))],
        out_shape=[jax.ShapeDtypeStruct((T, LANES), I32), jax.ShapeDtypeStruct((8, T), I32),
                   jax.ShapeDtypeStruct((nt, 8, LANES), I32), jax.ShapeDtypeStruct((8, LANES), I32)],
        scratch_shapes=[pltpu.VMEM((1, LANES), F32), pltpu.VMEM((1, LANES), F32)],
        compiler_params=_params(("arbitrary", "arbitrary")),
        name="plan",
    )(route, tri, upper)


def _for_each_run_unit(tile, start_a_ref, units_ref, start_b_ref, fn):
    def per_expert(e, total):
        j = tile * N_EXPERTS + e
        a0, b0, n = start_a_ref[j], start_b_ref[j], units_ref[j]

        def per_unit(u, carry):
            fn(pl.multiple_of(a0 + u * RUN_ROWS, RUN_ROWS), pl.multiple_of(b0 + u * RUN_ROWS, RUN_ROWS))
            return carry

        lax.fori_loop(0, n, per_unit, 0)
        return total + n

    return lax.fori_loop(0, N_EXPERTS, per_expert, 0)


def _dispatch_kernel(ls_ref, un_ref, gd_ref, ts_ref, tu_ref, lposT_ref, n2_ref, xs_out, xl, sems):
    i = pl.program_id(0)
    nt = pl.num_programs(0)
    rows = xl.shape[1]
    tm = n2_ref.shape[0]
    slot = i % 2
    lp = lposT_ref[...]
    r = lax.broadcasted_iota(I32, (rows, tm), 0)
    pick = jnp.where((r == lp[0:1, :]) | (r == lp[1:2, :]), 1.0, 0.0).astype(BF16)
    xl[slot] = jnp.dot(pick, n2_ref[...].astype(BF16), preferred_element_type=F32)

    def unit_copy(s, src, dst):
        return pltpu.make_async_copy(xl.at[s, pl.ds(src, RUN_ROWS)], xs_out.at[pl.ds(dst, RUN_ROWS)], sems.at[s])

    def start_unit(src, dst):
        unit_copy(slot, src, dst).start()

    def wait_tile(tile, s):
        def count(e, total):
            return total + un_ref[tile * N_EXPERTS + e]

        def wait_unit(u, carry):
            unit_copy(s, 0, 0).wait()
            return carry

        lax.fori_loop(0, lax.fori_loop(0, N_EXPERTS, count, 0), wait_unit, 0)

    _for_each_run_unit(i, ls_ref, un_ref, gd_ref, start_unit)

    @pl.when(i > 0)
    def _():
        wait_tile(i - 1, 1 - slot)

    @pl.when(i == nt - 1)
    def _():
        wait_tile(i, slot)
        xl[0] = jnp.zeros((rows, xl.shape[2]), xl.dtype)
        sem = sems.at[0]

        def zero_copy(dst, n):
            return pltpu.make_async_copy(xl.at[0, pl.ds(0, n)], xs_out.at[pl.ds(dst, n)], sem)

        def per_expert(e, total):
            d0, n = ts_ref[e], tu_ref[e]

            def per_unit(u, carry):
                zero_copy(pl.multiple_of(d0 + u * RUN_ROWS, RUN_ROWS), RUN_ROWS).start()
                return carry

            lax.fori_loop(0, n, per_unit, 0)
            return total + n

        def wait_unit_zero(u, carry):
            zero_copy(0, RUN_ROWS).wait()
            return carry

        lax.fori_loop(0, lax.fori_loop(0, N_EXPERTS, per_expert, 0), wait_unit_zero, 0)

        last = N_EXPERTS - 1
        first_unused = (ts_ref[last] + tu_ref[last] * RUN_ROWS) // MOE_BLOCK
        n_blocks = xs_out.shape[0] // MOE_BLOCK

        def start_block(b, carry):
            zero_copy(pl.multiple_of(b * MOE_BLOCK, MOE_BLOCK), MOE_BLOCK).start()
            return carry

        def wait_block(b, carry):
            zero_copy(0, MOE_BLOCK).wait()
            return carry

        lax.fori_loop(first_unused, n_blocks, start_block, 0)
        lax.fori_loop(first_unused, n_blocks, wait_block, 0)


def _dispatch(n2, lposT, tables, n_rows, tm):
    T, D = n2.shape
    local_rows = 2 * tm + N_EXPERTS * RUN_ROWS
    grid_spec = pltpu.PrefetchScalarGridSpec(
        num_scalar_prefetch=5,
        grid=(T // tm,),
        in_specs=[pl.BlockSpec((8, tm), lambda i, *_: (0, i)),
                  pl.BlockSpec((tm, D), lambda i, *_: (i, 0))],
        out_specs=pl.BlockSpec(memory_space=pl.ANY),
        scratch_shapes=[pltpu.VMEM((2, local_rows, D), F32), pltpu.SemaphoreType.DMA((2,))],
    )
    return pl.pallas_call(
        _dispatch_kernel,
        grid_spec=grid_spec,
        out_shape=jax.ShapeDtypeStruct((n_rows, D), F32),
        compiler_params=_params(("arbitrary",), has_side_effects=True),
        name="dispatch",
    )(*tables, lposT, n2)


def _expert_kernel(be_ref, nused_ref, x_ref, wgu_ref, wd_ref, y_ref):
    i = pl.program_id(0)
    de = wd_ref.shape[0]
    half = x_ref.shape[0] // 2

    @pl.when(i < nused_ref[0])
    def _():
        gu = [jnp.dot(x_ref[r * half:(r + 1) * half, :].astype(BF16), wgu_ref[...], preferred_element_type=F32)
              for r in range(2)]
        for r in range(2):
            g, u = gu[r][:, 0:de], gu[r][:, de:2 * de]
            hmid = (g * jax.nn.sigmoid(g) * u).astype(BF16)
            y_ref[r * half:(r + 1) * half, :] = jnp.dot(hmid, wd_ref[...], preferred_element_type=F32)

    @pl.when(i >= nused_ref[0])
    def _():
        y_ref[...] = jnp.zeros_like(y_ref)


def _experts(xs, block_e, nused, w_gate, w_up, w_down):
    P, D = xs.shape
    nb = P // MOE_BLOCK
    de = w_gate.shape[2]
    grid_spec = pltpu.PrefetchScalarGridSpec(
        num_scalar_prefetch=2,
        grid=(nb,),
        in_specs=[pl.BlockSpec((MOE_BLOCK, D), lambda i, be, nu: (jnp.maximum(jnp.minimum(i, nu[0] - 1), 0), 0)),
                  pl.BlockSpec((None, D, 2 * de), lambda i, be, nu: (be[i], 0, 0)),
                  pl.BlockSpec((None, de, D), lambda i, be, nu: (be[i], 0, 0))],
        out_specs=pl.BlockSpec((MOE_BLOCK, D), lambda i, be, nu: (i, 0)),
    )
    return pl.pallas_call(
        _expert_kernel,
        grid_spec=grid_spec,
        out_shape=jax.ShapeDtypeStruct((P, D), F32),
        compiler_params=_params(("arbitrary",)),
        name="experts",
    )(block_e, nused, xs, jnp.concatenate([w_gate, w_up], axis=2).astype(BF16), w_down.astype(BF16))


def _combine_kernel(ls_ref, un_ref, gd_ref, lpos_ref, route_ref, h1_ref, ys_ref, out_ref, yl, sems):
    i = pl.program_id(0)
    nt = pl.num_programs(0)
    tm = h1_ref.shape[0]
    rows = yl.shape[1]
    slot = i % 2

    def unit_copy(s, src, dst):
        return pltpu.make_async_copy(ys_ref.at[pl.ds(src, RUN_ROWS)], yl.at[s, pl.ds(dst, RUN_ROWS)], sems.at[s])

    def gather(tile, s):
        def start_unit(src, dst):
            unit_copy(s, src, dst).start()
        _for_each_run_unit(tile, gd_ref, un_ref, ls_ref, start_unit)

    @pl.when(i == 0)
    def _():
        yl[...] = jnp.zeros_like(yl)
        gather(0, 0)

    @pl.when(i + 1 < nt)
    def _():
        gather(i + 1, 1 - slot)

    def count(e, total):
        return total + un_ref[i * N_EXPERTS + e]

    def wait_unit(u, carry):
        unit_copy(slot, 0, 0).wait()
        return carry

    lax.fori_loop(0, lax.fori_loop(0, N_EXPERTS, count, 0), wait_unit, 0)

    lp = lpos_ref[...]
    route = route_ref[...]
    r = lax.broadcasted_iota(I32, (tm, rows), 1)
    w = jnp.where(r == lp[:, 0:1], route[:, 2:3], 0.0) + jnp.where(r == lp[:, 1:2], route[:, 3:4], 0.0)
    out_ref[...] = h1_ref[...] + jnp.dot(w.astype(BF16), yl[slot].astype(BF16), preferred_element_type=F32)


def _combine(h1, route, lpos, tables, ys, tm):
    T, D = h1.shape
    local_rows = 2 * tm + N_EXPERTS * RUN_ROWS
    grid_spec = pltpu.PrefetchScalarGridSpec(
        num_scalar_prefetch=3,
        grid=(T // tm,),
        in_specs=[pl.BlockSpec((tm, LANES), lambda i, *_: (i, 0)),
                  pl.BlockSpec((tm, LANES), lambda i, *_: (i, 0)),
                  pl.BlockSpec((tm, D), lambda i, *_: (i, 0)),
                  pl.BlockSpec(memory_space=pl.ANY)],
        out_specs=pl.BlockSpec((tm, D), lambda i, *_: (i, 0)),
        scratch_shapes=[pltpu.VMEM((2, local_rows, D), F32), pltpu.SemaphoreType.DMA((2,))],
    )
    return pl.pallas_call(
        _combine_kernel,
        grid_spec=grid_spec,
        out_shape=jax.ShapeDtypeStruct((T, D), F32),
        compiler_params=_params(("arbitrary",)),
        name="combine",
    )(*tables, lpos, route, h1, ys)


def _mixers(x, norm_mix, w_in, q_lat_norm, kv_lat_norm, w_uq, w_ukv, q_norm, k_norm, lb_logits, layer):
    B, S, D = x.shape
    T = B * S
    q, k, v, hq, fz, hi, hg = _inproj(x.reshape(T, D), S, norm_mix, w_in, q_lat_norm, kv_lat_norm,
                                      w_uq, w_ukv, q_norm, k_norm)
    score_bound = (math.sqrt(MLA_QK) * math.log2(math.e) * 1.02) * jnp.max(jnp.abs(q_norm)) * jnp.max(jnp.abs(k_norm))
    a = _attention(q, k.reshape(B, S, -1), v, score_bound.astype(F32))
    lb = jnp.cumsum(jax.nn.softmax(lb_logits.astype(F32), axis=0), axis=0)[layer]
    hw = hq.shape[1]
    o = _hgrn(hq.reshape(B, S, hw), fz.reshape(2, B, S, hw), hi.reshape(B, S, hw), lb.reshape(2, 1, hw))
    return a.reshape(T, -1), o.reshape(2, T, hw), hg


def _moe(h1, n2, route, w_gate, w_up, w_down, tm=512):
    T, D = h1.shape
    tm = min(tm, T)
    nt = T // tm
    lpos, lposT, runs, glob = _plan(route, tm)
    per_run = lambda row: runs[:, row, :N_EXPERTS].reshape(-1)
    tables = (per_run(0), per_run(1), per_run(2))
    tails = (glob[1, :N_EXPERTS] * RUN_ROWS, glob[2, :N_EXPERTS])
    n_rows = -(-(2 * T + N_EXPERTS * RUN_ROWS * nt) // MOE_BLOCK) * MOE_BLOCK + N_EXPERTS * MOE_BLOCK
    nb = n_rows // MOE_BLOCK
    pend = glob[0, :N_EXPERTS] * RUN_ROWS
    block_row0 = jnp.arange(nb, dtype=I32) * MOE_BLOCK
    block_e = jnp.minimum(jnp.sum((pend[None, :] <= block_row0[:, None]).astype(I32), axis=1), N_EXPERTS - 1)
    nused = pend[N_EXPERTS - 1:] // MOE_BLOCK
    xs = _dispatch(n2, lposT, tables + tails, n_rows, tm)
    ys = _experts(xs, block_e, nused, w_gate, w_up, w_down)
    return _combine(h1, route, lpos, tables, ys, tm)


def kernel(x, norm_mix, w_in, q_lat_norm, kv_lat_norm, w_uq, w_ukv, q_norm, k_norm, lb_logits, hg_out_norm,
           w_out, norm_ffn, w_group, b_group, w_router, b_router, w_gate, w_up, w_down):
    B, S, D = x.shape
    h = x
    for l in range(norm_mix.shape[0]):
        h2 = h.reshape(B * S, D)
        a, o, hg = _mixers(h, norm_mix[l], w_in[l], q_lat_norm[l], kv_lat_norm[l], w_uq[l], w_ukv[l],
                           q_norm[l], k_norm[l], lb_logits, l)
        h1, n2, route = _outproj(h2, a, o, hg, hg_out_norm[l], w_out[l], norm_ffn[l], w_group[l], b_group[l],
                                 w_router[l], b_router[l])
        h = _moe(h1, n2, route, w_gate[l], w_up[l], w_down[l]).reshape(B, S, D)
    return h
```

```python
import functools
import math

import jax
import jax.numpy as jnp
from jax import lax
from jax.experimental import pallas as pl
from jax.experimental.pallas import tpu as pltpu

F32 = jnp.float32
BF16 = jnp.bfloat16
I32 = jnp.int32

EPS = 1e-6
LANES = 128
VMEM_LIMIT = 48 * 1024 * 1024

MLA_HEADS = 8
MLA_NOPE = 64
MLA_ROPE = 32
MLA_QK = MLA_NOPE + MLA_ROPE
MLA_V = 64
ROPE_THETA = 10000.0
ATTN_MAX_FIXED_SHIFT = 40.0
HG_HEADS = 4
HG_DK = 128
HG_DV = 128
HG_CHUNK = 64
HG_SAFE_EXPONENT = 60.0
N_GROUPS = 4
EXPERTS_PER_GROUP = 8
N_EXPERTS = N_GROUPS * EXPERTS_PER_GROUP
MOE_BLOCK = 512
RUN_ROWS = 8
RUN_LONG_SHIFT = 2
RUN_LONG_UNITS = 1 << RUN_LONG_SHIFT

NT_DIMS = (((1,), (1,)), ((), ()))
TN_DIMS = (((0,), (0,)), ((), ()))


def _params(sem, **kw):
    return pltpu.CompilerParams(dimension_semantics=sem, vmem_limit_bytes=VMEM_LIMIT, **kw)


def _full(shape):
    n = len(shape)
    return pl.BlockSpec(shape, lambda *_: (0,) * n)


def _inproj_kernel(x_ref, g_ref, wlatT_ref, wkr_ref, wh_ref, qlgc_ref, kvlgc_ref, kvlg_ref, wuqT_ref, wuk_ref,
                   wuvT_ref, qngc_ref, kng_ref, vonec_ref, cosT_ref, sinT_ref, cos_ref, sa_ref, sb_ref,
                   qT_out, k_out, vT_out, hq_out, fz_out, hi_out, hg_out, *, q_scale):
    x = x_ref[...]
    ms = jnp.mean(x * x, axis=-1, keepdims=True)
    n = (x * lax.rsqrt(ms + EPS) * g_ref[...]).astype(BF16)

    hw = hq_out.shape[1]
    ql = qlgc_ref.shape[0]
    kvl = kvlgc_ref.shape[0]
    half = MLA_ROPE // 2

    def wide(j):
        return jnp.dot(n, wh_ref[:, j * hw:(j + 1) * hw], preferred_element_type=F32).astype(BF16)

    latT = lax.dot_general(wlatT_ref[...], n, NT_DIMS, preferred_element_type=F32)
    latk = jnp.dot(n, wkr_ref[...], preferred_element_type=F32)
    hq_out[...] = wide(0)
    fz_out[0] = wide(1)
    qlT = latT[0:ql]
    kvlT = latT[ql:ql + kvl]
    qnT = (qlT * lax.rsqrt(jnp.mean(qlT * qlT, axis=0, keepdims=True) + EPS) * qlgc_ref[...]).astype(BF16)
    kvnT = (kvlT * lax.rsqrt(jnp.mean(kvlT * kvlT, axis=0, keepdims=True) + EPS) * kvlgc_ref[...]).astype(BF16)
    q_allT = jnp.dot(wuqT_ref[...], qnT, preferred_element_type=F32)
    vT_out[...] = (jnp.dot(wuvT_ref[...], kvnT, preferred_element_type=F32) + vonec_ref[...]).astype(BF16)
    kvl_r = latk[:, 0:kvl]
    kr = latk[:, kvl:kvl + LANES]
    kvn = (kvl_r * lax.rsqrt(jnp.mean(kvl_r * kvl_r, axis=-1, keepdims=True) + EPS) * kvlg_ref[...]).astype(BF16)
    k_all = jnp.dot(kvn, wuk_ref[...], preferred_element_type=F32)
    fz_out[1] = wide(2)
    hi_out[...] = wide(3)
    hg_out[...] = wide(4)

    cosT = cosT_ref[...]
    sinT = sinT_ref[...]
    qngc = qngc_ref[...]
    for h in range(MLA_HEADS):
        t = q_allT[h * LANES:(h + 1) * LANES]
        tn = t * lax.rsqrt(jnp.sum(t * t, axis=0, keepdims=True) * (1.0 / MLA_QK) + EPS) * qngc
        x1 = tn[MLA_NOPE:MLA_NOPE + half]
        x2 = tn[MLA_NOPE + half:MLA_QK]
        rot = jnp.concatenate([tn[0:MLA_NOPE], x1 * cosT - x2 * sinT, x2 * cosT + x1 * sinT, tn[MLA_QK:LANES]],
                              axis=0)
        qT_out[h * LANES:(h + 1) * LANES, :] = (rot * q_scale).astype(BF16)

    kng = kng_ref[...]
    krg = kr * kng
    up = pltpu.roll(krg, LANES - half, axis=1)
    dn = pltpu.roll(krg, half, axis=1)
    kr_rot = krg * cos_ref[...] + up * sa_ref[...] + dn * sb_ref[...]
    kr_ssq = jnp.sum(kr * kr, axis=-1, keepdims=True)
    for h in range(MLA_HEADS):
        sl = slice(h * LANES, (h + 1) * LANES)
        t = k_all[:, sl]
        r = lax.rsqrt((jnp.sum(t * t, axis=-1, keepdims=True) + kr_ssq) * (1.0 / MLA_QK) + EPS)
        k_out[:, sl] = (r * (t * kng + kr_rot)).astype(BF16)


def _inproj(x2, seq, norm_mix, w_in, q_lat_norm, kv_lat_norm, w_uq, w_ukv, q_norm, k_norm, tm=512):
    T, D = x2.shape
    tm = min(tm, seq)
    H = MLA_HEADS
    ql, kvl = w_uq.shape[0], w_ukv.shape[0]
    hw = HG_HEADS * HG_DK
    o_kr = ql + kvl
    o_h = o_kr + MLA_ROPE
    zeros = lambda c: jnp.zeros((D, c), F32)
    w_latT = w_in[:, :o_kr].T.astype(BF16)
    w_kr = jnp.concatenate([w_in[:, ql:o_kr], zeros(MLA_NOPE), w_in[:, o_kr:o_h],
                            zeros(LANES - MLA_QK)], axis=1).astype(BF16)
    w_h = w_in[:, o_h:].astype(BF16)
    pad = LANES - MLA_QK
    wuqT = jnp.pad(w_uq.reshape(ql, H, MLA_QK), ((0, 0), (0, 0), (0, pad))).reshape(ql, H * LANES).T.astype(BF16)
    wkv = w_ukv.reshape(kvl, H, MLA_NOPE + MLA_V)
    wuk = jnp.pad(wkv[:, :, :MLA_NOPE], ((0, 0), (0, 0), (0, LANES - MLA_NOPE))).reshape(kvl, H * LANES).astype(BF16)
    wuvT = jnp.pad(wkv[:, :, MLA_NOPE:], ((0, 0), (0, 0), (0, LANES - MLA_V))).reshape(kvl, H * LANES).T.astype(BF16)
    vonec = jnp.tile(jnp.concatenate([jnp.zeros((MLA_V,), F32), jnp.ones((LANES - MLA_V,), F32)]), H)
    vonec = vonec.reshape(H * LANES, 1)
    qngc = jnp.pad(q_norm, (0, pad)).reshape(LANES, 1)
    kng = jnp.pad(k_norm, (0, pad)).reshape(1, LANES)

    half = MLA_ROPE // 2
    inv = 1.0 / (ROPE_THETA ** (jnp.arange(half, dtype=F32) / half))
    ang = jnp.arange(seq, dtype=F32)[:, None] * inv[None, :]
    cos, sin = jnp.cos(ang), jnp.sin(ang)
    z = lambda c: jnp.zeros((seq, c), F32)
    cos_t = jnp.concatenate([jnp.ones((seq, MLA_NOPE), F32), cos, cos, z(pad)], axis=1)
    sa_t = jnp.concatenate([z(MLA_NOPE), -sin, z(half), z(pad)], axis=1)
    sb_t = jnp.concatenate([z(MLA_NOPE), z(half), sin, z(pad)], axis=1)

    nseq = seq // tm
    row = lambda w: pl.BlockSpec((tm, w), lambda i: (i, 0))
    col = lambda r: pl.BlockSpec((r, tm), lambda i: (0, i))
    tab = pl.BlockSpec((tm, LANES), lambda i: (i % nseq, 0))
    tabT = pl.BlockSpec((half, tm), lambda i: (0, i % nseq))
    q_scale = (MLA_QK ** -0.5) * math.log2(math.e)
    outs = pl.pallas_call(
        functools.partial(_inproj_kernel, q_scale=q_scale),
        grid=(T // tm,),
        in_specs=[row(D), _full((1, D)), _full(w_latT.shape), _full(w_kr.shape), _full(w_h.shape),
                  _full((ql, 1)), _full((kvl, 1)), _full((1, kvl)), _full(wuqT.shape), _full(wuk.shape),
                  _full(wuvT.shape), _full((LANES, 1)), _full((1, LANES)), _full((H * LANES, 1)),
                  tabT, tabT, tab, tab, tab],
        out_specs=[col(H * LANES), row(H * LANES), col(H * LANES), row(hw),
                   pl.BlockSpec((2, tm, hw), lambda i: (0, i, 0)), row(hw), row(hw)],
        out_shape=[jax.ShapeDtypeStruct((H * LANES, T), BF16), jax.ShapeDtypeStruct((T, H * LANES), BF16),
                   jax.ShapeDtypeStruct((H * LANES, T), BF16), jax.ShapeDtypeStruct((T, hw), BF16),
                   jax.ShapeDtypeStruct((2, T, hw), BF16), jax.ShapeDtypeStruct((T, hw), BF16),
                   jax.ShapeDtypeStruct((T, hw), BF16)],
        compiler_params=_params(("parallel",)),
        name="inproj",
    )(x2, norm_mix.reshape(1, D), w_latT, w_kr, w_h, q_lat_norm.reshape(ql, 1), kv_lat_norm.reshape(kvl, 1),
      kv_lat_norm.reshape(1, kvl), wuqT, wuk, wuvT, qngc, kng, vonec, cos.T, sin.T, cos_t, sa_t, sb_t)
    return outs


def _attn_kernel(qT_ref, k_ref, vT_ref, o_ref, s00, s01, s10, s11, *, tk):
    tq = qT_ref.shape[1]
    nk = k_ref.shape[0] // tk
    qTs = [qT_ref[j * LANES:(j + 1) * LANES, :] for j in range(2)]
    s_bufs = ((s00, s01), (s10, s11))

    def scores(j, slot, c):
        r0 = pl.multiple_of(c * tk, tk)
        sT = jnp.dot(k_ref[pl.ds(r0, tk), j * LANES:(j + 1) * LANES], qTs[j], preferred_element_type=F32)
        s_bufs[j][slot][...] = sT
        return jnp.max(sT, axis=0, keepdims=True)

    def absorb(j, slot, c, m, acc, mx):
        r0 = pl.multiple_of(c * tk, tk)
        m_new = jnp.maximum(m, mx)
        pT = jnp.exp2(s_bufs[j][slot][...] - m_new).astype(BF16)
        pv = jnp.dot(vT_ref[j * LANES:(j + 1) * LANES, pl.ds(r0, tk)], pT, preferred_element_type=F32)
        return m_new, jnp.exp2(m - m_new) * acc + pv

    def step(c, slot, state, prefetch):
        mx_next = [scores(j, 1 - slot, c + 1) if prefetch else state[j][2] for j in range(2)]
        new = []
        for j in range(2):
            m, acc, mx = state[j]
            m, acc = absorb(j, slot, c, m, acc, mx)
            new.append((m, acc, mx_next[j]))
        return tuple(new)

    def pair(i, state):
        c = 2 * i
        return step(c + 1, 1, step(c, 0, state, True), True)

    state = tuple((jnp.full((1, tq), -jnp.inf, F32), jnp.zeros((LANES, tq), F32), scores(j, 0, 0))
                  for j in range(2))
    state = lax.fori_loop(0, nk // 2 - 1, pair, state)
    state = step(nk - 1, 1, step(nk - 2, 0, state, True), False)
    acc0, acc1 = state[0][1], state[1][1]
    oT = jnp.concatenate([acc0[0:MLA_V] / acc0[MLA_V:MLA_V + 1], acc1[0:MLA_V] / acc1[MLA_V:MLA_V + 1]], axis=0)
    o_ref[...] = oT.T.astype(o_ref.dtype)


def _attn_bounded_kernel(shift_ref, qT_ref, k_ref, vT_ref, o_ref, p00, p01, p10, p11, *, tk):
    tq = qT_ref.shape[1]
    nk = k_ref.shape[0] // tk
    shift = shift_ref[0]
    qTs = [qT_ref[j * LANES:(j + 1) * LANES, :] for j in range(2)]
    p_bufs = ((p00, p01), (p10, p11))

    def probs(j, slot, c):
        r0 = pl.multiple_of(c * tk, tk)
        sT = jnp.dot(k_ref[pl.ds(r0, tk), j * LANES:(j + 1) * LANES], qTs[j], preferred_element_type=F32)
        p_bufs[j][slot][...] = jnp.exp2(sT - shift).astype(BF16)

    def absorb(j, slot, c, acc):
        r0 = pl.multiple_of(c * tk, tk)
        return acc + jnp.dot(vT_ref[j * LANES:(j + 1) * LANES, pl.ds(r0, tk)], p_bufs[j][slot][...],
                             preferred_element_type=F32)

    def step(c, slot, accs, prefetch):
        if prefetch:
            for j in range(2):
                probs(j, 1 - slot, c + 1)
        return tuple(absorb(j, slot, c, accs[j]) for j in range(2))

    def pair(i, accs):
        c = 2 * i
        return step(c + 1, 1, step(c, 0, accs, True), True)

    for j in range(2):
        probs(j, 0, 0)
    accs = tuple(jnp.zeros((LANES, tq), F32) for _ in range(2))
    accs = lax.fori_loop(0, nk // 2 - 1, pair, accs)
    acc0, acc1 = step(nk - 1, 1, step(nk - 2, 0, accs, True), False)
    oT = jnp.concatenate([acc0[0:MLA_V] / acc0[MLA_V:MLA_V + 1], acc1[0:MLA_V] / acc1[MLA_V:MLA_V + 1]], axis=0)
    o_ref[...] = oT.T.astype(o_ref.dtype)


def _attention(qT, k3, vT, score_bound, tq=512, tk=512):
    B, S, _ = k3.shape
    tq, tk = min(tq, S), min(tk, S)
    nq = S // tq
    hp = MLA_HEADS // 2
    specs = dict(
        grid=(B, hp, nq),
        in_specs=[pl.BlockSpec((2 * LANES, tq), lambda b, h, i: (h, b * nq + i)),
                  pl.BlockSpec((None, S, 2 * LANES), lambda b, h, i: (b, 0, h)),
                  pl.BlockSpec((2 * LANES, S), lambda b, h, i: (h, b))],
        out_specs=pl.BlockSpec((None, tq, 2 * MLA_V), lambda b, h, i: (b, i, h)),
        out_shape=jax.ShapeDtypeStruct((B, S, MLA_HEADS * MLA_V), BF16),
        compiler_params=_params(("parallel", "parallel", "arbitrary")),
    )

    def bounded():
        tkb = max(min(4 * tk, S // 2), tk)
        in_specs = [pl.BlockSpec(memory_space=pltpu.SMEM)] + specs["in_specs"]
        return pl.pallas_call(functools.partial(_attn_bounded_kernel, tk=tkb), name="attention_bounded",
                              scratch_shapes=[pltpu.VMEM((tkb, tq), BF16)] * 4,
                              **{**specs, "in_specs": in_specs})(score_bound.reshape(1), qT, k3, vT)

    def online():
        return pl.pallas_call(functools.partial(_attn_kernel, tk=tk), name="attention",
                              scratch_shapes=[pltpu.VMEM((tk, tq), F32)] * 4, **specs)(qT, k3, vT)

    return lax.cond(score_bound <= ATTN_MAX_FIXED_SHIFT, bounded, online)


def _hgrn_kernel(hq_ref, z_ref, hi_ref, lb_ref, tri_ref, o_ref, st_ref, kk_scr, b_scr, edge_scr, qh_scr, a_scr,
                 u_scr, qf_scr, of_scr):
    d = pl.program_id(1)
    C = HG_CHUNK
    tl = hq_ref.shape[0]
    nc = tl // C
    heads = [slice(h * HG_DK, (h + 1) * HG_DK) for h in range(HG_HEADS)]

    @pl.when(pl.program_id(2) == 0)
    def _():
        st_ref[...] = jnp.zeros_like(st_ref)

    one_m_lb = 1.0 - lb_ref[...]
    tri = tri_ref[...]
    keep = tri > 0
    fwd = d == 0

    worst = jnp.zeros_like(one_m_lb)
    for c in range(nc):
        rows = slice(c * C, (c + 1) * C)
        kk = one_m_lb * jax.nn.sigmoid(-z_ref[rows, :].astype(F32))
        g = jnp.log(1.0 - kk)
        g_hi = g.astype(BF16)
        g_lo = (g - g_hi.astype(F32)).astype(BF16)
        b = jnp.dot(tri, g_hi, preferred_element_type=F32) + jnp.dot(tri, g_lo, preferred_element_type=F32)
        b_edge = jnp.where(fwd, b[C - 1:C, :], b[0:1, :])
        kk_scr[rows, :] = kk
        b_scr[rows, :] = b
        edge_scr[c] = b_edge
        worst = jnp.maximum(worst, -b_edge)
    safe = jnp.max(worst) < HG_SAFE_EXPONENT

    @pl.when(safe)
    def _():
        for c in range(nc):
            rows = slice(c * C, (c + 1) * C)
            kk = kk_scr[rows, :]
            b = b_scr[rows, :]
            hq = hq_ref[rows, :].astype(F32)
            qh = (hq * jax.nn.sigmoid(hq) * jnp.exp(b)).astype(BF16)
            kt = (kk * jnp.exp(-b)).astype(BF16)
            ks = (kk * jnp.exp(edge_scr[c] - b)).astype(BF16)
            v = hi_ref[rows, :]
            qh_scr[rows, :] = qh
            for h, sl in enumerate(heads):
                a = lax.dot_general(qh[:, sl], kt[:, sl], NT_DIMS, preferred_element_type=F32)
                a_scr[c, h] = jnp.where(keep, a, 0.0).astype(BF16)
                u_scr[c, h] = lax.dot_general(v[:, sl], ks[:, sl], TN_DIMS, preferred_element_type=F32)
        st = [st_ref[h] for h in range(HG_HEADS)]
        for p in range(nc):
            c = jnp.where(fwd, p, nc - 1 - p)
            r0 = pl.multiple_of(c * C, C)
            dec = jnp.exp(edge_scr[c])
            qh = qh_scr[pl.ds(r0, C), :]
            v = hi_ref[pl.ds(r0, C), :]
            for h, sl in enumerate(heads):
                o = jnp.dot(a_scr[c, h], v[:, sl], preferred_element_type=F32)
                o = o + lax.dot_general(qh[:, sl], st[h].astype(BF16), NT_DIMS, preferred_element_type=F32)
                o_ref[pl.ds(r0, C), sl] = o.astype(o_ref.dtype)
                st[h] = st[h] * dec[:, sl] + u_scr[c, h]
        for h in range(HG_HEADS):
            st_ref[h] = st[h]

    @pl.when(jnp.logical_not(safe))
    def _():
        hq = hq_ref[...].astype(F32)
        qf_scr[...] = hq * jax.nn.sigmoid(hq)
        first = lax.broadcasted_iota(I32, (16, HG_DK), 0) == 0

        def row(i, carry):
            t = jnp.where(fwd, i, tl - 1 - i)
            kk = kk_scr[pl.ds(t, 1), :]
            f = 1.0 - kk
            q = qf_scr[pl.ds(t, 1), :]
            g0 = pl.multiple_of((t // 16) * 16, 16)
            v = hi_ref[pl.ds(g0, 16), :].astype(F32)
            v = jnp.sum(jnp.where(lax.broadcasted_iota(I32, v.shape, 0) == t % 16, v, 0.0), axis=0, keepdims=True)
            outs = []
            for h, sl in enumerate(heads):
                pad = lambda x: jnp.where(first, jnp.broadcast_to(x[:, sl], (16, HG_DK)), 0.0).astype(BF16)
                st = st_ref[h] * f[:, sl] + lax.dot_general(pad(v), pad(kk), TN_DIMS, preferred_element_type=F32)
                st_ref[h] = st
                outs.append(lax.dot_general(pad(q), st.astype(BF16), NT_DIMS, preferred_element_type=F32)[0:1])
            of_scr[pl.ds(t, 1), :] = jnp.concatenate(outs, axis=1)
            return carry

        lax.fori_loop(0, tl, row, 0)
        o_ref[...] = of_scr[...].astype(o_ref.dtype)


def _hgrn(hq3, fz4, hi3, lb, tl=512):
    B, S, W = hq3.shape
    tl = min(tl, S)
    nt = S // tl
    C = HG_CHUNK
    r = lax.broadcasted_iota(I32, (C, C), 0)
    c = lax.broadcasted_iota(I32, (C, C), 1)
    tri = jnp.stack([r >= c, r <= c]).astype(BF16)
    tile = lambda b, d, i: (b, i + d * (nt - 1 - 2 * i), 0)
    return pl.pallas_call(
        _hgrn_kernel,
        grid=(B, 2, nt),
        in_specs=[pl.BlockSpec((None, tl, W), tile),
                  pl.BlockSpec((None, None, tl, W), lambda b, d, i: (d, b, i + d * (nt - 1 - 2 * i), 0)),
                  pl.BlockSpec((None, tl, W), tile),
                  pl.BlockSpec((None, 1, W), lambda b, d, i: (d, 0, 0)),
                  pl.BlockSpec((None, C, C), lambda b, d, i: (d, 0, 0))],
        out_specs=pl.BlockSpec((None, None, tl, W), lambda b, d, i: (d, b, i + d * (nt - 1 - 2 * i), 0)),
        out_shape=jax.ShapeDtypeStruct((2, B, S, W), BF16),
        scratch_shapes=[pltpu.VMEM((HG_HEADS, HG_DV, HG_DK), F32),
                        pltpu.VMEM((tl, W), F32), pltpu.VMEM((tl, W), F32),
                        pltpu.VMEM((tl // C, 1, W), F32),
                        pltpu.VMEM((tl, W), BF16),
                        pltpu.VMEM((tl // C, HG_HEADS, C, C), BF16),
                        pltpu.VMEM((tl // C, HG_HEADS, HG_DV, HG_DK), F32),
                        pltpu.VMEM((tl, W), F32), pltpu.VMEM((tl, W), F32)],
        compiler_params=_params(("parallel", "parallel", "arbitrary")),
        name="hgrn",
    )(hq3, fz4, hi3, lb, tri)


def _outproj_kernel(x_ref, a_ref, o_ref, hg_ref, ong_ref, wa_ref, wr_ref, g2_ref, wrt_ref, brt_ref,
                    h1_out, n2_out, route_out):
    o = o_ref[0].astype(F32) + o_ref[1].astype(F32)
    hg = hg_ref[...].astype(F32)
    gate = hg * jax.nn.sigmoid(hg)
    ong = ong_ref[...]
    parts = []
    for h in range(HG_HEADS):
        sl = slice(h * HG_DV, (h + 1) * HG_DV)
        oh = o[:, sl]
        parts.append((oh * lax.rsqrt(jnp.mean(oh * oh, axis=-1, keepdims=True) + EPS) * ong * gate[:, sl]).astype(BF16))
    r = jnp.concatenate(parts, axis=1)
    h1 = x_ref[...] + jnp.dot(a_ref[...], wa_ref[...], preferred_element_type=F32)
    h1 = h1 + jnp.dot(r, wr_ref[...], preferred_element_type=F32)
    h1_out[...] = h1
    n2 = h1 * lax.rsqrt(jnp.mean(h1 * h1, axis=-1, keepdims=True) + EPS) * g2_ref[...]
    n2_hi = n2.astype(BF16)
    n2_out[...] = n2_hi
    n2_lo = (n2 - n2_hi.astype(F32)).astype(BF16)
    l_hi = jnp.dot(n2_hi, wrt_ref[...], preferred_element_type=F32)
    l_lo = jnp.dot(n2_lo, wrt_ref[:, 0:LANES], preferred_element_type=F32)
    logits = l_hi[:, 0:LANES] + l_hi[:, LANES:2 * LANES] + l_lo + brt_ref[...]
    tm = logits.shape[0]
    lane = lax.broadcasted_iota(I32, (tm, LANES), 1)
    ninf = -jnp.inf
    is_g = lane < N_GROUPS
    gl = jnp.where(is_g, logits, ninf)
    gmax = jnp.max(gl, axis=-1, keepdims=True)
    gidx = jnp.min(jnp.where(gl == gmax, lane, LANES), axis=-1, keepdims=True)
    g_w = 1.0 / jnp.sum(jnp.where(is_g, jnp.exp(logits - gmax), 0.0), axis=-1, keepdims=True)
    lo = N_GROUPS + EXPERTS_PER_GROUP * gidx
    el = jnp.where((lane >= lo) & (lane < lo + EXPERTS_PER_GROUP), logits, ninf)
    m1 = jnp.max(el, axis=-1, keepdims=True)
    i1 = jnp.min(jnp.where(el == m1, lane, LANES), axis=-1, keepdims=True)
    el2 = jnp.where(lane == i1, ninf, el)
    m2 = jnp.max(el2, axis=-1, keepdims=True)
    i2 = jnp.min(jnp.where(el2 == m2, lane, LANES), axis=-1, keepdims=True)
    t = jnp.exp(m2 - m1)
    w1 = 1.0 / (1.0 + t)
    w2 = t / (1.0 + t)
    e1 = (i1 - N_GROUPS).astype(F32)
    e2 = (i2 - N_GROUPS).astype(F32)
    route = jnp.where(lane == 0, e1, jnp.where(lane == 1, e2, jnp.where(lane == 2, g_w * w1,
                      jnp.where(lane == 3, g_w * w2, 0.0))))
    route_out[...] = route


def _outproj(x2, a2, o3, hg2, hg_out_norm, w_out, norm_ffn, w_group, b_group, w_router, b_router, tm=512):
    T, D = x2.shape
    tm = min(tm, T)
    wa = w_out[:MLA_HEADS * MLA_V].astype(BF16)
    wr = w_out[MLA_HEADS * MLA_V:].astype(BF16)
    npad = LANES - N_GROUPS - N_EXPERTS
    wrt = jnp.concatenate([w_group, w_router, jnp.zeros((D, npad), F32)], axis=1)
    wrt_hi = wrt.astype(BF16)
    wrt = jnp.concatenate([wrt_hi, (wrt - wrt_hi.astype(F32)).astype(BF16)], axis=1)
    brt = jnp.concatenate([b_group, b_router, jnp.zeros((npad,), F32)]).reshape(1, LANES)
    row = lambda w: pl.BlockSpec((tm, w), lambda i: (i, 0))
    hw = HG_HEADS * HG_DV
    return pl.pallas_call(
        _outproj_kernel,
        grid=(T // tm,),
        in_specs=[row(D), row(a2.shape[1]), pl.BlockSpec((2, tm, hw), lambda i: (0, i, 0)), row(hw),
                  _full((1, HG_DV)), _full(wa.shape), _full(wr.shape), _full((1, D)), _full(wrt.shape),
                  _full((1, LANES))],
        out_specs=[row(D), row(D), row(LANES)],
        out_shape=[jax.ShapeDtypeStruct((T, D), F32), jax.ShapeDtypeStruct((T, D), BF16),
                   jax.ShapeDtypeStruct((T, LANES), F32)],
        compiler_params=_params(("parallel",)),
        name="outproj",
    )(x2, a2, o3, hg2, hg_out_norm.reshape(1, HG_DV), wa, wr, norm_ffn.reshape(1, D), wrt, brt)


def _plan_kernel(route_ref, tri_ref, upper_ref, lpos_out, lposT_out, runs_out, glob_out, tot_ref, base_ref):
    p = pl.program_id(0)
    i = pl.program_id(1)
    tm = route_ref.shape[0]
    lane = lax.broadcasted_iota(I32, (tm, LANES), 1)
    route = route_ref[...]
    is1 = lane == route[:, 0:1].astype(I32)
    is2 = lane == route[:, 1:2].astype(I32)
    onehot = jnp.where(is1 | is2, 1.0, 0.0)
    units = jnp.ceil(jnp.sum(onehot, axis=0, keepdims=True) * (1.0 / RUN_ROWS))
    sub = lax.broadcasted_iota(I32, (8, LANES), 0)
    rows3 = lambda a, b, c: jnp.where(sub == 0, a, jnp.where(sub == 1, b, jnp.where(sub == 2, c, 0.0)))

    @pl.when((p == 0) & (i == 0))
    def _():
        tot_ref[...] = jnp.zeros_like(tot_ref)

    @pl.when(p == 0)
    def _():
        tot_ref[...] += units

    @pl.when((p == 1) & (i == 0))
    def _():
        tot = tot_ref[...]
        block_units = MOE_BLOCK // RUN_ROWS
        padded = jnp.ceil(tot * (1.0 / block_units)) * block_units
        start = jnp.dot(jnp.broadcast_to(padded, (8, LANES)), upper_ref[...], preferred_element_type=F32,
                        precision=lax.Precision.HIGHEST)[0:1]
        base_ref[...] = start
        glob_out[...] = rows3(start + padded, start + tot, padded - tot).astype(I32)

    @pl.when(p == 1)
    def _():
        before = jnp.dot(tri_ref[...], onehot.astype(BF16), preferred_element_type=F32)
        local = jnp.dot(jnp.broadcast_to(units, (8, LANES)).astype(BF16), upper_ref[...].astype(BF16),
                        preferred_element_type=F32)[0:1] * RUN_ROWS
        pos = local + before
        p1 = jnp.sum(jnp.where(is1, pos, 0.0), axis=-1, keepdims=True)
        p2 = jnp.sum(jnp.where(is2, pos, 0.0), axis=-1, keepdims=True)
        slab = jnp.where(lane == 0, p1, jnp.where(lane == 1, p2, 0.0))
        lpos_out[...] = slab.astype(I32)
        lposT_out[...] = slab.T[0:8].astype(I32)
        runs_out[...] = rows3(local, units, base_ref[...] * RUN_ROWS).astype(I32)
        base_ref[...] += units


def _plan(route, tm):
    T = route.shape[0]
    nt = T // tm
    r = lax.broadcasted_iota(I32, (tm, tm), 0)
    c = lax.broadcasted_iota(I32, (tm, tm), 1)
    tri = (r > c).astype(BF16)
    ru = lax.broadcasted_iota(I32, (LANES, LANES), 0)
    cu = lax.broadcasted_iota(I32, (LANES, LANES), 1)
    upper = (ru < cu).astype(F32)
    return pl.pallas_call(
        _plan_kernel,
        grid=(2, nt),
        in_specs=[pl.BlockSpec((tm, LANES), lambda p, i: (i, 0)), _full((tm, tm)), _full((LANES, LANES))],
        out_specs=[pl.BlockSpec((tm, LANES), lambda p, i: (i * p, 0)),
                   pl.BlockSpec((8, tm), lambda p, i: (0, i * p)),
                   pl.BlockSpec((None, 8, LANES), lambda p, i: (i * p, 0, 0)),
                   _full((8, LANES))],
        out_shape=[jax.ShapeDtypeStruct((T, LANES), I32), jax.ShapeDtypeStruct((8, T), I32),
                   jax.ShapeDtypeStruct((nt, 8, LANES), I32), jax.ShapeDtypeStruct((8, LANES), I32)],
        scratch_shapes=[pltpu.VMEM((1, LANES), F32), pltpu.VMEM((1, LANES), F32)],
        compiler_params=_params(("arbitrary", "arbitrary")),
        name="plan",
    )(route, tri, upper)


def _run_pieces(tile, start_a_ref, units_ref, start_b_ref, fn):
    def per_expert(e, carry):
        j = tile * N_EXPERTS + e
        a0, b0, n = start_a_ref[j], start_b_ref[j], units_ref[j]
        n_long = lax.shift_right_logical(n, RUN_LONG_SHIFT)
        rest0 = n_long * (RUN_LONG_UNITS * RUN_ROWS)

        def long_piece(u, c):
            o = u * (RUN_LONG_UNITS * RUN_ROWS)
            fn(pl.multiple_of(a0 + o, RUN_ROWS), pl.multiple_of(b0 + o, RUN_ROWS), RUN_LONG_UNITS * RUN_ROWS, 0)
            return c

        def unit_piece(u, c):
            o = rest0 + u * RUN_ROWS
            fn(pl.multiple_of(a0 + o, RUN_ROWS), pl.multiple_of(b0 + o, RUN_ROWS), RUN_ROWS, 1)
            return c

        lax.fori_loop(0, n_long, long_piece, 0)
        lax.fori_loop(0, n & (RUN_LONG_UNITS - 1), unit_piece, 0)
        return carry

    lax.fori_loop(0, N_EXPERTS, per_expert, 0)


def _wait_run_pieces(tile, units_ref, wait_fn):
    def count(e, tot):
        n = units_ref[tile * N_EXPERTS + e]
        return tot[0] + lax.shift_right_logical(n, RUN_LONG_SHIFT), tot[1] + (n & (RUN_LONG_UNITS - 1))

    n_long, n_unit = lax.fori_loop(0, N_EXPERTS, count, (jnp.int32(0), jnp.int32(0)))

    def wait_long(u, c):
        wait_fn(RUN_LONG_UNITS * RUN_ROWS)
        return c

    def wait_unit(u, c):
        wait_fn(RUN_ROWS)
        return c

    lax.fori_loop(0, n_long, wait_long, 0)
    lax.fori_loop(0, n_unit, wait_unit, 0)


def _dispatch_kernel(ls_ref, un_ref, gd_ref, ts_ref, tu_ref, lposT_ref, n2_ref, xs_out, xl, sems):
    i = pl.program_id(0)
    nt = pl.num_programs(0)
    rows = xl.shape[1]
    tm = n2_ref.shape[0]
    slot = i % 2
    lp = lposT_ref[...]
    r = lax.broadcasted_iota(I32, (rows, tm), 0)
    pick = jnp.where((r == lp[0:1, :]) | (r == lp[1:2, :]), 1.0, 0.0).astype(BF16)
    xl[slot] = jnp.dot(pick, n2_ref[...], preferred_element_type=F32)

    def piece_copy(s, src, dst, n):
        return pltpu.make_async_copy(xl.at[s, pl.ds(src, n)], xs_out.at[pl.ds(dst, n)], sems.at[s])

    def wait_tile(tile, s):
        _wait_run_pieces(tile, un_ref, lambda n: piece_copy(s, 0, 0, n).wait())

    _run_pieces(i, ls_ref, un_ref, gd_ref,
                lambda src, dst, n, prio: piece_copy(slot, src, dst, n).start(priority=prio))

    @pl.when(i > 0)
    def _():
        wait_tile(i - 1, 1 - slot)

    @pl.when(i == nt - 1)
    def _():
        wait_tile(i, slot)
        xl[0] = jnp.zeros((rows, xl.shape[2]), xl.dtype)
        sem = sems.at[0]

        def zero_copy(dst, n):
            return pltpu.make_async_copy(xl.at[0, pl.ds(0, n)], xs_out.at[pl.ds(dst, n)], sem)

        def per_expert(e, total):
            d0, n = ts_ref[e], tu_ref[e]

            def per_unit(u, carry):
                zero_copy(pl.multiple_of(d0 + u * RUN_ROWS, RUN_ROWS), RUN_ROWS).start()
                return carry

            lax.fori_loop(0, n, per_unit, 0)
            return total + n

        def wait_unit_zero(u, carry):
            zero_copy(0, RUN_ROWS).wait()
            return carry

        lax.fori_loop(0, lax.fori_loop(0, N_EXPERTS, per_expert, 0), wait_unit_zero, 0)

        last = N_EXPERTS - 1
        first_unused = (ts_ref[last] + tu_ref[last] * RUN_ROWS) // MOE_BLOCK
        n_blocks = xs_out.shape[0] // MOE_BLOCK

        def start_block(b, carry):
            zero_copy(pl.multiple_of(b * MOE_BLOCK, MOE_BLOCK), MOE_BLOCK).start()
            return carry

        def wait_block(b, carry):
            zero_copy(0, MOE_BLOCK).wait()
            return carry

        lax.fori_loop(first_unused, n_blocks, start_block, 0)
        lax.fori_loop(first_unused, n_blocks, wait_block, 0)


def _dispatch(n2, lposT, tables, n_rows, tm):
    T, D = n2.shape
    local_rows = 2 * tm + N_EXPERTS * RUN_ROWS
    grid_spec = pltpu.PrefetchScalarGridSpec(
        num_scalar_prefetch=5,
        grid=(T // tm,),
        in_specs=[pl.BlockSpec((8, tm), lambda i, *_: (0, i)),
                  pl.BlockSpec((tm, D), lambda i, *_: (i, 0))],
        out_specs=pl.BlockSpec(memory_space=pl.ANY),
        scratch_shapes=[pltpu.VMEM((2, local_rows, D), F32), pltpu.SemaphoreType.DMA((2,))],
    )
    return pl.pallas_call(
        _dispatch_kernel,
        grid_spec=grid_spec,
        out_shape=jax.ShapeDtypeStruct((n_rows, D), F32),
        compiler_params=_params(("arbitrary",), has_side_effects=True),
        name="dispatch",
    )(*tables, lposT, n2)


def _expert_kernel(be_ref, nused_ref, x_ref, wgu_ref, wd_ref, y_ref):
    i = pl.program_id(0)
    de = wd_ref.shape[0]
    half = x_ref.shape[0] // 2

    @pl.when(i < nused_ref[0])
    def _():
        gu = [jnp.dot(x_ref[r * half:(r + 1) * half, :].astype(BF16), wgu_ref[...], preferred_element_type=F32)
              for r in range(2)]
        for r in range(2):
            g, u = gu[r][:, 0:de], gu[r][:, de:2 * de]
            hmid = (g * jax.nn.sigmoid(g) * u).astype(BF16)
            y_ref[r * half:(r + 1) * half, :] = jnp.dot(hmid, wd_ref[...], preferred_element_type=F32)

    @pl.when(i >= nused_ref[0])
    def _():
        y_ref[...] = jnp.zeros_like(y_ref)


def _experts(xs, block_e, nused, w_gate, w_up, w_down):
    P, D = xs.shape
    nb = P // MOE_BLOCK
    de = w_gate.shape[2]
    grid_spec = pltpu.PrefetchScalarGridSpec(
        num_scalar_prefetch=2,
        grid=(nb,),
        in_specs=[pl.BlockSpec((MOE_BLOCK, D), lambda i, be, nu: (jnp.maximum(jnp.minimum(i, nu[0] - 1), 0), 0)),
                  pl.BlockSpec((None, D, 2 * de), lambda i, be, nu: (be[i], 0, 0)),
                  pl.BlockSpec((None, de, D), lambda i, be, nu: (be[i], 0, 0))],
        out_specs=pl.BlockSpec((MOE_BLOCK, D), lambda i, be, nu: (i, 0)),
    )
    return pl.pallas_call(
        _expert_kernel,
        grid_spec=grid_spec,
        out_shape=jax.ShapeDtypeStruct((P, D), F32),
        compiler_params=_params(("arbitrary",)),
        name="experts",
    )(block_e, nused, xs, jnp.concatenate([w_gate, w_up], axis=2).astype(BF16), w_down.astype(BF16))


def _combine_kernel(ls_ref, un_ref, gd_ref, lpos_ref, route_ref, h1_ref, ys_ref, out_ref, yl, sems):
    i = pl.program_id(0)
    nt = pl.num_programs(0)
    tm = h1_ref.shape[0]
    rows = yl.shape[1]
    slot = i % 2

    def piece_copy(s, src, dst, n):
        return pltpu.make_async_copy(ys_ref.at[pl.ds(src, n)], yl.at[s, pl.ds(dst, n)], sems.at[s])

    def gather(tile, s):
        _run_pieces(tile, gd_ref, un_ref, ls_ref,
                    lambda src, dst, n, prio: piece_copy(s, src, dst, n).start(priority=prio))

    @pl.when(i == 0)
    def _():
        yl[...] = jnp.zeros_like(yl)
        gather(0, 0)

    @pl.when(i + 1 < nt)
    def _():
        gather(i + 1, 1 - slot)

    _wait_run_pieces(i, un_ref, lambda n: piece_copy(slot, 0, 0, n).wait())

    lp = lpos_ref[...]
    route = route_ref[...]
    r = lax.broadcasted_iota(I32, (tm, rows), 1)
    w = jnp.where(r == lp[:, 0:1], route[:, 2:3], 0.0) + jnp.where(r == lp[:, 1:2], route[:, 3:4], 0.0)
    out_ref[...] = h1_ref[...] + jnp.dot(w.astype(BF16), yl[slot].astype(BF16), preferred_element_type=F32)


def _combine(h1, route, lpos, tables, ys, tm):
    T, D = h1.shape
    local_rows = 2 * tm + N_EXPERTS * RUN_ROWS
    grid_spec = pltpu.PrefetchScalarGridSpec(
        num_scalar_prefetch=3,
        grid=(T // tm,),
        in_specs=[pl.BlockSpec((tm, LANES), lambda i, *_: (i, 0)),
                  pl.BlockSpec((tm, LANES), lambda i, *_: (i, 0)),
                  pl.BlockSpec((tm, D), lambda i, *_: (i, 0)),
                  pl.BlockSpec(memory_space=pl.ANY)],
        out_specs=pl.BlockSpec((tm, D), lambda i, *_: (i, 0)),
        scratch_shapes=[pltpu.VMEM((2, local_rows, D), F32), pltpu.SemaphoreType.DMA((2,))],
    )
    return pl.pallas_call(
        _combine_kernel,
        grid_spec=grid_spec,
        out_shape=jax.ShapeDtypeStruct((T, D), F32),
        compiler_params=_params(("arbitrary",)),
        name="combine",
    )(*tables, lpos, route, h1, ys)


def _mixers(x, norm_mix, w_in, q_lat_norm, kv_lat_norm, w_uq, w_ukv, q_norm, k_norm, lb_logits, layer):
    B, S, D = x.shape
    T = B * S
    q, k, v, hq, fz, hi, hg = _inproj(x.reshape(T, D), S, norm_mix, w_in, q_lat_norm, kv_lat_norm,
                                      w_uq, w_ukv, q_norm, k_norm)
    score_bound = (math.sqrt(MLA_QK) * math.log2(math.e) * 1.02) * jnp.max(jnp.abs(q_norm)) * jnp.max(jnp.abs(k_norm))
    a = _attention(q, k.reshape(B, S, -1), v, score_bound.astype(F32))
    lb = jnp.cumsum(jax.nn.softmax(lb_logits.astype(F32), axis=0), axis=0)[layer]
    hw = hq.shape[1]
    o = _hgrn(hq.reshape(B, S, hw), fz.reshape(2, B, S, hw), hi.reshape(B, S, hw), lb.reshape(2, 1, hw))
    return a.reshape(T, -1), o.reshape(2, T, hw), hg


def _moe(h1, n2, route, w_gate, w_up, w_down, tm=512):
    T, D = h1.shape
    tm = min(tm, T)
    nt = T // tm
    lpos, lposT, runs, glob = _plan(route, tm)
    per_run = lambda row: runs[:, row, :N_EXPERTS].reshape(-1)
    tables = (per_run(0), per_run(1), per_run(2))
    tails = (glob[1, :N_EXPERTS] * RUN_ROWS, glob[2, :N_EXPERTS])
    n_rows = -(-(2 * T + N_EXPERTS * RUN_ROWS * nt) // MOE_BLOCK) * MOE_BLOCK + N_EXPERTS * MOE_BLOCK
    nb = n_rows // MOE_BLOCK
    pend = glob[0, :N_EXPERTS] * RUN_ROWS
    block_row0 = jnp.arange(nb, dtype=I32) * MOE_BLOCK
    block_e = jnp.minimum(jnp.sum((pend[None, :] <= block_row0[:, None]).astype(I32), axis=1), N_EXPERTS - 1)
    nused = pend[N_EXPERTS - 1:] // MOE_BLOCK
    xs = _dispatch(n2, lposT, tables + tails, n_rows, tm)
    ys = _experts(xs, block_e, nused, w_gate, w_up, w_down)
    return _combine(h1, route, lpos, tables, ys, tm)


def kernel(x, norm_mix, w_in, q_lat_norm, kv_lat_norm, w_uq, w_ukv, q_norm, k_norm, lb_logits, hg_out_norm,
           w_out, norm_ffn, w_group, b_group, w_router, b_router, w_gate, w_up, w_down):
    B, S, D = x.shape
    h = x
    for l in range(norm_mix.shape[0]):
        h2 = h.reshape(B * S, D)
        a, o, hg = _mixers(h, norm_mix[l], w_in[l], q_lat_norm[l], kv_lat_norm[l], w_uq[l], w_ukv[l],
                           q_norm[l], k_norm[l], lb_logits, l)
        h1, n2, route = _outproj(h2, a, o, hg, hg_out_norm[l], w_out[l], norm_ffn[l], w_group[l], b_group[l],
                                 w_router[l], b_router[l])
        h = _moe(h1, n2, route, w_gate[l], w_up[l], w_down[l]).reshape(B, S, D)
    return h
```

```python
import functools
import math

import jax
import jax.numpy as jnp
from jax import lax
from jax.experimental import pallas as pl
from jax.experimental.pallas import tpu as pltpu

F32 = jnp.float32
BF16 = jnp.bfloat16
I32 = jnp.int32

EPS = 1e-6
LANES = 128
VMEM_LIMIT = 48 * 1024 * 1024

MLA_HEADS = 8
MLA_NOPE = 64
MLA_ROPE = 32
MLA_QK = MLA_NOPE + MLA_ROPE
MLA_V = 64
V_ROWS = MLA_V + 16
ROPE_THETA = 10000.0
ATTN_MAX_FIXED_SHIFT = 40.0
HG_HEADS = 4
HG_DK = 128
HG_DV = 128
HG_CHUNK = 64
HG_SAFE_EXPONENT = 60.0
N_GROUPS = 4
EXPERTS_PER_GROUP = 8
N_EXPERTS = N_GROUPS * EXPERTS_PER_GROUP
MOE_BLOCK = 512
RUN_ROWS = 16
RUN_LONG_SHIFT = 2
RUN_LONG_UNITS = 1 << RUN_LONG_SHIFT

NT_DIMS = (((1,), (1,)), ((), ()))
TN_DIMS = (((0,), (0,)), ((), ()))


def _params(sem, **kw):
    return pltpu.CompilerParams(dimension_semantics=sem, vmem_limit_bytes=VMEM_LIMIT, **kw)


def _full(shape):
    n = len(shape)
    return pl.BlockSpec(shape, lambda *_: (0,) * n)


def _inproj_kernel(x_ref, g_ref, wlatT_ref, wkr_ref, wh_ref, qlgc_ref, kvlgc_ref, kvlg_ref, wuqT_ref, wuk_ref,
                   wuvT_ref, qngc_ref, kng_ref, vonec_ref, cosT_ref, sinT_ref, cos_ref, sa_ref, sb_ref,
                   qT_out, k_out, vT_out, hq_out, fz_out, hi_out, hg_out, *, q_scale):
    x = x_ref[...]
    ms = jnp.mean(x * x, axis=-1, keepdims=True)
    n = (x * lax.rsqrt(ms + EPS) * g_ref[...]).astype(BF16)

    hw = hq_out.shape[1]
    ql = qlgc_ref.shape[0]
    kvl = kvlgc_ref.shape[0]
    half = MLA_ROPE // 2

    def wide(j):
        return jnp.dot(n, wh_ref[:, j * hw:(j + 1) * hw], preferred_element_type=F32).astype(BF16)

    latT = lax.dot_general(wlatT_ref[...], n, NT_DIMS, preferred_element_type=F32)
    latk = jnp.dot(n, wkr_ref[...], preferred_element_type=F32)
    hq_out[...] = wide(0)
    fz_out[0] = wide(1)
    qlT = latT[0:ql]
    kvlT = latT[ql:ql + kvl]
    qnT = (qlT * lax.rsqrt(jnp.mean(qlT * qlT, axis=0, keepdims=True) + EPS) * qlgc_ref[...]).astype(BF16)
    kvnT = (kvlT * lax.rsqrt(jnp.mean(kvlT * kvlT, axis=0, keepdims=True) + EPS) * kvlgc_ref[...]).astype(BF16)
    q_allT = jnp.dot(wuqT_ref[...], qnT, preferred_element_type=F32)
    vT_out[...] = (jnp.dot(wuvT_ref[...], kvnT, preferred_element_type=F32) + vonec_ref[...]).astype(BF16)
    kvl_r = latk[:, 0:kvl]
    kr = latk[:, kvl:kvl + LANES]
    kvn = (kvl_r * lax.rsqrt(jnp.mean(kvl_r * kvl_r, axis=-1, keepdims=True) + EPS) * kvlg_ref[...]).astype(BF16)
    k_all = jnp.dot(kvn, wuk_ref[...], preferred_element_type=F32)
    fz_out[1] = wide(2)
    hi_out[...] = wide(3)
    hg_out[...] = wide(4)

    cosT = cosT_ref[...]
    sinT = sinT_ref[...]
    qngc = qngc_ref[...]
    for h in range(MLA_HEADS):
        t = q_allT[h * LANES:(h + 1) * LANES]
        tn = t * lax.rsqrt(jnp.sum(t * t, axis=0, keepdims=True) * (1.0 / MLA_QK) + EPS) * qngc
        x1 = tn[MLA_NOPE:MLA_NOPE + half]
        x2 = tn[MLA_NOPE + half:MLA_QK]
        rot = jnp.concatenate([tn[0:MLA_NOPE], x1 * cosT - x2 * sinT, x2 * cosT + x1 * sinT, tn[MLA_QK:LANES]],
                              axis=0)
        qT_out[h * LANES:(h + 1) * LANES, :] = (rot * q_scale).astype(BF16)

    kng = kng_ref[...]
    krg = kr * kng
    up = pltpu.roll(krg, LANES - half, axis=1)
    dn = pltpu.roll(krg, half, axis=1)
    kr_rot = krg * cos_ref[...] + up * sa_ref[...] + dn * sb_ref[...]
    kr_ssq = jnp.sum(kr * kr, axis=-1, keepdims=True)
    for h in range(MLA_HEADS):
        sl = slice(h * LANES, (h + 1) * LANES)
        t = k_all[:, sl]
        r = lax.rsqrt((jnp.sum(t * t, axis=-1, keepdims=True) + kr_ssq) * (1.0 / MLA_QK) + EPS)
        k_out[:, sl] = (r * (t * kng + kr_rot)).astype(BF16)


def _inproj(x2, seq, norm_mix, w_in, q_lat_norm, kv_lat_norm, w_uq, w_ukv, q_norm, k_norm, tm=512):
    T, D = x2.shape
    tm = min(tm, seq)
    H = MLA_HEADS
    ql, kvl = w_uq.shape[0], w_ukv.shape[0]
    hw = HG_HEADS * HG_DK
    o_kr = ql + kvl
    o_h = o_kr + MLA_ROPE
    zeros = lambda c: jnp.zeros((D, c), F32)
    w_latT = w_in[:, :o_kr].T.astype(BF16)
    w_kr = jnp.concatenate([w_in[:, ql:o_kr], zeros(MLA_NOPE), w_in[:, o_kr:o_h],
                            zeros(LANES - MLA_QK)], axis=1).astype(BF16)
    w_h = w_in[:, o_h:].astype(BF16)
    pad = LANES - MLA_QK
    wuqT = jnp.pad(w_uq.reshape(ql, H, MLA_QK), ((0, 0), (0, 0), (0, pad))).reshape(ql, H * LANES).T.astype(BF16)
    wkv = w_ukv.reshape(kvl, H, MLA_NOPE + MLA_V)
    wuk = jnp.pad(wkv[:, :, :MLA_NOPE], ((0, 0), (0, 0), (0, LANES - MLA_NOPE))).reshape(kvl, H * LANES).astype(BF16)
    wuvT = jnp.pad(wkv[:, :, MLA_NOPE:], ((0, 0), (0, 0), (0, V_ROWS - MLA_V))).reshape(kvl, H * V_ROWS).T.astype(BF16)
    vonec = jnp.tile(jnp.concatenate([jnp.zeros((MLA_V,), F32), jnp.ones((V_ROWS - MLA_V,), F32)]), H)
    vonec = vonec.reshape(H * V_ROWS, 1)
    qngc = jnp.pad(q_norm, (0, pad)).reshape(LANES, 1)
    kng = jnp.pad(k_norm, (0, pad)).reshape(1, LANES)

    half = MLA_ROPE // 2
    inv = 1.0 / (ROPE_THETA ** (jnp.arange(half, dtype=F32) / half))
    ang = jnp.arange(seq, dtype=F32)[:, None] * inv[None, :]
    cos, sin = jnp.cos(ang), jnp.sin(ang)
    z = lambda c: jnp.zeros((seq, c), F32)
    cos_t = jnp.concatenate([jnp.ones((seq, MLA_NOPE), F32), cos, cos, z(pad)], axis=1)
    sa_t = jnp.concatenate([z(MLA_NOPE), -sin, z(half), z(pad)], axis=1)
    sb_t = jnp.concatenate([z(MLA_NOPE), z(half), sin, z(pad)], axis=1)

    nseq = seq // tm
    row = lambda w: pl.BlockSpec((tm, w), lambda i: (i, 0))
    col = lambda r: pl.BlockSpec((r, tm), lambda i: (0, i))
    tab = pl.BlockSpec((tm, LANES), lambda i: (i % nseq, 0))
    tabT = pl.BlockSpec((half, tm), lambda i: (0, i % nseq))
    q_scale = (MLA_QK ** -0.5) * math.log2(math.e)
    outs = pl.pallas_call(
        functools.partial(_inproj_kernel, q_scale=q_scale),
        grid=(T // tm,),
        in_specs=[row(D), _full((1, D)), _full(w_latT.shape), _full(w_kr.shape), _full(w_h.shape),
                  _full((ql, 1)), _full((kvl, 1)), _full((1, kvl)), _full(wuqT.shape), _full(wuk.shape),
                  _full(wuvT.shape), _full((LANES, 1)), _full((1, LANES)), _full((H * V_ROWS, 1)),
                  tabT, tabT, tab, tab, tab],
        out_specs=[col(H * LANES), row(H * LANES), col(H * V_ROWS), row(hw),
                   pl.BlockSpec((2, tm, hw), lambda i: (0, i, 0)), row(hw), row(hw)],
        out_shape=[jax.ShapeDtypeStruct((H * LANES, T), BF16), jax.ShapeDtypeStruct((T, H * LANES), BF16),
                   jax.ShapeDtypeStruct((H * V_ROWS, T), BF16), jax.ShapeDtypeStruct((T, hw), BF16),
                   jax.ShapeDtypeStruct((2, T, hw), BF16), jax.ShapeDtypeStruct((T, hw), BF16),
                   jax.ShapeDtypeStruct((T, hw), BF16)],
        compiler_params=_params(("parallel",)),
        name="inproj",
    )(x2, norm_mix.reshape(1, D), w_latT, w_kr, w_h, q_lat_norm.reshape(ql, 1), kv_lat_norm.reshape(kvl, 1),
      kv_lat_norm.reshape(1, kvl), wuqT, wuk, wuvT, qngc, kng, vonec, cos.T, sin.T, cos_t, sa_t, sb_t)
    return outs


def _attn_kernel(qT_ref, k_ref, vT_ref, o_ref, s00, s01, s10, s11, *, tk):
    tq = qT_ref.shape[1]
    nk = k_ref.shape[0] // tk
    qTs = [qT_ref[j * LANES:(j + 1) * LANES, :] for j in range(2)]
    s_bufs = ((s00, s01), (s10, s11))

    def scores(j, slot, c):
        r0 = pl.multiple_of(c * tk, tk)
        sT = jnp.dot(k_ref[pl.ds(r0, tk), j * LANES:(j + 1) * LANES], qTs[j], preferred_element_type=F32)
        s_bufs[j][slot][...] = sT
        return jnp.max(sT, axis=0, keepdims=True)

    def absorb(j, slot, c, m, acc, mx):
        r0 = pl.multiple_of(c * tk, tk)
        m_new = jnp.maximum(m, mx)
        pT = jnp.exp2(s_bufs[j][slot][...] - m_new).astype(BF16)
        pv = jnp.dot(vT_ref[j * V_ROWS:(j + 1) * V_ROWS, pl.ds(r0, tk)], pT, preferred_element_type=F32)
        return m_new, jnp.exp2(m - m_new) * acc + pv

    def step(c, slot, state, prefetch):
        mx_next = [scores(j, 1 - slot, c + 1) if prefetch else state[j][2] for j in range(2)]
        new = []
        for j in range(2):
            m, acc, mx = state[j]
            m, acc = absorb(j, slot, c, m, acc, mx)
            new.append((m, acc, mx_next[j]))
        return tuple(new)

    def pair(i, state):
        c = 2 * i
        return step(c + 1, 1, step(c, 0, state, True), True)

    state = tuple((jnp.full((1, tq), -jnp.inf, F32), jnp.zeros((V_ROWS, tq), F32), scores(j, 0, 0))
                  for j in range(2))
    state = lax.fori_loop(0, nk // 2 - 1, pair, state)
    state = step(nk - 1, 1, step(nk - 2, 0, state, True), False)
    acc0, acc1 = state[0][1], state[1][1]
    oT = jnp.concatenate([acc0[0:MLA_V] / acc0[MLA_V:MLA_V + 1], acc1[0:MLA_V] / acc1[MLA_V:MLA_V + 1]], axis=0)
    o_ref[...] = oT.T.astype(o_ref.dtype)


def _attn_bounded_kernel(shift_ref, qT_ref, k_ref, vT_ref, o_ref, p00, p01, p10, p11, *, tk):
    tq = qT_ref.shape[1]
    nk = k_ref.shape[0] // tk
    shift = shift_ref[0]
    qTs = [qT_ref[j * LANES:(j + 1) * LANES, :] for j in range(2)]
    p_bufs = ((p00, p01), (p10, p11))

    def probs(j, c):
        sT = jnp.dot(k_ref[c * tk:(c + 1) * tk, j * LANES:(j + 1) * LANES], qTs[j], preferred_element_type=F32)
        p_bufs[j][c % 2][...] = jnp.exp2(sT - shift).astype(BF16)

    def absorb(j, c, acc):
        return acc + jnp.dot(vT_ref[j * V_ROWS:(j + 1) * V_ROWS, c * tk:(c + 1) * tk], p_bufs[j][c % 2][...],
                             preferred_element_type=F32)

    for j in range(2):
        probs(j, 0)
    accs = [jnp.zeros((V_ROWS, tq), F32) for _ in range(2)]
    for c in range(nk):
        if c + 1 < nk:
            for j in range(2):
                probs(j, c + 1)
        accs = [absorb(j, c, accs[j]) for j in range(2)]
    acc0, acc1 = accs
    oT = jnp.concatenate([acc0[0:MLA_V] / acc0[MLA_V:MLA_V + 1], acc1[0:MLA_V] / acc1[MLA_V:MLA_V + 1]], axis=0)
    o_ref[...] = oT.T.astype(o_ref.dtype)


def _attention(qT, k3, vT, score_bound, tq=512, tk=512):
    B, S, _ = k3.shape
    tq, tk = min(tq, S), min(tk, S)
    nq = S // tq
    hp = MLA_HEADS // 2
    specs = dict(
        grid=(B, hp, nq),
        in_specs=[pl.BlockSpec((2 * LANES, tq), lambda b, h, i: (h, b * nq + i)),
                  pl.BlockSpec((None, S, 2 * LANES), lambda b, h, i: (b, 0, h)),
                  pl.BlockSpec((2 * V_ROWS, S), lambda b, h, i: (h, b))],
        out_specs=pl.BlockSpec((None, tq, 2 * MLA_V), lambda b, h, i: (b, i, h)),
        out_shape=jax.ShapeDtypeStruct((B, S, MLA_HEADS * MLA_V), BF16),
        compiler_params=_params(("parallel", "parallel", "arbitrary")),
    )

    def bounded():
        tkb = tk
        in_specs = [pl.BlockSpec(memory_space=pltpu.SMEM)] + specs["in_specs"]
        return pl.pallas_call(functools.partial(_attn_bounded_kernel, tk=tkb), name="attention_bounded",
                              scratch_shapes=[pltpu.VMEM((tkb, tq), BF16)] * 4,
                              **{**specs, "in_specs": in_specs})(score_bound.reshape(1), qT, k3, vT)

    def online():
        return pl.pallas_call(functools.partial(_attn_kernel, tk=tk), name="attention",
                              scratch_shapes=[pltpu.VMEM((tk, tq), F32)] * 4, **specs)(qT, k3, vT)

    return lax.cond(score_bound <= ATTN_MAX_FIXED_SHIFT, bounded, online)


def _hgrn_kernel(hq_ref, z_ref, hi_ref, lb_ref, tri_ref, o_ref, st_ref, kk_scr, b_scr, edge_scr, qh_scr, a_scr,
                 u_scr, qf_scr, of_scr):
    d = pl.program_id(1)
    C = HG_CHUNK
    tl = hq_ref.shape[0]
    nc = tl // C
    heads = [slice(h * HG_DK, (h + 1) * HG_DK) for h in range(HG_HEADS)]

    @pl.when(pl.program_id(2) == 0)
    def _():
        st_ref[...] = jnp.zeros_like(st_ref)

    one_m_lb = 1.0 - lb_ref[...]
    tri = tri_ref[...]
    keep = tri > 0
    fwd = d == 0

    worst = jnp.zeros_like(one_m_lb)
    for c in range(nc):
        rows = slice(c * C, (c + 1) * C)
        kk = one_m_lb * jax.nn.sigmoid(-z_ref[rows, :].astype(F32))
        g = jnp.log(1.0 - kk)
        g_hi = g.astype(BF16)
        g_lo = (g - g_hi.astype(F32)).astype(BF16)
        b = jnp.dot(tri, g_hi, preferred_element_type=F32) + jnp.dot(tri, g_lo, preferred_element_type=F32)
        b_edge = jnp.where(fwd, b[C - 1:C, :], b[0:1, :])
        kk_scr[rows, :] = kk
        b_scr[rows, :] = b
        edge_scr[c] = b_edge
        worst = jnp.maximum(worst, -b_edge)
    safe = jnp.max(worst) < HG_SAFE_EXPONENT

    @pl.when(safe)
    def _():
        for c in range(nc):
            rows = slice(c * C, (c + 1) * C)
            kk = kk_scr[rows, :]
            b = b_scr[rows, :]
            hq = hq_ref[rows, :].astype(F32)
            qh = (hq * jax.nn.sigmoid(hq) * jnp.exp(b)).astype(BF16)
            kt = (kk * jnp.exp(-b)).astype(BF16)
            ks = (kk * jnp.exp(edge_scr[c] - b)).astype(BF16)
            v = hi_ref[rows, :]
            qh_scr[rows, :] = qh
            for h, sl in enumerate(heads):
                a = lax.dot_general(qh[:, sl], kt[:, sl], NT_DIMS, preferred_element_type=F32)
                a_scr[c, h] = jnp.where(keep, a, 0.0).astype(BF16)
                u_scr[c, h] = lax.dot_general(v[:, sl], ks[:, sl], TN_DIMS, preferred_element_type=F32)
        st = [st_ref[h] for h in range(HG_HEADS)]
        for p in range(nc):
            c = jnp.where(fwd, p, nc - 1 - p)
            r0 = pl.multiple_of(c * C, C)
            dec = jnp.exp(edge_scr[c])
            qh = qh_scr[pl.ds(r0, C), :]
            v = hi_ref[pl.ds(r0, C), :]
            for h, sl in enumerate(heads):
                o = jnp.dot(a_scr[c, h], v[:, sl], preferred_element_type=F32)
                o = o + lax.dot_general(qh[:, sl], st[h].astype(BF16), NT_DIMS, preferred_element_type=F32)
                o_ref[pl.ds(r0, C), sl] = o.astype(o_ref.dtype)
                st[h] = st[h] * dec[:, sl] + u_scr[c, h]
        for h in range(HG_HEADS):
            st_ref[h] = st[h]

    @pl.when(jnp.logical_not(safe))
    def _():
        hq = hq_ref[...].astype(F32)
        qf_scr[...] = hq * jax.nn.sigmoid(hq)
        first = lax.broadcasted_iota(I32, (16, HG_DK), 0) == 0

        def row(i, carry):
            t = jnp.where(fwd, i, tl - 1 - i)
            kk = kk_scr[pl.ds(t, 1), :]
            f = 1.0 - kk
            q = qf_scr[pl.ds(t, 1), :]
            g0 = pl.multiple_of((t // 16) * 16, 16)
            v = hi_ref[pl.ds(g0, 16), :].astype(F32)
            v = jnp.sum(jnp.where(lax.broadcasted_iota(I32, v.shape, 0) == t % 16, v, 0.0), axis=0, keepdims=True)
            outs = []
            for h, sl in enumerate(heads):
                pad = lambda x: jnp.where(first, jnp.broadcast_to(x[:, sl], (16, HG_DK)), 0.0).astype(BF16)
                st = st_ref[h] * f[:, sl] + lax.dot_general(pad(v), pad(kk), TN_DIMS, preferred_element_type=F32)
                st_ref[h] = st
                outs.append(lax.dot_general(pad(q), st.astype(BF16), NT_DIMS, preferred_element_type=F32)[0:1])
            of_scr[pl.ds(t, 1), :] = jnp.concatenate(outs, axis=1)
            return carry

        lax.fori_loop(0, tl, row, 0)
        o_ref[...] = of_scr[...].astype(o_ref.dtype)


def _hgrn(hq3, fz4, hi3, lb, tl=512):
    B, S, W = hq3.shape
    tl = min(tl, S)
    nt = S // tl
    C = HG_CHUNK
    r = lax.broadcasted_iota(I32, (C, C), 0)
    c = lax.broadcasted_iota(I32, (C, C), 1)
    tri = jnp.stack([r >= c, r <= c]).astype(BF16)
    tile = lambda b, d, i: (b, i + d * (nt - 1 - 2 * i), 0)
    return pl.pallas_call(
        _hgrn_kernel,
        grid=(B, 2, nt),
        in_specs=[pl.BlockSpec((None, tl, W), tile),
                  pl.BlockSpec((None, None, tl, W), lambda b, d, i: (d, b, i + d * (nt - 1 - 2 * i), 0)),
                  pl.BlockSpec((None, tl, W), tile),
                  pl.BlockSpec((None, 1, W), lambda b, d, i: (d, 0, 0)),
                  pl.BlockSpec((None, C, C), lambda b, d, i: (d, 0, 0))],
        out_specs=pl.BlockSpec((None, None, tl, W), lambda b, d, i: (d, b, i + d * (nt - 1 - 2 * i), 0)),
        out_shape=jax.ShapeDtypeStruct((2, B, S, W), BF16),
        scratch_shapes=[pltpu.VMEM((HG_HEADS, HG_DV, HG_DK), F32),
                        pltpu.VMEM((tl, W), F32), pltpu.VMEM((tl, W), F32),
                        pltpu.VMEM((tl // C, 1, W), F32),
                        pltpu.VMEM((tl, W), BF16),
                        pltpu.VMEM((tl // C, HG_HEADS, C, C), BF16),
                        pltpu.VMEM((tl // C, HG_HEADS, HG_DV, HG_DK), F32),
                        pltpu.VMEM((tl, W), F32), pltpu.VMEM((tl, W), F32)],
        compiler_params=_params(("parallel", "parallel", "arbitrary")),
        name="hgrn",
    )(hq3, fz4, hi3, lb, tri)


def _outproj_kernel(x_ref, a_ref, o_ref, hg_ref, ong_ref, wa_ref, wr_ref, g2_ref, wrt_ref, brt_ref,
                    h1_out, n2_out, route_out):
    o = o_ref[0].astype(F32) + o_ref[1].astype(F32)
    hg = hg_ref[...].astype(F32)
    gate = hg * jax.nn.sigmoid(hg)
    ong = ong_ref[...]
    parts = []
    for h in range(HG_HEADS):
        sl = slice(h * HG_DV, (h + 1) * HG_DV)
        oh = o[:, sl]
        parts.append((oh * lax.rsqrt(jnp.mean(oh * oh, axis=-1, keepdims=True) + EPS) * ong * gate[:, sl]).astype(BF16))
    r = jnp.concatenate(parts, axis=1)
    h1 = x_ref[...] + jnp.dot(a_ref[...], wa_ref[...], preferred_element_type=F32)
    h1 = h1 + jnp.dot(r, wr_ref[...], preferred_element_type=F32)
    h1_out[...] = h1
    n2 = h1 * lax.rsqrt(jnp.mean(h1 * h1, axis=-1, keepdims=True) + EPS) * g2_ref[...]
    n2_hi = n2.astype(BF16)
    n2_out[...] = n2_hi
    n2_lo = (n2 - n2_hi.astype(F32)).astype(BF16)
    l_hi = jnp.dot(n2_hi, wrt_ref[...], preferred_element_type=F32)
    l_lo = jnp.dot(n2_lo, wrt_ref[:, 0:LANES], preferred_element_type=F32)
    logits = l_hi[:, 0:LANES] + l_hi[:, LANES:2 * LANES] + l_lo + brt_ref[...]
    tm = logits.shape[0]
    lane = lax.broadcasted_iota(I32, (tm, LANES), 1)
    ninf = -jnp.inf
    is_g = lane < N_GROUPS
    gl = jnp.where(is_g, logits, ninf)
    gmax = jnp.max(gl, axis=-1, keepdims=True)
    gidx = jnp.min(jnp.where(gl == gmax, lane, LANES), axis=-1, keepdims=True)
    g_w = 1.0 / jnp.sum(jnp.where(is_g, jnp.exp(logits - gmax), 0.0), axis=-1, keepdims=True)
    lo = N_GROUPS + EXPERTS_PER_GROUP * gidx
    el = jnp.where((lane >= lo) & (lane < lo + EXPERTS_PER_GROUP), logits, ninf)
    m1 = jnp.max(el, axis=-1, keepdims=True)
    i1 = jnp.min(jnp.where(el == m1, lane, LANES), axis=-1, keepdims=True)
    el2 = jnp.where(lane == i1, ninf, el)
    m2 = jnp.max(el2, axis=-1, keepdims=True)
    i2 = jnp.min(jnp.where(el2 == m2, lane, LANES), axis=-1, keepdims=True)
    t = jnp.exp(m2 - m1)
    w1 = 1.0 / (1.0 + t)
    w2 = t / (1.0 + t)
    e1 = (i1 - N_GROUPS).astype(F32)
    e2 = (i2 - N_GROUPS).astype(F32)
    route = jnp.where(lane == 0, e1, jnp.where(lane == 1, e2, jnp.where(lane == 2, g_w * w1,
                      jnp.where(lane == 3, g_w * w2, 0.0))))
    route_out[...] = route


def _outproj(x2, a2, o3, hg2, hg_out_norm, w_out, norm_ffn, w_group, b_group, w_router, b_router, tm=512):
    T, D = x2.shape
    tm = min(tm, T)
    wa = w_out[:MLA_HEADS * MLA_V].astype(BF16)
    wr = w_out[MLA_HEADS * MLA_V:].astype(BF16)
    npad = LANES - N_GROUPS - N_EXPERTS
    wrt = jnp.concatenate([w_group, w_router, jnp.zeros((D, npad), F32)], axis=1)
    wrt_hi = wrt.astype(BF16)
    wrt = jnp.concatenate([wrt_hi, (wrt - wrt_hi.astype(F32)).astype(BF16)], axis=1)
    brt = jnp.concatenate([b_group, b_router, jnp.zeros((npad,), F32)]).reshape(1, LANES)
    row = lambda w: pl.BlockSpec((tm, w), lambda i: (i, 0))
    hw = HG_HEADS * HG_DV
    return pl.pallas_call(
        _outproj_kernel,
        grid=(T // tm,),
        in_specs=[row(D), row(a2.shape[1]), pl.BlockSpec((2, tm, hw), lambda i: (0, i, 0)), row(hw),
                  _full((1, HG_DV)), _full(wa.shape), _full(wr.shape), _full((1, D)), _full(wrt.shape),
                  _full((1, LANES))],
        out_specs=[row(D), row(D), row(LANES)],
        out_shape=[jax.ShapeDtypeStruct((T, D), F32), jax.ShapeDtypeStruct((T, D), BF16),
                   jax.ShapeDtypeStruct((T, LANES), F32)],
        compiler_params=_params(("parallel",)),
        name="outproj",
    )(x2, a2, o3, hg2, hg_out_norm.reshape(1, HG_DV), wa, wr, norm_ffn.reshape(1, D), wrt, brt)


def _plan_kernel(route_ref, tri_ref, upper_ref, lpos_out, lposT_out, runs_out, glob_out, tot_ref, base_ref):
    p = pl.program_id(0)
    i = pl.program_id(1)
    tm = route_ref.shape[0]
    lane = lax.broadcasted_iota(I32, (tm, LANES), 1)
    route = route_ref[...]
    is1 = lane == route[:, 0:1].astype(I32)
    is2 = lane == route[:, 1:2].astype(I32)
    onehot = jnp.where(is1 | is2, 1.0, 0.0)
    units = jnp.ceil(jnp.sum(onehot, axis=0, keepdims=True) * (1.0 / RUN_ROWS))
    sub = lax.broadcasted_iota(I32, (8, LANES), 0)
    rows3 = lambda a, b, c: jnp.where(sub == 0, a, jnp.where(sub == 1, b, jnp.where(sub == 2, c, 0.0)))

    @pl.when((p == 0) & (i == 0))
    def _():
        tot_ref[...] = jnp.zeros_like(tot_ref)

    @pl.when(p == 0)
    def _():
        tot_ref[...] += units

    @pl.when((p == 1) & (i == 0))
    def _():
        tot = tot_ref[...]
        block_units = MOE_BLOCK // RUN_ROWS
        padded = jnp.ceil(tot * (1.0 / block_units)) * block_units
        start = jnp.dot(jnp.broadcast_to(padded, (8, LANES)), upper_ref[...], preferred_element_type=F32,
                        precision=lax.Precision.HIGHEST)[0:1]
        base_ref[...] = start
        glob_out[...] = rows3(start + padded, start + tot, padded - tot).astype(I32)

    @pl.when(p == 1)
    def _():
        before = jnp.dot(tri_ref[...], onehot.astype(BF16), preferred_element_type=F32)
        local = jnp.dot(jnp.broadcast_to(units, (8, LANES)).astype(BF16), upper_ref[...].astype(BF16),
                        preferred_element_type=F32)[0:1] * RUN_ROWS
        pos = local + before
        p1 = jnp.sum(jnp.where(is1, pos, 0.0), axis=-1, keepdims=True)
        p2 = jnp.sum(jnp.where(is2, pos, 0.0), axis=-1, keepdims=True)
        slab = jnp.where(lane == 0, p1, jnp.where(lane == 1, p2, 0.0))
        lpos_out[...] = slab.astype(I32)
        lposT_out[...] = slab.T[0:8].astype(I32)
        runs_out[...] = rows3(local, units, base_ref[...] * RUN_ROWS).astype(I32)
        base_ref[...] += units


def _plan(route, tm):
    T = route.shape[0]
    nt = T // tm
    r = lax.broadcasted_iota(I32, (tm, tm), 0)
    c = lax.broadcasted_iota(I32, (tm, tm), 1)
    tri = (r > c).astype(BF16)
    ru = lax.broadcasted_iota(I32, (LANES, LANES), 0)
    cu = lax.broadcasted_iota(I32, (LANES, LANES), 1)
    upper = (ru < cu).astype(F32)
    return pl.pallas_call(
        _plan_kernel,
        grid=(2, nt),
        in_specs=[pl.BlockSpec((tm, LANES), lambda p, i: (i, 0)), _full((tm, tm)), _full((LANES, LANES))],
        out_specs=[pl.BlockSpec((tm, LANES), lambda p, i: (i * p, 0)),
                   pl.BlockSpec((8, tm), lambda p, i: (0, i * p)),
                   pl.BlockSpec((None, 8, LANES), lambda p, i: (i * p, 0, 0)),
                   _full((8, LANES))],
        out_shape=[jax.ShapeDtypeStruct((T, LANES), I32), jax.ShapeDtypeStruct((8, T), I32),
                   jax.ShapeDtypeStruct((nt, 8, LANES), I32), jax.ShapeDtypeStruct((8, LANES), I32)],
        scratch_shapes=[pltpu.VMEM((1, LANES), F32), pltpu.VMEM((1, LANES), F32)],
        compiler_params=_params(("arbitrary", "arbitrary")),
        name="plan",
    )(route, tri, upper)


def _run_pieces(tile, start_a_ref, units_ref, start_b_ref, fn):
    def per_expert(e, carry):
        j = tile * N_EXPERTS + e
        a0, b0, n = start_a_ref[j], start_b_ref[j], units_ref[j]
        n_long = lax.shift_right_logical(n, RUN_LONG_SHIFT)
        rest0 = n_long * (RUN_LONG_UNITS * RUN_ROWS)

        def long_piece(u, c):
            o = u * (RUN_LONG_UNITS * RUN_ROWS)
            fn(pl.multiple_of(a0 + o, RUN_ROWS), pl.multiple_of(b0 + o, RUN_ROWS), RUN_LONG_UNITS * RUN_ROWS, 0)
            return c

        def unit_piece(u, c):
            o = rest0 + u * RUN_ROWS
            fn(pl.multiple_of(a0 + o, RUN_ROWS), pl.multiple_of(b0 + o, RUN_ROWS), RUN_ROWS, 1)
            return c

        lax.fori_loop(0, n_long, long_piece, 0)
        lax.fori_loop(0, n & (RUN_LONG_UNITS - 1), unit_piece, 0)
        return carry

    lax.fori_loop(0, N_EXPERTS, per_expert, 0)


def _wait_run_pieces(tile, units_ref, wait_fn):
    def count(e, tot):
        n = units_ref[tile * N_EXPERTS + e]
        return tot[0] + lax.shift_right_logical(n, RUN_LONG_SHIFT), tot[1] + (n & (RUN_LONG_UNITS - 1))

    n_long, n_unit = lax.fori_loop(0, N_EXPERTS, count, (jnp.int32(0), jnp.int32(0)))

    def wait_long(u, c):
        wait_fn(RUN_LONG_UNITS * RUN_ROWS)
        return c

    def wait_unit(u, c):
        wait_fn(RUN_ROWS)
        return c

    lax.fori_loop(0, n_long, wait_long, 0)
    lax.fori_loop(0, n_unit, wait_unit, 0)


def _dispatch_kernel(ls_ref, un_ref, gd_ref, ts_ref, tu_ref, lposT_ref, n2_ref, xs_out, xl, sems):
    i = pl.program_id(0)
    nt = pl.num_programs(0)
    rows = xl.shape[1]
    tm = n2_ref.shape[0]
    slot = i % 2
    lp = lposT_ref[...]
    r = lax.broadcasted_iota(I32, (rows, tm), 0)
    pick = jnp.where((r == lp[0:1, :]) | (r == lp[1:2, :]), 1.0, 0.0).astype(BF16)
    xl[slot] = jnp.dot(pick, n2_ref[...], preferred_element_type=F32).astype(xl.dtype)

    def piece_copy(s, src, dst, n):
        return pltpu.make_async_copy(xl.at[s, pl.ds(src, n)], xs_out.at[pl.ds(dst, n)], sems.at[s])

    def wait_tile(tile, s):
        _wait_run_pieces(tile, un_ref, lambda n: piece_copy(s, 0, 0, n).wait())

    _run_pieces(i, ls_ref, un_ref, gd_ref,
                lambda src, dst, n, prio: piece_copy(slot, src, dst, n).start(priority=prio))

    @pl.when(i > 0)
    def _():
        wait_tile(i - 1, 1 - slot)

    @pl.when(i == nt - 1)
    def _():
        wait_tile(i, slot)
        xl[0] = jnp.zeros((rows, xl.shape[2]), xl.dtype)
        sem = sems.at[0]

        def zero_copy(dst, n):
            return pltpu.make_async_copy(xl.at[0, pl.ds(0, n)], xs_out.at[pl.ds(dst, n)], sem)

        def per_expert(e, total):
            d0, n = ts_ref[e], tu_ref[e]

            def per_unit(u, carry):
                zero_copy(pl.multiple_of(d0 + u * RUN_ROWS, RUN_ROWS), RUN_ROWS).start()
                return carry

            lax.fori_loop(0, n, per_unit, 0)
            return total + n

        def wait_unit_zero(u, carry):
            zero_copy(0, RUN_ROWS).wait()
            return carry

        lax.fori_loop(0, lax.fori_loop(0, N_EXPERTS, per_expert, 0), wait_unit_zero, 0)

        last = N_EXPERTS - 1
        first_unused = (ts_ref[last] + tu_ref[last] * RUN_ROWS) // MOE_BLOCK
        n_blocks = xs_out.shape[0] // MOE_BLOCK

        def start_block(b, carry):
            zero_copy(pl.multiple_of(b * MOE_BLOCK, MOE_BLOCK), MOE_BLOCK).start()
            return carry

        def wait_block(b, carry):
            zero_copy(0, MOE_BLOCK).wait()
            return carry

        lax.fori_loop(first_unused, n_blocks, start_block, 0)
        lax.fori_loop(first_unused, n_blocks, wait_block, 0)


def _dispatch(n2, lposT, tables, n_rows, tm):
    T, D = n2.shape
    local_rows = 2 * tm + N_EXPERTS * RUN_ROWS
    grid_spec = pltpu.PrefetchScalarGridSpec(
        num_scalar_prefetch=5,
        grid=(T // tm,),
        in_specs=[pl.BlockSpec((8, tm), lambda i, *_: (0, i)),
                  pl.BlockSpec((tm, D), lambda i, *_: (i, 0))],
        out_specs=pl.BlockSpec(memory_space=pl.ANY),
        scratch_shapes=[pltpu.VMEM((2, local_rows, D), BF16), pltpu.SemaphoreType.DMA((2,))],
    )
    return pl.pallas_call(
        _dispatch_kernel,
        grid_spec=grid_spec,
        out_shape=jax.ShapeDtypeStruct((n_rows, D), BF16),
        compiler_params=_params(("arbitrary",), has_side_effects=True),
        name="dispatch",
    )(*tables, lposT, n2)


def _expert_kernel(be_ref, nused_ref, x_ref, wgu_ref, wd_ref, y_ref):
    i = pl.program_id(0)
    de = wd_ref.shape[0]
    half = x_ref.shape[0] // 2

    @pl.when(i < nused_ref[0])
    def _():
        gu = [jnp.dot(x_ref[r * half:(r + 1) * half, :], wgu_ref[...], preferred_element_type=F32)
              for r in range(2)]
        for r in range(2):
            g, u = gu[r][:, 0:de], gu[r][:, de:2 * de]
            hmid = (g * jax.nn.sigmoid(g) * u).astype(BF16)
            y_ref[r * half:(r + 1) * half, :] = jnp.dot(hmid, wd_ref[...],
                                                        preferred_element_type=F32).astype(y_ref.dtype)

    @pl.when(i >= nused_ref[0])
    def _():
        y_ref[...] = jnp.zeros_like(y_ref)


def _experts(xs, block_e, nused, w_gate, w_up, w_down):
    P, D = xs.shape
    nb = P // MOE_BLOCK
    de = w_gate.shape[2]
    grid_spec = pltpu.PrefetchScalarGridSpec(
        num_scalar_prefetch=2,
        grid=(nb,),
        in_specs=[pl.BlockSpec((MOE_BLOCK, D), lambda i, be, nu: (jnp.maximum(jnp.minimum(i, nu[0] - 1), 0), 0)),
                  pl.BlockSpec((None, D, 2 * de), lambda i, be, nu: (be[i], 0, 0)),
                  pl.BlockSpec((None, de, D), lambda i, be, nu: (be[i], 0, 0))],
        out_specs=pl.BlockSpec((MOE_BLOCK, D), lambda i, be, nu: (i, 0)),
    )
    return pl.pallas_call(
        _expert_kernel,
        grid_spec=grid_spec,
        out_shape=jax.ShapeDtypeStruct((P, D), BF16),
        compiler_params=_params(("arbitrary",)),
        name="experts",
    )(block_e, nused, xs, jnp.concatenate([w_gate, w_up], axis=2).astype(BF16), w_down.astype(BF16))


def _combine_kernel(ls_ref, un_ref, gd_ref, lpos_ref, route_ref, h1_ref, ys_ref, out_ref, yl, sems):
    i = pl.program_id(0)
    nt = pl.num_programs(0)
    tm = h1_ref.shape[0]
    rows = yl.shape[1]
    slot = i % 2

    def piece_copy(s, src, dst, n):
        return pltpu.make_async_copy(ys_ref.at[pl.ds(src, n)], yl.at[s, pl.ds(dst, n)], sems.at[s])

    def gather(tile, s):
        _run_pieces(tile, gd_ref, un_ref, ls_ref,
                    lambda src, dst, n, prio: piece_copy(s, src, dst, n).start(priority=prio))

    @pl.when(i == 0)
    def _():
        yl[...] = jnp.zeros_like(yl)
        gather(0, 0)

    @pl.when(i + 1 < nt)
    def _():
        gather(i + 1, 1 - slot)

    _wait_run_pieces(i, un_ref, lambda n: piece_copy(slot, 0, 0, n).wait())

    lp = lpos_ref[...]
    route = route_ref[...]
    r = lax.broadcasted_iota(I32, (tm, rows), 1)
    w = jnp.where(r == lp[:, 0:1], route[:, 2:3], 0.0) + jnp.where(r == lp[:, 1:2], route[:, 3:4], 0.0)
    out_ref[...] = h1_ref[...] + jnp.dot(w.astype(BF16), yl[slot], preferred_element_type=F32)


def _combine(h1, route, lpos, tables, ys, tm):
    T, D = h1.shape
    local_rows = 2 * tm + N_EXPERTS * RUN_ROWS
    grid_spec = pltpu.PrefetchScalarGridSpec(
        num_scalar_prefetch=3,
        grid=(T // tm,),
        in_specs=[pl.BlockSpec((tm, LANES), lambda i, *_: (i, 0)),
                  pl.BlockSpec((tm, LANES), lambda i, *_: (i, 0)),
                  pl.BlockSpec((tm, D), lambda i, *_: (i, 0)),
                  pl.BlockSpec(memory_space=pl.ANY)],
        out_specs=pl.BlockSpec((tm, D), lambda i, *_: (i, 0)),
        scratch_shapes=[pltpu.VMEM((2, local_rows, D), BF16), pltpu.SemaphoreType.DMA((2,))],
    )
    return pl.pallas_call(
        _combine_kernel,
        grid_spec=grid_spec,
        out_shape=jax.ShapeDtypeStruct((T, D), F32),
        compiler_params=_params(("arbitrary",)),
        name="combine",
    )(*tables, lpos, route, h1, ys)


def _mixers(x, norm_mix, w_in, q_lat_norm, kv_lat_norm, w_uq, w_ukv, q_norm, k_norm, lb_logits, layer):
    B, S, D = x.shape
    T = B * S
    q, k, v, hq, fz, hi, hg = _inproj(x.reshape(T, D), S, norm_mix, w_in, q_lat_norm, kv_lat_norm,
                                      w_uq, w_ukv, q_norm, k_norm)
    score_bound = (math.sqrt(MLA_QK) * math.log2(math.e) * 1.02) * jnp.max(jnp.abs(q_norm)) * jnp.max(jnp.abs(k_norm))
    a = _attention(q, k.reshape(B, S, -1), v, score_bound.astype(F32))
    lb = jnp.cumsum(jax.nn.softmax(lb_logits.astype(F32), axis=0), axis=0)[layer]
    hw = hq.shape[1]
    o = _hgrn(hq.reshape(B, S, hw), fz.reshape(2, B, S, hw), hi.reshape(B, S, hw), lb.reshape(2, 1, hw))
    return a.reshape(T, -1), o.reshape(2, T, hw), hg


def _moe(h1, n2, route, w_gate, w_up, w_down, tm=512):
    T, D = h1.shape
    tm = min(tm, T)
    nt = T // tm
    lpos, lposT, runs, glob = _plan(route, tm)
    per_run = lambda row: runs[:, row, :N_EXPERTS].reshape(-1)
    tables = (per_run(0), per_run(1), per_run(2))
    tails = (glob[1, :N_EXPERTS] * RUN_ROWS, glob[2, :N_EXPERTS])
    n_rows = -(-(2 * T + N_EXPERTS * RUN_ROWS * nt) // MOE_BLOCK) * MOE_BLOCK + N_EXPERTS * MOE_BLOCK
    nb = n_rows // MOE_BLOCK
    pend = glob[0, :N_EXPERTS] * RUN_ROWS
    block_row0 = jnp.arange(nb, dtype=I32) * MOE_BLOCK
    block_e = jnp.minimum(jnp.sum((pend[None, :] <= block_row0[:, None]).astype(I32), axis=1), N_EXPERTS - 1)
    nused = pend[N_EXPERTS - 1:] // MOE_BLOCK
    xs = _dispatch(n2, lposT, tables + tails, n_rows, tm)
    ys = _experts(xs, block_e, nused, w_gate, w_up, w_down)
    return _combine(h1, route, lpos, tables, ys, tm)


def kernel(x, norm_mix, w_in, q_lat_norm, kv_lat_norm, w_uq, w_ukv, q_norm, k_norm, lb_logits, hg_out_norm,
           w_out, norm_ffn, w_group, b_group, w_router, b_router, w_gate, w_up, w_down):
    B, S, D = x.shape
    h = x
    for l in range(norm_mix.shape[0]):
        h2 = h.reshape(B * S, D)
        a, o, hg = _mixers(h, norm_mix[l], w_in[l], q_lat_norm[l], kv_lat_norm[l], w_uq[l], w_ukv[l],
                           q_norm[l], k_norm[l], lb_logits, l)
        h1, n2, route = _outproj(h2, a, o, hg, hg_out_norm[l], w_out[l], norm_ffn[l], w_group[l], b_group[l],
                                 w_router[l], b_router[l])
        h = _moe(h1, n2, route, w_gate[l], w_up[l], w_down[l]).reshape(B, S, D)
    return h
```

```python
import functools
import math

import jax
import jax.numpy as jnp
from jax import lax
from jax.experimental import pallas as pl
from jax.experimental.pallas import tpu as pltpu

F32 = jnp.float32
BF16 = jnp.bfloat16
I32 = jnp.int32

EPS = 1e-6
LANES = 128
VMEM_LIMIT = 48 * 1024 * 1024

MLA_HEADS = 8
MLA_NOPE = 64
MLA_ROPE = 32
MLA_QK = MLA_NOPE + MLA_ROPE
MLA_V = 64
V_ROWS = LANES
ROPE_THETA = 10000.0
ATTN_MAX_FIXED_SHIFT = 40.0
HG_HEADS = 4
HG_DK = 128
HG_DV = 128
HG_CHUNK = 64
HG_SAFE_EXPONENT = 60.0
N_GROUPS = 4
EXPERTS_PER_GROUP = 8
N_EXPERTS = N_GROUPS * EXPERTS_PER_GROUP
MOE_BLOCK = 512
RUN_ROWS = 8
RUN_LONG_SHIFT = 2
RUN_LONG_UNITS = 1 << RUN_LONG_SHIFT

NT_DIMS = (((1,), (1,)), ((), ()))
TN_DIMS = (((0,), (0,)), ((), ()))


def _params(sem, **kw):
    return pltpu.CompilerParams(dimension_semantics=sem, vmem_limit_bytes=VMEM_LIMIT, **kw)


def _full(shape):
    n = len(shape)
    return pl.BlockSpec(shape, lambda *_: (0,) * n)


def _inproj_kernel(x_ref, g_ref, wlatT_ref, wkr_ref, wh_ref, qlgc_ref, kvlgc_ref, kvlg_ref, wuqT_ref, wuk_ref,
                   wuvT_ref, qngc_ref, kng_ref, vonec_ref, cosT_ref, sinT_ref, cos_ref, sa_ref, sb_ref,
                   qT_out, k_out, vT_out, hq_out, fz_out, hi_out, hg_out, *, q_scale):
    x = x_ref[...]
    ms = jnp.mean(x * x, axis=-1, keepdims=True)
    n = (x * lax.rsqrt(ms + EPS) * g_ref[...]).astype(BF16)

    hw = hq_out.shape[1]
    ql = qlgc_ref.shape[0]
    kvl = kvlgc_ref.shape[0]
    half = MLA_ROPE // 2

    def wide(j):
        return jnp.dot(n, wh_ref[:, j * hw:(j + 1) * hw], preferred_element_type=F32).astype(BF16)

    latT = lax.dot_general(wlatT_ref[...], n, NT_DIMS, preferred_element_type=F32)
    latk = jnp.dot(n, wkr_ref[...], preferred_element_type=F32)
    hq_out[...] = wide(0)
    fz_out[0] = wide(1)
    qlT = latT[0:ql]
    kvlT = latT[ql:ql + kvl]
    qnT = (qlT * lax.rsqrt(jnp.mean(qlT * qlT, axis=0, keepdims=True) + EPS) * qlgc_ref[...]).astype(BF16)
    kvnT = (kvlT * lax.rsqrt(jnp.mean(kvlT * kvlT, axis=0, keepdims=True) + EPS) * kvlgc_ref[...]).astype(BF16)
    q_allT = jnp.dot(wuqT_ref[...], qnT, preferred_element_type=F32)
    vT_out[...] = (jnp.dot(wuvT_ref[...], kvnT, preferred_element_type=F32) + vonec_ref[...]).astype(BF16)
    kvl_r = latk[:, 0:kvl]
    kr = latk[:, kvl:kvl + LANES]
    kvn = (kvl_r * lax.rsqrt(jnp.mean(kvl_r * kvl_r, axis=-1, keepdims=True) + EPS) * kvlg_ref[...]).astype(BF16)
    k_all = jnp.dot(kvn, wuk_ref[...], preferred_element_type=F32)
    fz_out[1] = wide(2)
    hi_out[...] = wide(3)
    hg_out[...] = wide(4)

    cosT = cosT_ref[...]
    sinT = sinT_ref[...]
    qngc = qngc_ref[...]
    for h in range(MLA_HEADS):
        t = q_allT[h * LANES:(h + 1) * LANES]
        tn = t * lax.rsqrt(jnp.sum(t * t, axis=0, keepdims=True) * (1.0 / MLA_QK) + EPS) * qngc
        x1 = tn[MLA_NOPE:MLA_NOPE + half]
        x2 = tn[MLA_NOPE + half:MLA_QK]
        rot = jnp.concatenate([tn[0:MLA_NOPE], x1 * cosT - x2 * sinT, x2 * cosT + x1 * sinT, tn[MLA_QK:LANES]],
                              axis=0)
        qT_out[h * LANES:(h + 1) * LANES, :] = (rot * q_scale).astype(BF16)

    kng = kng_ref[...]
    krg = kr * kng
    up = pltpu.roll(krg, LANES - half, axis=1)
    dn = pltpu.roll(krg, half, axis=1)
    kr_rot = krg * cos_ref[...] + up * sa_ref[...] + dn * sb_ref[...]
    kr_ssq = jnp.sum(kr * kr, axis=-1, keepdims=True)
    for h in range(MLA_HEADS):
        sl = slice(h * LANES, (h + 1) * LANES)
        t = k_all[:, sl]
        r = lax.rsqrt((jnp.sum(t * t, axis=-1, keepdims=True) + kr_ssq) * (1.0 / MLA_QK) + EPS)
        k_out[:, sl] = (r * (t * kng + kr_rot)).astype(BF16)


def _inproj(x2, seq, norm_mix, w_in, q_lat_norm, kv_lat_norm, w_uq, w_ukv, q_norm, k_norm, tm=512):
    T, D = x2.shape
    tm = min(tm, seq)
    H = MLA_HEADS
    ql, kvl = w_uq.shape[0], w_ukv.shape[0]
    hw = HG_HEADS * HG_DK
    o_kr = ql + kvl
    o_h = o_kr + MLA_ROPE
    zeros = lambda c: jnp.zeros((D, c), F32)
    w_latT = w_in[:, :o_kr].T.astype(BF16)
    w_kr = jnp.concatenate([w_in[:, ql:o_kr], zeros(MLA_NOPE), w_in[:, o_kr:o_h],
                            zeros(LANES - MLA_QK)], axis=1).astype(BF16)
    w_h = w_in[:, o_h:].astype(BF16)
    pad = LANES - MLA_QK
    wuqT = jnp.pad(w_uq.reshape(ql, H, MLA_QK), ((0, 0), (0, 0), (0, pad))).reshape(ql, H * LANES).T.astype(BF16)
    wkv = w_ukv.reshape(kvl, H, MLA_NOPE + MLA_V)
    wuk = jnp.pad(wkv[:, :, :MLA_NOPE], ((0, 0), (0, 0), (0, LANES - MLA_NOPE))).reshape(kvl, H * LANES).astype(BF16)
    wuvT = jnp.pad(wkv[:, :, MLA_NOPE:], ((0, 0), (0, 0), (0, V_ROWS - MLA_V))).reshape(kvl, H * V_ROWS).T.astype(BF16)
    vonec = jnp.tile(jnp.concatenate([jnp.zeros((MLA_V,), F32), jnp.ones((V_ROWS - MLA_V,), F32)]), H)
    vonec = vonec.reshape(H * V_ROWS, 1)
    qngc = jnp.pad(q_norm, (0, pad)).reshape(LANES, 1)
    kng = jnp.pad(k_norm, (0, pad)).reshape(1, LANES)

    half = MLA_ROPE // 2
    inv = 1.0 / (ROPE_THETA ** (jnp.arange(half, dtype=F32) / half))
    ang = jnp.arange(seq, dtype=F32)[:, None] * inv[None, :]
    cos, sin = jnp.cos(ang), jnp.sin(ang)
    z = lambda c: jnp.zeros((seq, c), F32)
    cos_t = jnp.concatenate([jnp.ones((seq, MLA_NOPE), F32), cos, cos, z(pad)], axis=1)
    sa_t = jnp.concatenate([z(MLA_NOPE), -sin, z(half), z(pad)], axis=1)
    sb_t = jnp.concatenate([z(MLA_NOPE), z(half), sin, z(pad)], axis=1)

    nseq = seq // tm
    row = lambda w: pl.BlockSpec((tm, w), lambda i: (i, 0))
    col = lambda r: pl.BlockSpec((r, tm), lambda i: (0, i))
    tab = pl.BlockSpec((tm, LANES), lambda i: (i % nseq, 0))
    tabT = pl.BlockSpec((half, tm), lambda i: (0, i % nseq))
    q_scale = (MLA_QK ** -0.5) * math.log2(math.e)
    outs = pl.pallas_call(
        functools.partial(_inproj_kernel, q_scale=q_scale),
        grid=(T // tm,),
        in_specs=[row(D), _full((1, D)), _full(w_latT.shape), _full(w_kr.shape), _full(w_h.shape),
                  _full((ql, 1)), _full((kvl, 1)), _full((1, kvl)), _full(wuqT.shape), _full(wuk.shape),
                  _full(wuvT.shape), _full((LANES, 1)), _full((1, LANES)), _full((H * V_ROWS, 1)),
                  tabT, tabT, tab, tab, tab],
        out_specs=[col(H * LANES), row(H * LANES), col(H * V_ROWS), row(hw),
                   pl.BlockSpec((2, tm, hw), lambda i: (0, i, 0)), row(hw), row(hw)],
        out_shape=[jax.ShapeDtypeStruct((H * LANES, T), BF16), jax.ShapeDtypeStruct((T, H * LANES), BF16),
                   jax.ShapeDtypeStruct((H * V_ROWS, T), BF16), jax.ShapeDtypeStruct((T, hw), BF16),
                   jax.ShapeDtypeStruct((2, T, hw), BF16), jax.ShapeDtypeStruct((T, hw), BF16),
                   jax.ShapeDtypeStruct((T, hw), BF16)],
        compiler_params=_params(("parallel",)),
        name="inproj",
    )(x2, norm_mix.reshape(1, D), w_latT, w_kr, w_h, q_lat_norm.reshape(ql, 1), kv_lat_norm.reshape(kvl, 1),
      kv_lat_norm.reshape(1, kvl), wuqT, wuk, wuvT, qngc, kng, vonec, cos.T, sin.T, cos_t, sa_t, sb_t)
    return outs


def _attn_kernel(qT_ref, k_ref, vT_ref, o_ref, s00, s01, s10, s11, *, tk):
    tq = qT_ref.shape[1]
    nk = k_ref.shape[0] // tk
    qTs = [qT_ref[j * LANES:(j + 1) * LANES, :] for j in range(2)]
    s_bufs = ((s00, s01), (s10, s11))

    def scores(j, slot, c):
        r0 = pl.multiple_of(c * tk, tk)
        sT = jnp.dot(k_ref[pl.ds(r0, tk), j * LANES:(j + 1) * LANES], qTs[j], preferred_element_type=F32)
        s_bufs[j][slot][...] = sT
        return jnp.max(sT, axis=0, keepdims=True)

    def absorb(j, slot, c, m, acc, mx):
        r0 = pl.multiple_of(c * tk, tk)
        m_new = jnp.maximum(m, mx)
        pT = jnp.exp2(s_bufs[j][slot][...] - m_new).astype(BF16)
        pv = jnp.dot(vT_ref[j * V_ROWS:(j + 1) * V_ROWS, pl.ds(r0, tk)], pT, preferred_element_type=F32)
        return m_new, jnp.exp2(m - m_new) * acc + pv

    def step(c, slot, state, prefetch):
        mx_next = [scores(j, 1 - slot, c + 1) if prefetch else state[j][2] for j in range(2)]
        new = []
        for j in range(2):
            m, acc, mx = state[j]
            m, acc = absorb(j, slot, c, m, acc, mx)
            new.append((m, acc, mx_next[j]))
        return tuple(new)

    def pair(i, state):
        c = 2 * i
        return step(c + 1, 1, step(c, 0, state, True), True)

    state = tuple((jnp.full((1, tq), -jnp.inf, F32), jnp.zeros((V_ROWS, tq), F32), scores(j, 0, 0))
                  for j in range(2))
    state = lax.fori_loop(0, nk // 2 - 1, pair, state)
    state = step(nk - 1, 1, step(nk - 2, 0, state, True), False)
    acc0, acc1 = state[0][1], state[1][1]
    oT = jnp.concatenate([acc0[0:MLA_V] / acc0[MLA_V:MLA_V + 1], acc1[0:MLA_V] / acc1[MLA_V:MLA_V + 1]], axis=0)
    o_ref[...] = oT.T.astype(o_ref.dtype)


def _attn_bounded_kernel(shift_ref, qT_ref, k_ref, vT_ref, o_ref, p00, p01, p10, p11, *, tk):
    tq = qT_ref.shape[1]
    nk = k_ref.shape[0] // tk
    shift = shift_ref[0]
    qTs = [qT_ref[j * LANES:(j + 1) * LANES, :] for j in range(2)]
    p_bufs = ((p00, p01), (p10, p11))

    def probs(j, slot, c):
        r0 = pl.multiple_of(c * tk, tk)
        sT = jnp.dot(k_ref[pl.ds(r0, tk), j * LANES:(j + 1) * LANES], qTs[j], preferred_element_type=F32)
        p_bufs[j][slot][...] = jnp.exp2(sT - shift).astype(BF16)

    def absorb(j, slot, c, acc):
        r0 = pl.multiple_of(c * tk, tk)
        return acc + jnp.dot(vT_ref[j * V_ROWS:(j + 1) * V_ROWS, pl.ds(r0, tk)], p_bufs[j][slot][...],
                             preferred_element_type=F32)

    def step(c, slot, accs, prefetch):
        if prefetch:
            for j in range(2):
                probs(j, 1 - slot, c + 1)
        return tuple(absorb(j, slot, c, accs[j]) for j in range(2))

    def pair(i, accs):
        c = 2 * i
        return step(c + 1, 1, step(c, 0, accs, True), True)

    for j in range(2):
        probs(j, 0, 0)
    accs = tuple(jnp.zeros((V_ROWS, tq), F32) for _ in range(2))
    accs = lax.fori_loop(0, nk // 2 - 1, pair, accs)
    acc0, acc1 = step(nk - 1, 1, step(nk - 2, 0, accs, True), False)
    oT = jnp.concatenate([acc0[0:MLA_V] / acc0[MLA_V:MLA_V + 1], acc1[0:MLA_V] / acc1[MLA_V:MLA_V + 1]], axis=0)
    o_ref[...] = oT.T.astype(o_ref.dtype)


def _attention(qT, k3, vT, score_bound, tq=512, tk=512):
    B, S, _ = k3.shape
    tq, tk = min(tq, S), min(tk, S)
    nq = S // tq
    hp = MLA_HEADS // 2
    specs = dict(
        grid=(B, hp, nq),
        in_specs=[pl.BlockSpec((2 * LANES, tq), lambda b, h, i: (h, b * nq + i)),
                  pl.BlockSpec((None, S, 2 * LANES), lambda b, h, i: (b, 0, h)),
                  pl.BlockSpec((2 * V_ROWS, S), lambda b, h, i: (h, b))],
        out_specs=pl.BlockSpec((None, tq, 2 * MLA_V), lambda b, h, i: (b, i, h)),
        out_shape=jax.ShapeDtypeStruct((B, S, MLA_HEADS * MLA_V), BF16),
        compiler_params=_params(("parallel", "parallel", "arbitrary")),
    )

    def bounded():
        tkb = max(min(4 * tk, S // 2), tk)
        in_specs = [pl.BlockSpec(memory_space=pltpu.SMEM)] + specs["in_specs"]
        return pl.pallas_call(functools.partial(_attn_bounded_kernel, tk=tkb), name="attention_bounded",
                              scratch_shapes=[pltpu.VMEM((tkb, tq), BF16)] * 4,
                              **{**specs, "in_specs": in_specs})(score_bound.reshape(1), qT, k3, vT)

    def online():
        return pl.pallas_call(functools.partial(_attn_kernel, tk=tk), name="attention",
                              scratch_shapes=[pltpu.VMEM((tk, tq), F32)] * 4, **specs)(qT, k3, vT)

    return lax.cond(score_bound <= ATTN_MAX_FIXED_SHIFT, bounded, online)


def _hgrn_kernel(hq_ref, z_ref, hi_ref, lb_ref, tri_ref, o_ref, st_ref, kk_scr, b_scr, edge_scr, qh_scr, a_scr,
                 u_scr, qf_scr, of_scr):
    d = pl.program_id(1)
    C = HG_CHUNK
    tl = hq_ref.shape[0]
    nc = tl // C
    heads = [slice(h * HG_DK, (h + 1) * HG_DK) for h in range(HG_HEADS)]

    @pl.when(pl.program_id(2) == 0)
    def _():
        st_ref[...] = jnp.zeros_like(st_ref)

    one_m_lb = 1.0 - lb_ref[...]
    tri = tri_ref[...]
    keep = tri > 0
    fwd = d == 0

    worst = jnp.zeros_like(one_m_lb)
    for c in range(nc):
        rows = slice(c * C, (c + 1) * C)
        kk = one_m_lb * jax.nn.sigmoid(-z_ref[rows, :].astype(F32))
        g = jnp.log(1.0 - kk)
        g_hi = g.astype(BF16)
        g_lo = (g - g_hi.astype(F32)).astype(BF16)
        b = jnp.dot(tri, g_hi, preferred_element_type=F32) + jnp.dot(tri, g_lo, preferred_element_type=F32)
        b_edge = jnp.where(fwd, b[C - 1:C, :], b[0:1, :])
        kk_scr[rows, :] = kk
        b_scr[rows, :] = b
        edge_scr[c] = b_edge
        worst = jnp.maximum(worst, -b_edge)
    safe = jnp.max(worst) < HG_SAFE_EXPONENT

    @pl.when(safe)
    def _():
        for c in range(nc):
            rows = slice(c * C, (c + 1) * C)
            kk = kk_scr[rows, :]
            b = b_scr[rows, :]
            hq = hq_ref[rows, :].astype(F32)
            qh = (hq * jax.nn.sigmoid(hq) * jnp.exp(b)).astype(BF16)
            kt32 = kk * jnp.exp(-b)
            kt = kt32.astype(BF16)
            ks = (kt32 * jnp.exp(edge_scr[c])).astype(BF16)
            v = hi_ref[rows, :]
            qh_scr[rows, :] = qh
            for h, sl in enumerate(heads):
                a = lax.dot_general(qh[:, sl], kt[:, sl], NT_DIMS, preferred_element_type=F32)
                a_scr[c, h] = jnp.where(keep, a, 0.0).astype(BF16)
                u_scr[c, h] = lax.dot_general(v[:, sl], ks[:, sl], TN_DIMS, preferred_element_type=F32)
        st = [st_ref[h] for h in range(HG_HEADS)]
        for p in range(nc):
            c = jnp.where(fwd, p, nc - 1 - p)
            r0 = pl.multiple_of(c * C, C)
            dec = jnp.exp(edge_scr[c])
            qh = qh_scr[pl.ds(r0, C), :]
            v = hi_ref[pl.ds(r0, C), :]
            for h, sl in enumerate(heads):
                o = jnp.dot(a_scr[c, h], v[:, sl], preferred_element_type=F32)
                o = o + lax.dot_general(qh[:, sl], st[h].astype(BF16), NT_DIMS, preferred_element_type=F32)
                o_ref[pl.ds(r0, C), sl] = o.astype(o_ref.dtype)
                st[h] = st[h] * dec[:, sl] + u_scr[c, h]
        for h in range(HG_HEADS):
            st_ref[h] = st[h]

    @pl.when(jnp.logical_not(safe))
    def _():
        hq = hq_ref[...].astype(F32)
        qf_scr[...] = hq * jax.nn.sigmoid(hq)
        first = lax.broadcasted_iota(I32, (16, HG_DK), 0) == 0

        def row(i, carry):
            t = jnp.where(fwd, i, tl - 1 - i)
            kk = kk_scr[pl.ds(t, 1), :]
            f = 1.0 - kk
            q = qf_scr[pl.ds(t, 1), :]
            g0 = pl.multiple_of((t // 16) * 16, 16)
            v = hi_ref[pl.ds(g0, 16), :].astype(F32)
            v = jnp.sum(jnp.where(lax.broadcasted_iota(I32, v.shape, 0) == t % 16, v, 0.0), axis=0, keepdims=True)
            outs = []
            for h, sl in enumerate(heads):
                pad = lambda x: jnp.where(first, jnp.broadcast_to(x[:, sl], (16, HG_DK)), 0.0).astype(BF16)
                st = st_ref[h] * f[:, sl] + lax.dot_general(pad(v), pad(kk), TN_DIMS, preferred_element_type=F32)
                st_ref[h] = st
                outs.append(lax.dot_general(pad(q), st.astype(BF16), NT_DIMS, preferred_element_type=F32)[0:1])
            of_scr[pl.ds(t, 1), :] = jnp.concatenate(outs, axis=1)
            return carry

        lax.fori_loop(0, tl, row, 0)
        o_ref[...] = of_scr[...].astype(o_ref.dtype)


def _hgrn(hq3, fz4, hi3, lb, tl=512):
    B, S, W = hq3.shape
    tl = min(tl, S)
    nt = S // tl
    C = HG_CHUNK
    r = lax.broadcasted_iota(I32, (C, C), 0)
    c = lax.broadcasted_iota(I32, (C, C), 1)
    tri = jnp.stack([r >= c, r <= c]).astype(BF16)
    tile = lambda b, d, i: (b, i + d * (nt - 1 - 2 * i), 0)
    return pl.pallas_call(
        _hgrn_kernel,
        grid=(B, 2, nt),
        in_specs=[pl.BlockSpec((None, tl, W), tile),
                  pl.BlockSpec((None, None, tl, W), lambda b, d, i: (d, b, i + d * (nt - 1 - 2 * i), 0)),
                  pl.BlockSpec((None, tl, W), tile),
                  pl.BlockSpec((None, 1, W), lambda b, d, i: (d, 0, 0)),
                  pl.BlockSpec((None, C, C), lambda b, d, i: (d, 0, 0))],
        out_specs=pl.BlockSpec((None, None, tl, W), lambda b, d, i: (d, b, i + d * (nt - 1 - 2 * i), 0)),
        out_shape=jax.ShapeDtypeStruct((2, B, S, W), BF16),
        scratch_shapes=[pltpu.VMEM((HG_HEADS, HG_DV, HG_DK), F32),
                        pltpu.VMEM((tl, W), F32), pltpu.VMEM((tl, W), F32),
                        pltpu.VMEM((tl // C, 1, W), F32),
                        pltpu.VMEM((tl, W), BF16),
                        pltpu.VMEM((tl // C, HG_HEADS, C, C), BF16),
                        pltpu.VMEM((tl // C, HG_HEADS, HG_DV, HG_DK), F32),
                        pltpu.VMEM((tl, W), F32), pltpu.VMEM((tl, W), F32)],
        compiler_params=_params(("parallel", "parallel", "arbitrary")),
        name="hgrn",
    )(hq3, fz4, hi3, lb, tri)


def _outproj_kernel(x_ref, a_ref, o_ref, hg_ref, ong_ref, wa_ref, wr_ref, g2_ref, wrt_ref, brt_ref,
                    h1_out, n2_out, route_out, tot_out, tot_scr):
    o = o_ref[0].astype(F32) + o_ref[1].astype(F32)
    hg = hg_ref[...].astype(F32)
    gate = hg * jax.nn.sigmoid(hg)
    ong = ong_ref[...]
    parts = []
    for h in range(HG_HEADS):
        sl = slice(h * HG_DV, (h + 1) * HG_DV)
        oh = o[:, sl]
        parts.append((oh * lax.rsqrt(jnp.mean(oh * oh, axis=-1, keepdims=True) + EPS) * ong * gate[:, sl]).astype(BF16))
    r = jnp.concatenate(parts, axis=1)
    h1 = x_ref[...] + jnp.dot(a_ref[...], wa_ref[...], preferred_element_type=F32)
    h1 = h1 + jnp.dot(r, wr_ref[...], preferred_element_type=F32)
    h1_out[...] = h1
    n2 = h1 * lax.rsqrt(jnp.mean(h1 * h1, axis=-1, keepdims=True) + EPS) * g2_ref[...]
    n2_hi = n2.astype(BF16)
    n2_out[...] = n2_hi
    n2_lo = (n2 - n2_hi.astype(F32)).astype(BF16)
    l_hi = jnp.dot(n2_hi, wrt_ref[...], preferred_element_type=F32)
    l_lo = jnp.dot(n2_lo, wrt_ref[:, 0:LANES], preferred_element_type=F32)
    logits = l_hi[:, 0:LANES] + l_hi[:, LANES:2 * LANES] + l_lo + brt_ref[...]
    tm = logits.shape[0]
    lane = lax.broadcasted_iota(I32, (tm, LANES), 1)
    ninf = -jnp.inf
    is_g = lane < N_GROUPS
    gl = jnp.where(is_g, logits, ninf)
    gmax = jnp.max(gl, axis=-1, keepdims=True)
    gidx = jnp.min(jnp.where(gl == gmax, lane, LANES), axis=-1, keepdims=True)
    g_w = 1.0 / jnp.sum(jnp.where(is_g, jnp.exp(logits - gmax), 0.0), axis=-1, keepdims=True)
    lo = N_GROUPS + EXPERTS_PER_GROUP * gidx
    el = jnp.where((lane >= lo) & (lane < lo + EXPERTS_PER_GROUP), logits, ninf)
    m1 = jnp.max(el, axis=-1, keepdims=True)
    i1 = jnp.min(jnp.where(el == m1, lane, LANES), axis=-1, keepdims=True)
    el2 = jnp.where(lane == i1, ninf, el)
    m2 = jnp.max(el2, axis=-1, keepdims=True)
    i2 = jnp.min(jnp.where(el2 == m2, lane, LANES), axis=-1, keepdims=True)
    t = jnp.exp(m2 - m1)
    w1 = 1.0 / (1.0 + t)
    w2 = t / (1.0 + t)
    e1 = (i1 - N_GROUPS).astype(F32)
    e2 = (i2 - N_GROUPS).astype(F32)
    route = jnp.where(lane == 0, e1, jnp.where(lane == 1, e2, jnp.where(lane == 2, g_w * w1,
                      jnp.where(lane == 3, g_w * w2, 0.0))))
    route_out[...] = route

    @pl.when(pl.program_id(0) == 0)
    def _():
        tot_scr[...] = jnp.zeros_like(tot_scr)

    onehot = jnp.where((lane == i1 - N_GROUPS) | (lane == i2 - N_GROUPS), 1.0, 0.0)
    tot_scr[...] += jnp.ceil(jnp.sum(onehot, axis=0, keepdims=True) * (1.0 / RUN_ROWS))
    tot_out[...] = jnp.broadcast_to(tot_scr[...], tot_out.shape)


def _outproj(x2, a2, o3, hg2, hg_out_norm, w_out, norm_ffn, w_group, b_group, w_router, b_router, tm=512):
    T, D = x2.shape
    tm = min(tm, T)
    wa = w_out[:MLA_HEADS * MLA_V].astype(BF16)
    wr = w_out[MLA_HEADS * MLA_V:].astype(BF16)
    npad = LANES - N_GROUPS - N_EXPERTS
    wrt = jnp.concatenate([w_group, w_router, jnp.zeros((D, npad), F32)], axis=1)
    wrt_hi = wrt.astype(BF16)
    wrt = jnp.concatenate([wrt_hi, (wrt - wrt_hi.astype(F32)).astype(BF16)], axis=1)
    brt = jnp.concatenate([b_group, b_router, jnp.zeros((npad,), F32)]).reshape(1, LANES)
    row = lambda w: pl.BlockSpec((tm, w), lambda i: (i, 0))
    hw = HG_HEADS * HG_DV
    return pl.pallas_call(
        _outproj_kernel,
        grid=(T // tm,),
        in_specs=[row(D), row(a2.shape[1]), pl.BlockSpec((2, tm, hw), lambda i: (0, i, 0)), row(hw),
                  _full((1, HG_DV)), _full(wa.shape), _full(wr.shape), _full((1, D)), _full(wrt.shape),
                  _full((1, LANES))],
        out_specs=[row(D), row(D), row(LANES), _full((8, LANES))],
        out_shape=[jax.ShapeDtypeStruct((T, D), F32), jax.ShapeDtypeStruct((T, D), BF16),
                   jax.ShapeDtypeStruct((T, LANES), F32), jax.ShapeDtypeStruct((8, LANES), F32)],
        scratch_shapes=[pltpu.VMEM((1, LANES), F32)],
        compiler_params=_params(("arbitrary",)),
        name="outproj",
    )(x2, a2, o3, hg2, hg_out_norm.reshape(1, HG_DV), wa, wr, norm_ffn.reshape(1, D), wrt, brt)


def _plan_kernel(route_ref, tot_ref, tri_ref, upper_ref, lpos_out, lposT_out, runs_out, glob_out, base_ref):
    i = pl.program_id(0)
    tm = route_ref.shape[0]
    lane = lax.broadcasted_iota(I32, (tm, LANES), 1)
    route = route_ref[...]
    is1 = lane == route[:, 0:1].astype(I32)
    is2 = lane == route[:, 1:2].astype(I32)
    onehot = jnp.where(is1 | is2, 1.0, 0.0)
    units = jnp.ceil(jnp.sum(onehot, axis=0, keepdims=True) * (1.0 / RUN_ROWS))
    sub = lax.broadcasted_iota(I32, (8, LANES), 0)
    rows3 = lambda a, b, c: jnp.where(sub == 0, a, jnp.where(sub == 1, b, jnp.where(sub == 2, c, 0.0)))

    @pl.when(i == 0)
    def _():
        tot = tot_ref[0:1, :]
        block_units = MOE_BLOCK // RUN_ROWS
        padded = jnp.ceil(tot * (1.0 / block_units)) * block_units
        start = jnp.dot(jnp.broadcast_to(padded, (8, LANES)), upper_ref[...], preferred_element_type=F32,
                        precision=lax.Precision.HIGHEST)[0:1]
        base_ref[...] = start
        glob_out[...] = rows3(start + padded, start + tot, padded - tot).astype(I32)

    before = jnp.dot(tri_ref[...], onehot.astype(BF16), preferred_element_type=F32)
    local = jnp.dot(jnp.broadcast_to(units, (8, LANES)).astype(BF16), upper_ref[...].astype(BF16),
                    preferred_element_type=F32)[0:1] * RUN_ROWS
    pos = local + before
    p1 = jnp.sum(jnp.where(is1, pos, 0.0), axis=-1, keepdims=True)
    p2 = jnp.sum(jnp.where(is2, pos, 0.0), axis=-1, keepdims=True)
    slab = jnp.where(lane == 0, p1, jnp.where(lane == 1, p2, 0.0))
    lpos_out[...] = slab.astype(I32)
    lposT_out[...] = slab.T[0:8].astype(I32)
    runs_out[...] = rows3(local, units, base_ref[...] * RUN_ROWS).astype(I32)
    base_ref[...] += units


def _plan(route, totals, tm):
    T = route.shape[0]
    nt = T // tm
    r = lax.broadcasted_iota(I32, (tm, tm), 0)
    c = lax.broadcasted_iota(I32, (tm, tm), 1)
    tri = (r > c).astype(BF16)
    ru = lax.broadcasted_iota(I32, (LANES, LANES), 0)
    cu = lax.broadcasted_iota(I32, (LANES, LANES), 1)
    upper = (ru < cu).astype(F32)
    return pl.pallas_call(
        _plan_kernel,
        grid=(nt,),
        in_specs=[pl.BlockSpec((tm, LANES), lambda i: (i, 0)), _full((8, LANES)), _full((tm, tm)),
                  _full((LANES, LANES))],
        out_specs=[pl.BlockSpec((tm, LANES), lambda i: (i, 0)),
                   pl.BlockSpec((8, tm), lambda i: (0, i)),
                   pl.BlockSpec((None, 8, LANES), lambda i: (i, 0, 0)),
                   _full((8, LANES))],
        out_shape=[jax.ShapeDtypeStruct((T, LANES), I32), jax.ShapeDtypeStruct((8, T), I32),
                   jax.ShapeDtypeStruct((nt, 8, LANES), I32), jax.ShapeDtypeStruct((8, LANES), I32)],
        scratch_shapes=[pltpu.VMEM((1, LANES), F32)],
        compiler_params=_params(("arbitrary",)),
        name="plan",
    )(route, totals, tri, upper)


def _run_pieces(tile, start_a_ref, units_ref, start_b_ref, fn):
    def per_expert(e, carry):
        j = tile * N_EXPERTS + e
        a0, b0, n = start_a_ref[j], start_b_ref[j], units_ref[j]
        n_long = lax.shift_right_logical(n, RUN_LONG_SHIFT)
        rest0 = n_long * (RUN_LONG_UNITS * RUN_ROWS)

        def long_piece(u, c):
            o = u * (RUN_LONG_UNITS * RUN_ROWS)
            fn(pl.multiple_of(a0 + o, RUN_ROWS), pl.multiple_of(b0 + o, RUN_ROWS), RUN_LONG_UNITS * RUN_ROWS, 0)
            return c

        def unit_piece(u, c):
            o = rest0 + u * RUN_ROWS
            fn(pl.multiple_of(a0 + o, RUN_ROWS), pl.multiple_of(b0 + o, RUN_ROWS), RUN_ROWS, 1)
            return c

        lax.fori_loop(0, n_long, long_piece, 0)
        lax.fori_loop(0, n & (RUN_LONG_UNITS - 1), unit_piece, 0)
        return carry

    lax.fori_loop(0, N_EXPERTS, per_expert, 0)


def _wait_run_pieces(tile, units_ref, wait_fn):
    def count(e, tot):
        n = units_ref[tile * N_EXPERTS + e]
        return tot[0] + lax.shift_right_logical(n, RUN_LONG_SHIFT), tot[1] + (n & (RUN_LONG_UNITS - 1))

    n_long, n_unit = lax.fori_loop(0, N_EXPERTS, count, (jnp.int32(0), jnp.int32(0)))

    def wait_long(u, c):
        wait_fn(RUN_LONG_UNITS * RUN_ROWS)
        return c

    def wait_unit(u, c):
        wait_fn(RUN_ROWS)
        return c

    lax.fori_loop(0, n_long, wait_long, 0)
    lax.fori_loop(0, n_unit, wait_unit, 0)


def _dispatch_kernel(ls_ref, un_ref, gd_ref, ts_ref, tu_ref, lposT_ref, n2_ref, xs_out, xl, sems):
    i = pl.program_id(0)
    nt = pl.num_programs(0)
    rows = xl.shape[1]
    tm = n2_ref.shape[0]
    slot = i % 2
    lp = lposT_ref[...]
    r = lax.broadcasted_iota(I32, (rows, tm), 0)
    pick = jnp.where((r == lp[0:1, :]) | (r == lp[1:2, :]), 1.0, 0.0).astype(BF16)
    xl[slot] = jnp.dot(pick, n2_ref[...], preferred_element_type=F32)

    def piece_copy(s, src, dst, n):
        return pltpu.make_async_copy(xl.at[s, pl.ds(src, n)], xs_out.at[pl.ds(dst, n)], sems.at[s])

    def wait_tile(tile, s):
        _wait_run_pieces(tile, un_ref, lambda n: piece_copy(s, 0, 0, n).wait())

    _run_pieces(i, ls_ref, un_ref, gd_ref,
                lambda src, dst, n, prio: piece_copy(slot, src, dst, n).start(priority=prio))

    @pl.when(i > 0)
    def _():
        wait_tile(i - 1, 1 - slot)

    @pl.when(i == nt - 1)
    def _():
        wait_tile(i, slot)
        xl[0] = jnp.zeros((rows, xl.shape[2]), xl.dtype)
        sem = sems.at[0]

        def zero_copy(dst, n):
            return pltpu.make_async_copy(xl.at[0, pl.ds(0, n)], xs_out.at[pl.ds(dst, n)], sem)

        def per_expert(e, total):
            d0, n = ts_ref[e], tu_ref[e]

            def per_unit(u, carry):
                zero_copy(pl.multiple_of(d0 + u * RUN_ROWS, RUN_ROWS), RUN_ROWS).start()
                return carry

            lax.fori_loop(0, n, per_unit, 0)
            return total + n

        def wait_unit_zero(u, carry):
            zero_copy(0, RUN_ROWS).wait()
            return carry

        lax.fori_loop(0, lax.fori_loop(0, N_EXPERTS, per_expert, 0), wait_unit_zero, 0)

        last = N_EXPERTS - 1
        first_unused = (ts_ref[last] + tu_ref[last] * RUN_ROWS) // MOE_BLOCK
        n_blocks = xs_out.shape[0] // MOE_BLOCK

        def start_block(b, carry):
            zero_copy(pl.multiple_of(b * MOE_BLOCK, MOE_BLOCK), MOE_BLOCK).start()
            return carry

        def wait_block(b, carry):
            zero_copy(0, MOE_BLOCK).wait()
            return carry

        lax.fori_loop(first_unused, n_blocks, start_block, 0)
        lax.fori_loop(first_unused, n_blocks, wait_block, 0)


def _dispatch(n2, lposT, tables, n_rows, tm):
    T, D = n2.shape
    local_rows = 2 * tm + N_EXPERTS * RUN_ROWS
    grid_spec = pltpu.PrefetchScalarGridSpec(
        num_scalar_prefetch=5,
        grid=(T // tm,),
        in_specs=[pl.BlockSpec((8, tm), lambda i, *_: (0, i)),
                  pl.BlockSpec((tm, D), lambda i, *_: (i, 0))],
        out_specs=pl.BlockSpec(memory_space=pl.ANY),
        scratch_shapes=[pltpu.VMEM((2, local_rows, D), F32), pltpu.SemaphoreType.DMA((2,))],
    )
    return pl.pallas_call(
        _dispatch_kernel,
        grid_spec=grid_spec,
        out_shape=jax.ShapeDtypeStruct((n_rows, D), F32),
        compiler_params=_params(("arbitrary",), has_side_effects=True),
        name="dispatch",
    )(*tables, lposT, n2)


def _expert_kernel(be_ref, nused_ref, x_ref, wgu_ref, wd_ref, y_ref):
    i = pl.program_id(0)
    de = wd_ref.shape[0]
    half = x_ref.shape[0] // 2

    @pl.when(i < nused_ref[0])
    def _():
        gu = [jnp.dot(x_ref[r * half:(r + 1) * half, :].astype(BF16), wgu_ref[...], preferred_element_type=F32)
              for r in range(2)]
        for r in range(2):
            g, u = gu[r][:, 0:de], gu[r][:, de:2 * de]
            hmid = (g * jax.nn.sigmoid(g) * u).astype(BF16)
            y_ref[r * half:(r + 1) * half, :] = jnp.dot(hmid, wd_ref[...], preferred_element_type=F32)

    @pl.when(i >= nused_ref[0])
    def _():
        y_ref[...] = jnp.zeros_like(y_ref)


def _experts(xs, block_e, nused, w_gate, w_up, w_down):
    P, D = xs.shape
    nb = P // MOE_BLOCK
    de = w_gate.shape[2]
    grid_spec = pltpu.PrefetchScalarGridSpec(
        num_scalar_prefetch=2,
        grid=(nb,),
        in_specs=[pl.BlockSpec((MOE_BLOCK, D), lambda i, be, nu: (jnp.maximum(jnp.minimum(i, nu[0] - 1), 0), 0)),
                  pl.BlockSpec((None, D, 2 * de), lambda i, be, nu: (be[i], 0, 0)),
                  pl.BlockSpec((None, de, D), lambda i, be, nu: (be[i], 0, 0))],
        out_specs=pl.BlockSpec((MOE_BLOCK, D), lambda i, be, nu: (i, 0)),
    )
    return pl.pallas_call(
        _expert_kernel,
        grid_spec=grid_spec,
        out_shape=jax.ShapeDtypeStruct((P, D), F32),
        compiler_params=_params(("arbitrary",)),
        name="experts",
    )(block_e, nused, xs, jnp.concatenate([w_gate, w_up], axis=2).astype(BF16), w_down.astype(BF16))


def _combine_kernel(ls_ref, un_ref, gd_ref, lpos_ref, route_ref, h1_ref, ys_ref, out_ref, yl, sems):
    i = pl.program_id(0)
    nt = pl.num_programs(0)
    tm = h1_ref.shape[0]
    rows = yl.shape[1]
    slot = i % 2

    def piece_copy(s, src, dst, n):
        return pltpu.make_async_copy(ys_ref.at[pl.ds(src, n)], yl.at[s, pl.ds(dst, n)], sems.at[s])

    def gather(tile, s):
        _run_pieces(tile, gd_ref, un_ref, ls_ref,
                    lambda src, dst, n, prio: piece_copy(s, src, dst, n).start(priority=prio))

    @pl.when(i == 0)
    def _():
        yl[...] = jnp.zeros_like(yl)
        gather(0, 0)

    @pl.when(i + 1 < nt)
    def _():
        gather(i + 1, 1 - slot)

    _wait_run_pieces(i, un_ref, lambda n: piece_copy(slot, 0, 0, n).wait())

    lp = lpos_ref[...]
    route = route_ref[...]
    r = lax.broadcasted_iota(I32, (tm, rows), 1)
    w = jnp.where(r == lp[:, 0:1], route[:, 2:3], 0.0) + jnp.where(r == lp[:, 1:2], route[:, 3:4], 0.0)
    out_ref[...] = h1_ref[...] + jnp.dot(w.astype(BF16), yl[slot].astype(BF16), preferred_element_type=F32)


def _combine(h1, route, lpos, tables, ys, tm):
    T, D = h1.shape
    local_rows = 2 * tm + N_EXPERTS * RUN_ROWS
    grid_spec = pltpu.PrefetchScalarGridSpec(
        num_scalar_prefetch=3,
        grid=(T // tm,),
        in_specs=[pl.BlockSpec((tm, LANES), lambda i, *_: (i, 0)),
                  pl.BlockSpec((tm, LANES), lambda i, *_: (i, 0)),
                  pl.BlockSpec((tm, D), lambda i, *_: (i, 0)),
                  pl.BlockSpec(memory_space=pl.ANY)],
        out_specs=pl.BlockSpec((tm, D), lambda i, *_: (i, 0)),
        scratch_shapes=[pltpu.VMEM((2, local_rows, D), F32), pltpu.SemaphoreType.DMA((2,))],
    )
    return pl.pallas_call(
        _combine_kernel,
        grid_spec=grid_spec,
        out_shape=jax.ShapeDtypeStruct((T, D), F32),
        compiler_params=_params(("arbitrary",)),
        name="combine",
    )(*tables, lpos, route, h1, ys)


def _mixers(x, norm_mix, w_in, q_lat_norm, kv_lat_norm, w_uq, w_ukv, q_norm, k_norm, lb_logits, layer):
    B, S, D = x.shape
    T = B * S
    q, k, v, hq, fz, hi, hg = _inproj(x.reshape(T, D), S, norm_mix, w_in, q_lat_norm, kv_lat_norm,
                                      w_uq, w_ukv, q_norm, k_norm)
    score_bound = (math.sqrt(MLA_QK) * math.log2(math.e) * 1.02) * jnp.max(jnp.abs(q_norm)) * jnp.max(jnp.abs(k_norm))
    a = _attention(q, k.reshape(B, S, -1), v, score_bound.astype(F32))
    lb = jnp.cumsum(jax.nn.softmax(lb_logits.astype(F32), axis=0), axis=0)[layer]
    hw = hq.shape[1]
    o = _hgrn(hq.reshape(B, S, hw), fz.reshape(2, B, S, hw), hi.reshape(B, S, hw), lb.reshape(2, 1, hw))
    return a.reshape(T, -1), o.reshape(2, T, hw), hg


def _moe(h1, n2, route, totals, w_gate, w_up, w_down, tm=512):
    T, D = h1.shape
    tm = min(tm, T)
    nt = T // tm
    lpos, lposT, runs, glob = _plan(route, totals, tm)
    per_run = lambda row: runs[:, row, :N_EXPERTS].reshape(-1)
    tables = (per_run(0), per_run(1), per_run(2))
    tails = (glob[1, :N_EXPERTS] * RUN_ROWS, glob[2, :N_EXPERTS])
    n_rows = -(-(2 * T + N_EXPERTS * RUN_ROWS * nt) // MOE_BLOCK) * MOE_BLOCK + N_EXPERTS * MOE_BLOCK
    nb = n_rows // MOE_BLOCK
    pend = glob[0, :N_EXPERTS] * RUN_ROWS
    block_row0 = jnp.arange(nb, dtype=I32) * MOE_BLOCK
    block_e = jnp.minimum(jnp.sum((pend[None, :] <= block_row0[:, None]).astype(I32), axis=1), N_EXPERTS - 1)
    nused = pend[N_EXPERTS - 1:] // MOE_BLOCK
    xs = _dispatch(n2, lposT, tables + tails, n_rows, tm)
    ys = _experts(xs, block_e, nused, w_gate, w_up, w_down)
    return _combine(h1, route, lpos, tables, ys, tm)


def kernel(x, norm_mix, w_in, q_lat_norm, kv_lat_norm, w_uq, w_ukv, q_norm, k_norm, lb_logits, hg_out_norm,
           w_out, norm_ffn, w_group, b_group, w_router, b_router, w_gate, w_up, w_down):
    B, S, D = x.shape
    h = x
    for l in range(norm_mix.shape[0]):
        h2 = h.reshape(B * S, D)
        a, o, hg = _mixers(h, norm_mix[l], w_in[l], q_lat_norm[l], kv_lat_norm[l], w_uq[l], w_ukv[l],
                           q_norm[l], k_norm[l], lb_logits, l)
        h1, n2, route, totals = _outproj(h2, a, o, hg, hg_out_norm[l], w_out[l], norm_ffn[l], w_group[l],
                                         b_group[l], w_router[l], b_router[l])
        h = _moe(h1, n2, route, totals, w_gate[l], w_up[l], w_down[l]).reshape(B, S, D)
    return h
```

```python
import functools
import math

import jax
import jax.numpy as jnp
from jax import lax
from jax.experimental import pallas as pl
from jax.experimental.pallas import tpu as pltpu

F32 = jnp.float32
BF16 = jnp.bfloat16
I32 = jnp.int32

EPS = 1e-6
LANES = 128
VMEM_LIMIT = 48 * 1024 * 1024

MLA_HEADS = 8
MLA_NOPE = 64
MLA_ROPE = 32
MLA_QK = MLA_NOPE + MLA_ROPE
MLA_V = 64
V_ROWS = LANES
ROPE_THETA = 10000.0
ATTN_MAX_FIXED_SHIFT = 40.0
HG_HEADS = 4
HG_DK = 128
HG_DV = 128
HG_CHUNK = 64
HG_SAFE_EXPONENT = 60.0
N_GROUPS = 4
EXPERTS_PER_GROUP = 8
N_EXPERTS = N_GROUPS * EXPERTS_PER_GROUP
MOE_BLOCK = 512
RUN_ROWS = 8
RUN_LONG_SHIFT = 2
RUN_LONG_UNITS = 1 << RUN_LONG_SHIFT

NT_DIMS = (((1,), (1,)), ((), ()))
TN_DIMS = (((0,), (0,)), ((), ()))


def _params(sem, **kw):
    return pltpu.CompilerParams(dimension_semantics=sem, vmem_limit_bytes=VMEM_LIMIT, **kw)


def _full(shape):
    n = len(shape)
    return pl.BlockSpec(shape, lambda *_: (0,) * n)


def _inproj_kernel(x_ref, g_ref, wlatT_ref, wkr_ref, wh_ref, qlgc_ref, kvlgc_ref, kvlg_ref, wuqT_ref, wuk_ref,
                   wuvT_ref, qngc_ref, kng_ref, vonec_ref, cosT_ref, sinT_ref, cos_ref, sa_ref, sb_ref,
                   qT_out, k_out, vT_out, hq_out, fz_out, hi_out, hg_out, *, q_scale):
    x = x_ref[...]
    ms = jnp.mean(x * x, axis=-1, keepdims=True)
    n = (x * lax.rsqrt(ms + EPS) * g_ref[...]).astype(BF16)

    hw = hq_out.shape[1]
    ql = qlgc_ref.shape[0]
    kvl = kvlgc_ref.shape[0]
    half = MLA_ROPE // 2

    def wide(j):
        return jnp.dot(n, wh_ref[:, j * hw:(j + 1) * hw], preferred_element_type=F32).astype(BF16)

    latT = lax.dot_general(wlatT_ref[...], n, NT_DIMS, preferred_element_type=F32)
    latk = jnp.dot(n, wkr_ref[...], preferred_element_type=F32)
    hq_out[...] = wide(0)
    fz_out[0] = wide(1)
    qlT = latT[0:ql]
    kvlT = latT[ql:ql + kvl]
    qnT = (qlT * lax.rsqrt(jnp.mean(qlT * qlT, axis=0, keepdims=True) + EPS) * qlgc_ref[...]).astype(BF16)
    kvnT = (kvlT * lax.rsqrt(jnp.mean(kvlT * kvlT, axis=0, keepdims=True) + EPS) * kvlgc_ref[...]).astype(BF16)
    q_allT = jnp.dot(wuqT_ref[...], qnT, preferred_element_type=F32)
    vT_out[...] = (jnp.dot(wuvT_ref[...], kvnT, preferred_element_type=F32) + vonec_ref[...]).astype(BF16)
    kvl_r = latk[:, 0:kvl]
    kr = latk[:, kvl:kvl + LANES]
    kvn = (kvl_r * lax.rsqrt(jnp.mean(kvl_r * kvl_r, axis=-1, keepdims=True) + EPS) * kvlg_ref[...]).astype(BF16)
    k_all = jnp.dot(kvn, wuk_ref[...], preferred_element_type=F32)
    fz_out[1] = wide(2)
    hi_out[...] = wide(3)
    hg_out[...] = wide(4)

    cosT = cosT_ref[...]
    sinT = sinT_ref[...]
    qngc = qngc_ref[...]
    for h in range(MLA_HEADS):
        t = q_allT[h * LANES:(h + 1) * LANES]
        tn = t * lax.rsqrt(jnp.sum(t * t, axis=0, keepdims=True) * (1.0 / MLA_QK) + EPS) * qngc
        x1 = tn[MLA_NOPE:MLA_NOPE + half]
        x2 = tn[MLA_NOPE + half:MLA_QK]
        rot = jnp.concatenate([tn[0:MLA_NOPE], x1 * cosT - x2 * sinT, x2 * cosT + x1 * sinT, tn[MLA_QK:LANES]],
                              axis=0)
        qT_out[h * LANES:(h + 1) * LANES, :] = (rot * q_scale).astype(BF16)

    kng = kng_ref[...]
    krg = kr * kng
    up = pltpu.roll(krg, LANES - half, axis=1)
    dn = pltpu.roll(krg, half, axis=1)
    kr_rot = krg * cos_ref[...] + up * sa_ref[...] + dn * sb_ref[...]
    kr_ssq = jnp.sum(kr * kr, axis=-1, keepdims=True)
    for h in range(MLA_HEADS):
        sl = slice(h * LANES, (h + 1) * LANES)
        t = k_all[:, sl]
        r = lax.rsqrt((jnp.sum(t * t, axis=-1, keepdims=True) + kr_ssq) * (1.0 / MLA_QK) + EPS)
        k_out[:, sl] = (r * (t * kng + kr_rot)).astype(BF16)


def _inproj(x2, seq, norm_mix, w_in, q_lat_norm, kv_lat_norm, w_uq, w_ukv, q_norm, k_norm, tm=512):
    T, D = x2.shape
    tm = min(tm, seq)
    H = MLA_HEADS
    ql, kvl = w_uq.shape[0], w_ukv.shape[0]
    hw = HG_HEADS * HG_DK
    o_kr = ql + kvl
    o_h = o_kr + MLA_ROPE
    zeros = lambda c: jnp.zeros((D, c), F32)
    w_latT = w_in[:, :o_kr].T.astype(BF16)
    w_kr = jnp.concatenate([w_in[:, ql:o_kr], zeros(MLA_NOPE), w_in[:, o_kr:o_h],
                            zeros(LANES - MLA_QK)], axis=1).astype(BF16)
    w_h = w_in[:, o_h:].astype(BF16)
    pad = LANES - MLA_QK
    wuqT = jnp.pad(w_uq.reshape(ql, H, MLA_QK), ((0, 0), (0, 0), (0, pad))).reshape(ql, H * LANES).T.astype(BF16)
    wkv = w_ukv.reshape(kvl, H, MLA_NOPE + MLA_V)
    wuk = jnp.pad(wkv[:, :, :MLA_NOPE], ((0, 0), (0, 0), (0, LANES - MLA_NOPE))).reshape(kvl, H * LANES).astype(BF16)
    wuvT = jnp.pad(wkv[:, :, MLA_NOPE:], ((0, 0), (0, 0), (0, V_ROWS - MLA_V))).reshape(kvl, H * V_ROWS).T.astype(BF16)
    vonec = jnp.tile(jnp.concatenate([jnp.zeros((MLA_V,), F32), jnp.ones((V_ROWS - MLA_V,), F32)]), H)
    vonec = vonec.reshape(H * V_ROWS, 1)
    qngc = jnp.pad(q_norm, (0, pad)).reshape(LANES, 1)
    kng = jnp.pad(k_norm, (0, pad)).reshape(1, LANES)

    half = MLA_ROPE // 2
    inv = 1.0 / (ROPE_THETA ** (jnp.arange(half, dtype=F32) / half))
    ang = jnp.arange(seq, dtype=F32)[:, None] * inv[None, :]
    cos, sin = jnp.cos(ang), jnp.sin(ang)
    z = lambda c: jnp.zeros((seq, c), F32)
    cos_t = jnp.concatenate([jnp.ones((seq, MLA_NOPE), F32), cos, cos, z(pad)], axis=1)
    sa_t = jnp.concatenate([z(MLA_NOPE), -sin, z(half), z(pad)], axis=1)
    sb_t = jnp.concatenate([z(MLA_NOPE), z(half), sin, z(pad)], axis=1)

    nseq = seq // tm
    row = lambda w: pl.BlockSpec((tm, w), lambda i: (i, 0))
    col = lambda r: pl.BlockSpec((r, tm), lambda i: (0, i))
    tab = pl.BlockSpec((tm, LANES), lambda i: (i % nseq, 0))
    tabT = pl.BlockSpec((half, tm), lambda i: (0, i % nseq))
    q_scale = (MLA_QK ** -0.5) * math.log2(math.e)
    outs = pl.pallas_call(
        functools.partial(_inproj_kernel, q_scale=q_scale),
        grid=(T // tm,),
        in_specs=[row(D), _full((1, D)), _full(w_latT.shape), _full(w_kr.shape), _full(w_h.shape),
                  _full((ql, 1)), _full((kvl, 1)), _full((1, kvl)), _full(wuqT.shape), _full(wuk.shape),
                  _full(wuvT.shape), _full((LANES, 1)), _full((1, LANES)), _full((H * V_ROWS, 1)),
                  tabT, tabT, tab, tab, tab],
        out_specs=[col(H * LANES), row(H * LANES), col(H * V_ROWS), row(hw),
                   pl.BlockSpec((2, tm, hw), lambda i: (0, i, 0)), row(hw), row(hw)],
        out_shape=[jax.ShapeDtypeStruct((H * LANES, T), BF16), jax.ShapeDtypeStruct((T, H * LANES), BF16),
                   jax.ShapeDtypeStruct((H * V_ROWS, T), BF16), jax.ShapeDtypeStruct((T, hw), BF16),
                   jax.ShapeDtypeStruct((2, T, hw), BF16), jax.ShapeDtypeStruct((T, hw), BF16),
                   jax.ShapeDtypeStruct((T, hw), BF16)],
        compiler_params=_params(("parallel",)),
        name="inproj",
    )(x2, norm_mix.reshape(1, D), w_latT, w_kr, w_h, q_lat_norm.reshape(ql, 1), kv_lat_norm.reshape(kvl, 1),
      kv_lat_norm.reshape(1, kvl), wuqT, wuk, wuvT, qngc, kng, vonec, cos.T, sin.T, cos_t, sa_t, sb_t)
    return outs


def _attn_kernel(qT_ref, k_ref, vT_ref, o_ref, s00, s01, s10, s11, *, tk):
    tq = qT_ref.shape[1]
    nk = k_ref.shape[0] // tk
    qTs = [qT_ref[j * LANES:(j + 1) * LANES, :] for j in range(2)]
    s_bufs = ((s00, s01), (s10, s11))

    def scores(j, slot, c):
        r0 = pl.multiple_of(c * tk, tk)
        sT = jnp.dot(k_ref[pl.ds(r0, tk), j * LANES:(j + 1) * LANES], qTs[j], preferred_element_type=F32)
        s_bufs[j][slot][...] = sT
        return jnp.max(sT, axis=0, keepdims=True)

    def absorb(j, slot, c, m, acc, mx):
        r0 = pl.multiple_of(c * tk, tk)
        m_new = jnp.maximum(m, mx)
        pT = jnp.exp2(s_bufs[j][slot][...] - m_new).astype(BF16)
        pv = jnp.dot(vT_ref[j * V_ROWS:(j + 1) * V_ROWS, pl.ds(r0, tk)], pT, preferred_element_type=F32)
        return m_new, jnp.exp2(m - m_new) * acc + pv

    def step(c, slot, state, prefetch):
        mx_next = [scores(j, 1 - slot, c + 1) if prefetch else state[j][2] for j in range(2)]
        new = []
        for j in range(2):
            m, acc, mx = state[j]
            m, acc = absorb(j, slot, c, m, acc, mx)
            new.append((m, acc, mx_next[j]))
        return tuple(new)

    def pair(i, state):
        c = 2 * i
        return step(c + 1, 1, step(c, 0, state, True), True)

    state = tuple((jnp.full((1, tq), -jnp.inf, F32), jnp.zeros((V_ROWS, tq), F32), scores(j, 0, 0))
                  for j in range(2))
    state = lax.fori_loop(0, nk // 2 - 1, pair, state)
    state = step(nk - 1, 1, step(nk - 2, 0, state, True), False)
    acc0, acc1 = state[0][1], state[1][1]
    oT = jnp.concatenate([acc0[0:MLA_V] / acc0[MLA_V:MLA_V + 1], acc1[0:MLA_V] / acc1[MLA_V:MLA_V + 1]], axis=0)
    o_ref[...] = oT.T.astype(o_ref.dtype)


def _attn_bounded_kernel(shift_ref, qT_ref, k_ref, vT_ref, o_ref, p00, p01, p10, p11, *, tk):
    tq = qT_ref.shape[1]
    nk = k_ref.shape[0] // tk
    shift = shift_ref[0]
    qTs = [qT_ref[j * LANES:(j + 1) * LANES, :] for j in range(2)]
    p_bufs = ((p00, p01), (p10, p11))

    def probs(j, slot, c):
        r0 = pl.multiple_of(c * tk, tk)
        sT = jnp.dot(k_ref[pl.ds(r0, tk), j * LANES:(j + 1) * LANES], qTs[j], preferred_element_type=F32)
        p_bufs[j][slot][...] = jnp.exp2(sT - shift).astype(BF16)

    def absorb(j, slot, c, acc):
        r0 = pl.multiple_of(c * tk, tk)
        return acc + jnp.dot(vT_ref[j * V_ROWS:(j + 1) * V_ROWS, pl.ds(r0, tk)], p_bufs[j][slot][...],
                             preferred_element_type=F32)

    def step(c, slot, accs, prefetch):
        if prefetch:
            for j in range(2):
                probs(j, 1 - slot, c + 1)
        return tuple(absorb(j, slot, c, accs[j]) for j in range(2))

    def pair(i, accs):
        c = 2 * i
        return step(c + 1, 1, step(c, 0, accs, True), True)

    for j in range(2):
        probs(j, 0, 0)
    accs = tuple(jnp.zeros((V_ROWS, tq), F32) for _ in range(2))
    accs = lax.fori_loop(0, nk // 2 - 1, pair, accs)
    acc0, acc1 = step(nk - 1, 1, step(nk - 2, 0, accs, True), False)
    oT = jnp.concatenate([acc0[0:MLA_V] / acc0[MLA_V:MLA_V + 1], acc1[0:MLA_V] / acc1[MLA_V:MLA_V + 1]], axis=0)
    o_ref[...] = oT.T.astype(o_ref.dtype)


def _attention(qT, k3, vT, score_bound, tq=512, tk=512):
    B, S, _ = k3.shape
    tq, tk = min(tq, S), min(tk, S)
    nq = S // tq
    hp = MLA_HEADS // 2
    specs = dict(
        grid=(B, hp, nq),
        in_specs=[pl.BlockSpec((2 * LANES, tq), lambda b, h, i: (h, b * nq + i)),
                  pl.BlockSpec((None, S, 2 * LANES), lambda b, h, i: (b, 0, h)),
                  pl.BlockSpec((2 * V_ROWS, S), lambda b, h, i: (h, b))],
        out_specs=pl.BlockSpec((None, tq, 2 * MLA_V), lambda b, h, i: (b, i, h)),
        out_shape=jax.ShapeDtypeStruct((B, S, MLA_HEADS * MLA_V), BF16),
        compiler_params=_params(("parallel", "parallel", "arbitrary")),
    )

    def bounded():
        tkb = max(min(4 * tk, S // 2), tk)
        in_specs = [pl.BlockSpec(memory_space=pltpu.SMEM)] + specs["in_specs"]
        return pl.pallas_call(functools.partial(_attn_bounded_kernel, tk=tkb), name="attention_bounded",
                              scratch_shapes=[pltpu.VMEM((tkb, tq), BF16)] * 4,
                              **{**specs, "in_specs": in_specs})(score_bound.reshape(1), qT, k3, vT)

    def online():
        return pl.pallas_call(functools.partial(_attn_kernel, tk=tk), name="attention",
                              scratch_shapes=[pltpu.VMEM((tk, tq), F32)] * 4, **specs)(qT, k3, vT)

    return lax.cond(score_bound <= ATTN_MAX_FIXED_SHIFT, bounded, online)


def _hgrn_kernel(hq_ref, z_ref, hi_ref, lb_ref, tri_ref, o_ref, st_ref, kk_scr, b_scr, edge_scr, qh_scr, a_scr,
                 u_scr, qf_scr, of_scr):
    d = pl.program_id(1)
    C = HG_CHUNK
    tl = hq_ref.shape[0]
    nc = tl // C
    heads = [slice(h * HG_DK, (h + 1) * HG_DK) for h in range(HG_HEADS)]

    @pl.when(pl.program_id(2) == 0)
    def _():
        st_ref[...] = jnp.zeros_like(st_ref)

    one_m_lb = 1.0 - lb_ref[...]
    tri = tri_ref[...]
    keep = tri > 0
    fwd = d == 0

    worst = jnp.zeros_like(one_m_lb)
    for c in range(nc):
        rows = slice(c * C, (c + 1) * C)
        kk = one_m_lb * jax.nn.sigmoid(-z_ref[rows, :].astype(F32))
        g = jnp.log(1.0 - kk)
        g_hi = g.astype(BF16)
        g_lo = (g - g_hi.astype(F32)).astype(BF16)
        b = jnp.dot(tri, g_hi, preferred_element_type=F32) + jnp.dot(tri, g_lo, preferred_element_type=F32)
        b_edge = jnp.where(fwd, b[C - 1:C, :], b[0:1, :])
        kk_scr[rows, :] = kk
        b_scr[rows, :] = b
        edge_scr[c] = b_edge
        worst = jnp.maximum(worst, -b_edge)
    safe = jnp.max(worst) < HG_SAFE_EXPONENT

    @pl.when(safe)
    def _():
        for c in range(nc):
            rows = slice(c * C, (c + 1) * C)
            kk = kk_scr[rows, :]
            b = b_scr[rows, :]
            hq = hq_ref[rows, :].astype(F32)
            qh = (hq * jax.nn.sigmoid(hq) * jnp.exp(b)).astype(BF16)
            kt32 = kk * jnp.exp(-b)
            kt = kt32.astype(BF16)
            ks = (kt32 * jnp.exp(edge_scr[c])).astype(BF16)
            v = hi_ref[rows, :]
            qh_scr[rows, :] = qh
            for h, sl in enumerate(heads):
                a = lax.dot_general(qh[:, sl], kt[:, sl], NT_DIMS, preferred_element_type=F32)
                a_scr[c, h] = jnp.where(keep, a, 0.0).astype(BF16)
                u_scr[c, h] = lax.dot_general(v[:, sl], ks[:, sl], TN_DIMS, preferred_element_type=F32)
        st = [st_ref[h] for h in range(HG_HEADS)]
        for p in range(nc):
            c = jnp.where(fwd, p, nc - 1 - p)
            r0 = pl.multiple_of(c * C, C)
            dec = jnp.exp(edge_scr[c])
            qh = qh_scr[pl.ds(r0, C), :]
            v = hi_ref[pl.ds(r0, C), :]
            for h, sl in enumerate(heads):
                o = jnp.dot(a_scr[c, h], v[:, sl], preferred_element_type=F32)
                o = o + lax.dot_general(qh[:, sl], st[h].astype(BF16), NT_DIMS, preferred_element_type=F32)
                o_ref[pl.ds(r0, C), sl] = o.astype(o_ref.dtype)
                st[h] = st[h] * dec[:, sl] + u_scr[c, h]
        for h in range(HG_HEADS):
            st_ref[h] = st[h]

    @pl.when(jnp.logical_not(safe))
    def _():
        hq = hq_ref[...].astype(F32)
        qf_scr[...] = hq * jax.nn.sigmoid(hq)
        first = lax.broadcasted_iota(I32, (16, HG_DK), 0) == 0

        def row(i, carry):
            t = jnp.where(fwd, i, tl - 1 - i)
            kk = kk_scr[pl.ds(t, 1), :]
            f = 1.0 - kk
            q = qf_scr[pl.ds(t, 1), :]
            g0 = pl.multiple_of((t // 16) * 16, 16)
            v = hi_ref[pl.ds(g0, 16), :].astype(F32)
            v = jnp.sum(jnp.where(lax.broadcasted_iota(I32, v.shape, 0) == t % 16, v, 0.0), axis=0, keepdims=True)
            outs = []
            for h, sl in enumerate(heads):
                pad = lambda x: jnp.where(first, jnp.broadcast_to(x[:, sl], (16, HG_DK)), 0.0).astype(BF16)
                st = st_ref[h] * f[:, sl] + lax.dot_general(pad(v), pad(kk), TN_DIMS, preferred_element_type=F32)
                st_ref[h] = st
                outs.append(lax.dot_general(pad(q), st.astype(BF16), NT_DIMS, preferred_element_type=F32)[0:1])
            of_scr[pl.ds(t, 1), :] = jnp.concatenate(outs, axis=1)
            return carry

        lax.fori_loop(0, tl, row, 0)
        o_ref[...] = of_scr[...].astype(o_ref.dtype)


def _hgrn(hq3, fz4, hi3, lb, tl=1024):
    B, S, W = hq3.shape
    tl = min(tl, S)
    nt = S // tl
    C = HG_CHUNK
    r = lax.broadcasted_iota(I32, (C, C), 0)
    c = lax.broadcasted_iota(I32, (C, C), 1)
    tri = jnp.stack([r >= c, r <= c]).astype(BF16)
    tile = lambda b, d, i: (b, i + d * (nt - 1 - 2 * i), 0)
    return pl.pallas_call(
        _hgrn_kernel,
        grid=(B, 2, nt),
        in_specs=[pl.BlockSpec((None, tl, W), tile),
                  pl.BlockSpec((None, None, tl, W), lambda b, d, i: (d, b, i + d * (nt - 1 - 2 * i), 0)),
                  pl.BlockSpec((None, tl, W), tile),
                  pl.BlockSpec((None, 1, W), lambda b, d, i: (d, 0, 0)),
                  pl.BlockSpec((None, C, C), lambda b, d, i: (d, 0, 0))],
        out_specs=pl.BlockSpec((None, None, tl, W), lambda b, d, i: (d, b, i + d * (nt - 1 - 2 * i), 0)),
        out_shape=jax.ShapeDtypeStruct((2, B, S, W), BF16),
        scratch_shapes=[pltpu.VMEM((HG_HEADS, HG_DV, HG_DK), F32),
                        pltpu.VMEM((tl, W), F32), pltpu.VMEM((tl, W), F32),
                        pltpu.VMEM((tl // C, 1, W), F32),
                        pltpu.VMEM((tl, W), BF16),
                        pltpu.VMEM((tl // C, HG_HEADS, C, C), BF16),
                        pltpu.VMEM((tl // C, HG_HEADS, HG_DV, HG_DK), F32),
                        pltpu.VMEM((tl, W), F32), pltpu.VMEM((tl, W), F32)],
        compiler_params=_params(("parallel", "parallel", "arbitrary")),
        name="hgrn",
    )(hq3, fz4, hi3, lb, tri)


def _outproj_kernel(x_ref, a_ref, o_ref, hg_ref, ong_ref, wa_ref, wr_ref, g2_ref, wrt_ref, brt_ref,
                    h1_out, n2_out, route_out, tot_out, tot_scr):
    o = o_ref[0].astype(F32) + o_ref[1].astype(F32)
    hg = hg_ref[...].astype(F32)
    gate = hg * jax.nn.sigmoid(hg)
    ong = ong_ref[...]
    parts = []
    for h in range(HG_HEADS):
        sl = slice(h * HG_DV, (h + 1) * HG_DV)
        oh = o[:, sl]
        parts.append((oh * lax.rsqrt(jnp.mean(oh * oh, axis=-1, keepdims=True) + EPS) * ong * gate[:, sl]).astype(BF16))
    r = jnp.concatenate(parts, axis=1)
    h1 = x_ref[...] + jnp.dot(a_ref[...], wa_ref[...], preferred_element_type=F32)
    h1 = h1 + jnp.dot(r, wr_ref[...], preferred_element_type=F32)
    h1_out[...] = h1
    n2 = h1 * lax.rsqrt(jnp.mean(h1 * h1, axis=-1, keepdims=True) + EPS) * g2_ref[...]
    n2_hi = n2.astype(BF16)
    n2_out[...] = n2_hi
    n2_lo = (n2 - n2_hi.astype(F32)).astype(BF16)
    l_hi = jnp.dot(n2_hi, wrt_ref[...], preferred_element_type=F32)
    l_lo = jnp.dot(n2_lo, wrt_ref[:, 0:LANES], preferred_element_type=F32)
    logits = l_hi[:, 0:LANES] + l_hi[:, LANES:2 * LANES] + l_lo + brt_ref[...]
    tm = logits.shape[0]
    lane = lax.broadcasted_iota(I32, (tm, LANES), 1)
    ninf = -jnp.inf
    is_g = lane < N_GROUPS
    gl = jnp.where(is_g, logits, ninf)
    gmax = jnp.max(gl, axis=-1, keepdims=True)
    gidx = jnp.min(jnp.where(gl == gmax, lane, LANES), axis=-1, keepdims=True)
    g_w = 1.0 / jnp.sum(jnp.where(is_g, jnp.exp(logits - gmax), 0.0), axis=-1, keepdims=True)
    lo = N_GROUPS + EXPERTS_PER_GROUP * gidx
    el = jnp.where((lane >= lo) & (lane < lo + EXPERTS_PER_GROUP), logits, ninf)
    m1 = jnp.max(el, axis=-1, keepdims=True)
    i1 = jnp.min(jnp.where(el == m1, lane, LANES), axis=-1, keepdims=True)
    el2 = jnp.where(lane == i1, ninf, el)
    m2 = jnp.max(el2, axis=-1, keepdims=True)
    i2 = jnp.min(jnp.where(el2 == m2, lane, LANES), axis=-1, keepdims=True)
    t = jnp.exp(m2 - m1)
    w1 = 1.0 / (1.0 + t)
    w2 = t / (1.0 + t)
    e1 = (i1 - N_GROUPS).astype(F32)
    e2 = (i2 - N_GROUPS).astype(F32)
    route = jnp.where(lane == 0, e1, jnp.where(lane == 1, e2, jnp.where(lane == 2, g_w * w1,
                      jnp.where(lane == 3, g_w * w2, 0.0))))
    route_out[...] = route

    @pl.when(pl.program_id(0) == 0)
    def _():
        tot_scr[...] = jnp.zeros_like(tot_scr)

    onehot = jnp.where((lane == i1 - N_GROUPS) | (lane == i2 - N_GROUPS), 1.0, 0.0)
    tot_scr[...] += jnp.ceil(jnp.sum(onehot, axis=0, keepdims=True) * (1.0 / RUN_ROWS))
    tot_out[...] = jnp.broadcast_to(tot_scr[...], tot_out.shape)


def _outproj(x2, a2, o3, hg2, hg_out_norm, w_out, norm_ffn, w_group, b_group, w_router, b_router, tm=512):
    T, D = x2.shape
    tm = min(tm, T)
    wa = w_out[:MLA_HEADS * MLA_V].astype(BF16)
    wr = w_out[MLA_HEADS * MLA_V:].astype(BF16)
    npad = LANES - N_GROUPS - N_EXPERTS
    wrt = jnp.concatenate([w_group, w_router, jnp.zeros((D, npad), F32)], axis=1)
    wrt_hi = wrt.astype(BF16)
    wrt = jnp.concatenate([wrt_hi, (wrt - wrt_hi.astype(F32)).astype(BF16)], axis=1)
    brt = jnp.concatenate([b_group, b_router, jnp.zeros((npad,), F32)]).reshape(1, LANES)
    row = lambda w: pl.BlockSpec((tm, w), lambda i: (i, 0))
    hw = HG_HEADS * HG_DV
    return pl.pallas_call(
        _outproj_kernel,
        grid=(T // tm,),
        in_specs=[row(D), row(a2.shape[1]), pl.BlockSpec((2, tm, hw), lambda i: (0, i, 0)), row(hw),
                  _full((1, HG_DV)), _full(wa.shape), _full(wr.shape), _full((1, D)), _full(wrt.shape),
                  _full((1, LANES))],
        out_specs=[row(D), row(D), row(LANES), _full((8, LANES))],
        out_shape=[jax.ShapeDtypeStruct((T, D), F32), jax.ShapeDtypeStruct((T, D), BF16),
                   jax.ShapeDtypeStruct((T, LANES), F32), jax.ShapeDtypeStruct((8, LANES), F32)],
        scratch_shapes=[pltpu.VMEM((1, LANES), F32)],
        compiler_params=_params(("arbitrary",)),
        name="outproj",
    )(x2, a2, o3, hg2, hg_out_norm.reshape(1, HG_DV), wa, wr, norm_ffn.reshape(1, D), wrt, brt)


def _plan_kernel(route_ref, tot_ref, tri_ref, upper_ref, lpos_out, lposT_out, runs_out, glob_out, base_ref):
    i = pl.program_id(0)
    tm = route_ref.shape[0]
    lane = lax.broadcasted_iota(I32, (tm, LANES), 1)
    route = route_ref[...]
    is1 = lane == route[:, 0:1].astype(I32)
    is2 = lane == route[:, 1:2].astype(I32)
    onehot = jnp.where(is1 | is2, 1.0, 0.0)
    units = jnp.ceil(jnp.sum(onehot, axis=0, keepdims=True) * (1.0 / RUN_ROWS))
    sub = lax.broadcasted_iota(I32, (8, LANES), 0)
    rows3 = lambda a, b, c: jnp.where(sub == 0, a, jnp.where(sub == 1, b, jnp.where(sub == 2, c, 0.0)))

    @pl.when(i == 0)
    def _():
        tot = tot_ref[0:1, :]
        block_units = MOE_BLOCK // RUN_ROWS
        padded = jnp.ceil(tot * (1.0 / block_units)) * block_units
        start = jnp.dot(jnp.broadcast_to(padded, (8, LANES)), upper_ref[...], preferred_element_type=F32,
                        precision=lax.Precision.HIGHEST)[0:1]
        base_ref[...] = start
        glob_out[...] = rows3(start + padded, start + tot, padded - tot).astype(I32)

    before = jnp.dot(tri_ref[...], onehot.astype(BF16), preferred_element_type=F32)
    local = jnp.dot(jnp.broadcast_to(units, (8, LANES)).astype(BF16), upper_ref[...].astype(BF16),
                    preferred_element_type=F32)[0:1] * RUN_ROWS
    pos = local + before
    p1 = jnp.sum(jnp.where(is1, pos, 0.0), axis=-1, keepdims=True)
    p2 = jnp.sum(jnp.where(is2, pos, 0.0), axis=-1, keepdims=True)
    slab = jnp.where(lane == 0, p1, jnp.where(lane == 1, p2, 0.0))
    lpos_out[...] = slab.astype(I32)
    lposT_out[...] = slab.T[0:8].astype(I32)
    runs_out[...] = rows3(local, units, base_ref[...] * RUN_ROWS).astype(I32)
    base_ref[...] += units


def _plan(route, totals, tm):
    T = route.shape[0]
    nt = T // tm
    r = lax.broadcasted_iota(I32, (tm, tm), 0)
    c = lax.broadcasted_iota(I32, (tm, tm), 1)
    tri = (r > c).astype(BF16)
    ru = lax.broadcasted_iota(I32, (LANES, LANES), 0)
    cu = lax.broadcasted_iota(I32, (LANES, LANES), 1)
    upper = (ru < cu).astype(F32)
    return pl.pallas_call(
        _plan_kernel,
        grid=(nt,),
        in_specs=[pl.BlockSpec((tm, LANES), lambda i: (i, 0)), _full((8, LANES)), _full((tm, tm)),
                  _full((LANES, LANES))],
        out_specs=[pl.BlockSpec((tm, LANES), lambda i: (i, 0)),
                   pl.BlockSpec((8, tm), lambda i: (0, i)),
                   pl.BlockSpec((None, 8, LANES), lambda i: (i, 0, 0)),
                   _full((8, LANES))],
        out_shape=[jax.ShapeDtypeStruct((T, LANES), I32), jax.ShapeDtypeStruct((8, T), I32),
                   jax.ShapeDtypeStruct((nt, 8, LANES), I32), jax.ShapeDtypeStruct((8, LANES), I32)],
        scratch_shapes=[pltpu.VMEM((1, LANES), F32)],
        compiler_params=_params(("arbitrary",)),
        name="plan",
    )(route, totals, tri, upper)


def _run_pieces(tile, start_a_ref, units_ref, start_b_ref, fn):
    def per_expert(e, carry):
        j = tile * N_EXPERTS + e
        a0, b0, n = start_a_ref[j], start_b_ref[j], units_ref[j]
        n_long = lax.shift_right_logical(n, RUN_LONG_SHIFT)
        rest0 = n_long * (RUN_LONG_UNITS * RUN_ROWS)

        def long_piece(u, c):
            o = u * (RUN_LONG_UNITS * RUN_ROWS)
            fn(pl.multiple_of(a0 + o, RUN_ROWS), pl.multiple_of(b0 + o, RUN_ROWS), RUN_LONG_UNITS * RUN_ROWS, 0)
            return c

        def unit_piece(u, c):
            o = rest0 + u * RUN_ROWS
            fn(pl.multiple_of(a0 + o, RUN_ROWS), pl.multiple_of(b0 + o, RUN_ROWS), RUN_ROWS, 1)
            return c

        lax.fori_loop(0, n_long, long_piece, 0)
        lax.fori_loop(0, n & (RUN_LONG_UNITS - 1), unit_piece, 0)
        return carry

    lax.fori_loop(0, N_EXPERTS, per_expert, 0)


def _wait_run_pieces(tile, units_ref, wait_fn):
    def count(e, tot):
        n = units_ref[tile * N_EXPERTS + e]
        return tot[0] + lax.shift_right_logical(n, RUN_LONG_SHIFT), tot[1] + (n & (RUN_LONG_UNITS - 1))

    n_long, n_unit = lax.fori_loop(0, N_EXPERTS, count, (jnp.int32(0), jnp.int32(0)))

    def wait_long(u, c):
        wait_fn(RUN_LONG_UNITS * RUN_ROWS)
        return c

    def wait_unit(u, c):
        wait_fn(RUN_ROWS)
        return c

    lax.fori_loop(0, n_long, wait_long, 0)
    lax.fori_loop(0, n_unit, wait_unit, 0)


def _dispatch_kernel(ls_ref, un_ref, gd_ref, ts_ref, tu_ref, lposT_ref, n2_ref, xs_out, xl, sems):
    i = pl.program_id(0)
    nt = pl.num_programs(0)
    rows = xl.shape[1]
    tm = n2_ref.shape[0]
    slot = i % 2
    lp = lposT_ref[...]
    r = lax.broadcasted_iota(I32, (rows, tm), 0)
    pick = jnp.where((r == lp[0:1, :]) | (r == lp[1:2, :]), 1.0, 0.0).astype(BF16)
    xl[slot] = jnp.dot(pick, n2_ref[...], preferred_element_type=F32)

    def piece_copy(s, src, dst, n):
        return pltpu.make_async_copy(xl.at[s, pl.ds(src, n)], xs_out.at[pl.ds(dst, n)], sems.at[s])

    def wait_tile(tile, s):
        _wait_run_pieces(tile, un_ref, lambda n: piece_copy(s, 0, 0, n).wait())

    _run_pieces(i, ls_ref, un_ref, gd_ref,
                lambda src, dst, n, prio: piece_copy(slot, src, dst, n).start(priority=prio))

    @pl.when(i > 0)
    def _():
        wait_tile(i - 1, 1 - slot)

    @pl.when(i == nt - 1)
    def _():
        wait_tile(i, slot)
        xl[0] = jnp.zeros((rows, xl.shape[2]), xl.dtype)
        sem = sems.at[0]

        def zero_copy(dst, n):
            return pltpu.make_async_copy(xl.at[0, pl.ds(0, n)], xs_out.at[pl.ds(dst, n)], sem)

        def per_expert(e, total):
            d0, n = ts_ref[e], tu_ref[e]

            def per_unit(u, carry):
                zero_copy(pl.multiple_of(d0 + u * RUN_ROWS, RUN_ROWS), RUN_ROWS).start()
                return carry

            lax.fori_loop(0, n, per_unit, 0)
            return total + n

        def wait_unit_zero(u, carry):
            zero_copy(0, RUN_ROWS).wait()
            return carry

        lax.fori_loop(0, lax.fori_loop(0, N_EXPERTS, per_expert, 0), wait_unit_zero, 0)

        last = N_EXPERTS - 1
        first_unused = (ts_ref[last] + tu_ref[last] * RUN_ROWS) // MOE_BLOCK
        n_blocks = xs_out.shape[0] // MOE_BLOCK

        def start_block(b, carry):
            zero_copy(pl.multiple_of(b * MOE_BLOCK, MOE_BLOCK), MOE_BLOCK).start()
            return carry

        def wait_block(b, carry):
            zero_copy(0, MOE_BLOCK).wait()
            return carry

        lax.fori_loop(first_unused, n_blocks, start_block, 0)
        lax.fori_loop(first_unused, n_blocks, wait_block, 0)


def _dispatch(n2, lposT, tables, n_rows, tm):
    T, D = n2.shape
    local_rows = 2 * tm + N_EXPERTS * RUN_ROWS
    grid_spec = pltpu.PrefetchScalarGridSpec(
        num_scalar_prefetch=5,
        grid=(T // tm,),
        in_specs=[pl.BlockSpec((8, tm), lambda i, *_: (0, i)),
                  pl.BlockSpec((tm, D), lambda i, *_: (i, 0))],
        out_specs=pl.BlockSpec(memory_space=pl.ANY),
        scratch_shapes=[pltpu.VMEM((2, local_rows, D), F32), pltpu.SemaphoreType.DMA((2,))],
    )
    return pl.pallas_call(
        _dispatch_kernel,
        grid_spec=grid_spec,
        out_shape=jax.ShapeDtypeStruct((n_rows, D), F32),
        compiler_params=_params(("arbitrary",), has_side_effects=True),
        name="dispatch",
    )(*tables, lposT, n2)


def _expert_kernel(be_ref, nused_ref, x_ref, wgu_ref, wd_ref, y_ref):
    i = pl.program_id(0)
    de = wd_ref.shape[0]
    half = x_ref.shape[0] // 2

    @pl.when(i < nused_ref[0])
    def _():
        gu = [jnp.dot(x_ref[r * half:(r + 1) * half, :].astype(BF16), wgu_ref[...], preferred_element_type=F32)
              for r in range(2)]
        for r in range(2):
            g, u = gu[r][:, 0:de], gu[r][:, de:2 * de]
            hmid = (g * jax.nn.sigmoid(g) * u).astype(BF16)
            y_ref[r * half:(r + 1) * half, :] = jnp.dot(hmid, wd_ref[...], preferred_element_type=F32)

    @pl.when(i >= nused_ref[0])
    def _():
        y_ref[...] = jnp.zeros_like(y_ref)


def _experts(xs, block_e, nused, w_gate, w_up, w_down):
    P, D = xs.shape
    nb = P // MOE_BLOCK
    de = w_gate.shape[2]
    grid_spec = pltpu.PrefetchScalarGridSpec(
        num_scalar_prefetch=2,
        grid=(nb,),
        in_specs=[pl.BlockSpec((MOE_BLOCK, D), lambda i, be, nu: (jnp.maximum(jnp.minimum(i, nu[0] - 1), 0), 0)),
                  pl.BlockSpec((None, D, 2 * de), lambda i, be, nu: (be[i], 0, 0)),
                  pl.BlockSpec((None, de, D), lambda i, be, nu: (be[i], 0, 0))],
        out_specs=pl.BlockSpec((MOE_BLOCK, D), lambda i, be, nu: (i, 0)),
    )
    return pl.pallas_call(
        _expert_kernel,
        grid_spec=grid_spec,
        out_shape=jax.ShapeDtypeStruct((P, D), F32),
        compiler_params=_params(("arbitrary",)),
        name="experts",
    )(block_e, nused, xs, jnp.concatenate([w_gate, w_up], axis=2).astype(BF16), w_down.astype(BF16))


def _combine_kernel(ls_ref, un_ref, gd_ref, lpos_ref, route_ref, h1_ref, ys_ref, out_ref, yl, sems):
    i = pl.program_id(0)
    nt = pl.num_programs(0)
    tm = h1_ref.shape[0]
    rows = yl.shape[1]
    slot = i % 2

    def piece_copy(s, src, dst, n):
        return pltpu.make_async_copy(ys_ref.at[pl.ds(src, n)], yl.at[s, pl.ds(dst, n)], sems.at[s])

    def gather(tile, s):
        _run_pieces(tile, gd_ref, un_ref, ls_ref,
                    lambda src, dst, n, prio: piece_copy(s, src, dst, n).start(priority=prio))

    @pl.when(i == 0)
    def _():
        yl[...] = jnp.zeros_like(yl)
        gather(0, 0)

    @pl.when(i + 1 < nt)
    def _():
        gather(i + 1, 1 - slot)

    _wait_run_pieces(i, un_ref, lambda n: piece_copy(slot, 0, 0, n).wait())

    lp = lpos_ref[...]
    route = route_ref[...]
    r = lax.broadcasted_iota(I32, (tm, rows), 1)
    w = jnp.where(r == lp[:, 0:1], route[:, 2:3], 0.0) + jnp.where(r == lp[:, 1:2], route[:, 3:4], 0.0)
    out_ref[...] = h1_ref[...] + jnp.dot(w.astype(BF16), yl[slot].astype(BF16), preferred_element_type=F32)


def _combine(h1, route, lpos, tables, ys, tm):
    T, D = h1.shape
    local_rows = 2 * tm + N_EXPERTS * RUN_ROWS
    grid_spec = pltpu.PrefetchScalarGridSpec(
        num_scalar_prefetch=3,
        grid=(T // tm,),
        in_specs=[pl.BlockSpec((tm, LANES), lambda i, *_: (i, 0)),
                  pl.BlockSpec((tm, LANES), lambda i, *_: (i, 0)),
                  pl.BlockSpec((tm, D), lambda i, *_: (i, 0)),
                  pl.BlockSpec(memory_space=pl.ANY)],
        out_specs=pl.BlockSpec((tm, D), lambda i, *_: (i, 0)),
        scratch_shapes=[pltpu.VMEM((2, local_rows, D), F32), pltpu.SemaphoreType.DMA((2,))],
    )
    return pl.pallas_call(
        _combine_kernel,
        grid_spec=grid_spec,
        out_shape=jax.ShapeDtypeStruct((T, D), F32),
        compiler_params=_params(("arbitrary",)),
        name="combine",
    )(*tables, lpos, route, h1, ys)


def _mixers(x, norm_mix, w_in, q_lat_norm, kv_lat_norm, w_uq, w_ukv, q_norm, k_norm, lb_logits, layer):
    B, S, D = x.shape
    T = B * S
    q, k, v, hq, fz, hi, hg = _inproj(x.reshape(T, D), S, norm_mix, w_in, q_lat_norm, kv_lat_norm,
                                      w_uq, w_ukv, q_norm, k_norm)
    score_bound = (math.sqrt(MLA_QK) * math.log2(math.e) * 1.02) * jnp.max(jnp.abs(q_norm)) * jnp.max(jnp.abs(k_norm))
    a = _attention(q, k.reshape(B, S, -1), v, score_bound.astype(F32))
    lb = jnp.cumsum(jax.nn.softmax(lb_logits.astype(F32), axis=0), axis=0)[layer]
    hw = hq.shape[1]
    o = _hgrn(hq.reshape(B, S, hw), fz.reshape(2, B, S, hw), hi.reshape(B, S, hw), lb.reshape(2, 1, hw))
    return a.reshape(T, -1), o.reshape(2, T, hw), hg


def _moe(h1, n2, route, totals, w_gate, w_up, w_down, tm=512):
    T, D = h1.shape
    tm = min(tm, T)
    nt = T // tm
    lpos, lposT, runs, glob = _plan(route, totals, tm)
    per_run = lambda row: runs[:, row, :N_EXPERTS].reshape(-1)
    tables = (per_run(0), per_run(1), per_run(2))
    tails = (glob[1, :N_EXPERTS] * RUN_ROWS, glob[2, :N_EXPERTS])
    n_rows = -(-(2 * T + N_EXPERTS * RUN_ROWS * nt) // MOE_BLOCK) * MOE_BLOCK + N_EXPERTS * MOE_BLOCK
    nb = n_rows // MOE_BLOCK
    pend = glob[0, :N_EXPERTS] * RUN_ROWS
    block_row0 = jnp.arange(nb, dtype=I32) * MOE_BLOCK
    block_e = jnp.minimum(jnp.sum((pend[None, :] <= block_row0[:, None]).astype(I32), axis=1), N_EXPERTS - 1)
    nused = pend[N_EXPERTS - 1:] // MOE_BLOCK
    xs = _dispatch(n2, lposT, tables + tails, n_rows, tm)
    ys = _experts(xs, block_e, nused, w_gate, w_up, w_down)
    return _combine(h1, route, lpos, tables, ys, tm)


def kernel(x, norm_mix, w_in, q_lat_norm, kv_lat_norm, w_uq, w_ukv, q_norm, k_norm, lb_logits, hg_out_norm,
           w_out, norm_ffn, w_group, b_group, w_router, b_router, w_gate, w_up, w_down):
    B, S, D = x.shape
    h = x
    for l in range(norm_mix.shape[0]):
        h2 = h.reshape(B * S, D)
        a, o, hg = _mixers(h, norm_mix[l], w_in[l], q_lat_norm[l], kv_lat_norm[l], w_uq[l], w_ukv[l],
                           q_norm[l], k_norm[l], lb_logits, l)
        h1, n2, route, totals = _outproj(h2, a, o, hg, hg_out_norm[l], w_out[l], norm_ffn[l], w_group[l],
                                         b_group[l], w_router[l], b_router[l])
        h = _moe(h1, n2, route, totals, w_gate[l], w_up[l], w_down[l]).reshape(B, S, D)
    return h
```

```python
import functools
import math

import jax
import jax.numpy as jnp
from jax import lax
from jax.experimental import pallas as pl
from jax.experimental.pallas import tpu as pltpu

F32 = jnp.float32
BF16 = jnp.bfloat16
I32 = jnp.int32

EPS = 1e-6
LANES = 128
VMEM_LIMIT = 48 * 1024 * 1024

MLA_HEADS = 8
MLA_NOPE = 64
MLA_ROPE = 32
MLA_QK = MLA_NOPE + MLA_ROPE
MLA_V = 64
V_ROWS = LANES
ROPE_THETA = 10000.0
ATTN_MAX_FIXED_SHIFT = 40.0
HG_HEADS = 4
HG_DK = 128
HG_DV = 128
HG_CHUNK = 64
HG_SAFE_EXPONENT = 60.0
N_GROUPS = 4
EXPERTS_PER_GROUP = 8
N_EXPERTS = N_GROUPS * EXPERTS_PER_GROUP
MOE_BLOCK = 512
RUN_ROWS = 8
RUN_LONG_SHIFT = 2
RUN_LONG_UNITS = 1 << RUN_LONG_SHIFT

NT_DIMS = (((1,), (1,)), ((), ()))
TN_DIMS = (((0,), (0,)), ((), ()))


def _params(sem, **kw):
    return pltpu.CompilerParams(dimension_semantics=sem, vmem_limit_bytes=VMEM_LIMIT, **kw)


def _full(shape):
    n = len(shape)
    return pl.BlockSpec(shape, lambda *_: (0,) * n)


def _inproj_kernel(x_ref, g_ref, wlatT_ref, wkr_ref, wh_ref, qlgc_ref, kvlgc_ref, kvlg_ref, wuqT_ref, wuk_ref,
                   wuvT_ref, qngc_ref, kng_ref, vonec_ref, cosT_ref, sinT_ref, cos_ref, sa_ref, sb_ref,
                   qT_out, k_out, vT_out, hq_out, fz_out, hi_out, hg_out, *, q_scale):
    x = x_ref[...]
    ms = jnp.mean(x * x, axis=-1, keepdims=True)
    n = (x * lax.rsqrt(ms + EPS) * g_ref[...]).astype(BF16)

    hw = hq_out.shape[1]
    ql = qlgc_ref.shape[0]
    kvl = kvlgc_ref.shape[0]
    half = MLA_ROPE // 2

    def wide(j):
        return jnp.dot(n, wh_ref[:, j * hw:(j + 1) * hw], preferred_element_type=F32).astype(BF16)

    latT = lax.dot_general(wlatT_ref[...], n, NT_DIMS, preferred_element_type=F32)
    latk = jnp.dot(n, wkr_ref[...], preferred_element_type=F32)
    hq_out[...] = wide(0)
    fz_out[0] = wide(1)
    qlT = latT[0:ql]
    kvlT = latT[ql:ql + kvl]
    qnT = (qlT * lax.rsqrt(jnp.mean(qlT * qlT, axis=0, keepdims=True) + EPS) * qlgc_ref[...]).astype(BF16)
    kvnT = (kvlT * lax.rsqrt(jnp.mean(kvlT * kvlT, axis=0, keepdims=True) + EPS) * kvlgc_ref[...]).astype(BF16)
    q_allT = jnp.dot(wuqT_ref[...], qnT, preferred_element_type=F32)
    vT_out[...] = (jnp.dot(wuvT_ref[...], kvnT, preferred_element_type=F32) + vonec_ref[...]).astype(BF16)
    kvl_r = latk[:, 0:kvl]
    kr = latk[:, kvl:kvl + LANES]
    kvn = (kvl_r * lax.rsqrt(jnp.mean(kvl_r * kvl_r, axis=-1, keepdims=True) + EPS) * kvlg_ref[...]).astype(BF16)
    k_all = jnp.dot(kvn, wuk_ref[...], preferred_element_type=F32)
    fz_out[1] = wide(2)
    hi_out[...] = wide(3)
    hg_out[...] = wide(4)

    cosT = cosT_ref[...]
    sinT = sinT_ref[...]
    qngc = qngc_ref[...]
    for h in range(MLA_HEADS):
        t = q_allT[h * LANES:(h + 1) * LANES]
        tn = t * lax.rsqrt(jnp.sum(t * t, axis=0, keepdims=True) * (1.0 / MLA_QK) + EPS) * qngc
        x1 = tn[MLA_NOPE:MLA_NOPE + half]
        x2 = tn[MLA_NOPE + half:MLA_QK]
        rot = jnp.concatenate([tn[0:MLA_NOPE], x1 * cosT - x2 * sinT, x2 * cosT + x1 * sinT, tn[MLA_QK:LANES]],
                              axis=0)
        qT_out[h * LANES:(h + 1) * LANES, :] = (rot * q_scale).astype(BF16)

    kng = kng_ref[...]
    krg = kr * kng
    up = pltpu.roll(krg, LANES - half, axis=1)
    dn = pltpu.roll(krg, half, axis=1)
    kr_rot = krg * cos_ref[...] + up * sa_ref[...] + dn * sb_ref[...]
    kr_ssq = jnp.sum(kr * kr, axis=-1, keepdims=True)
    for h in range(MLA_HEADS):
        sl = slice(h * LANES, (h + 1) * LANES)
        t = k_all[:, sl]
        r = lax.rsqrt((jnp.sum(t * t, axis=-1, keepdims=True) + kr_ssq) * (1.0 / MLA_QK) + EPS)
        k_out[:, sl] = (r * (t * kng + kr_rot)).astype(BF16)


def _inproj(x2, seq, norm_mix, w_in, q_lat_norm, kv_lat_norm, w_uq, w_ukv, q_norm, k_norm, tm=512):
    T, D = x2.shape
    tm = min(tm, seq)
    H = MLA_HEADS
    ql, kvl = w_uq.shape[0], w_ukv.shape[0]
    hw = HG_HEADS * HG_DK
    o_kr = ql + kvl
    o_h = o_kr + MLA_ROPE
    zeros = lambda c: jnp.zeros((D, c), F32)
    w_latT = w_in[:, :o_kr].T.astype(BF16)
    w_kr = jnp.concatenate([w_in[:, ql:o_kr], zeros(MLA_NOPE), w_in[:, o_kr:o_h],
                            zeros(LANES - MLA_QK)], axis=1).astype(BF16)
    w_h = w_in[:, o_h:].astype(BF16)
    pad = LANES - MLA_QK
    wuqT = jnp.pad(w_uq.reshape(ql, H, MLA_QK), ((0, 0), (0, 0), (0, pad))).reshape(ql, H * LANES).T.astype(BF16)
    wkv = w_ukv.reshape(kvl, H, MLA_NOPE + MLA_V)
    wuk = jnp.pad(wkv[:, :, :MLA_NOPE], ((0, 0), (0, 0), (0, LANES - MLA_NOPE))).reshape(kvl, H * LANES).astype(BF16)
    wuvT = jnp.pad(wkv[:, :, MLA_NOPE:], ((0, 0), (0, 0), (0, V_ROWS - MLA_V))).reshape(kvl, H * V_ROWS).T.astype(BF16)
    vonec = jnp.tile(jnp.concatenate([jnp.zeros((MLA_V,), F32), jnp.ones((V_ROWS - MLA_V,), F32)]), H)
    vonec = vonec.reshape(H * V_ROWS, 1)
    qngc = jnp.pad(q_norm, (0, pad)).reshape(LANES, 1)
    kng = jnp.pad(k_norm, (0, pad)).reshape(1, LANES)

    half = MLA_ROPE // 2
    inv = 1.0 / (ROPE_THETA ** (jnp.arange(half, dtype=F32) / half))
    ang = jnp.arange(seq, dtype=F32)[:, None] * inv[None, :]
    cos, sin = jnp.cos(ang), jnp.sin(ang)
    z = lambda c: jnp.zeros((seq, c), F32)
    cos_t = jnp.concatenate([jnp.ones((seq, MLA_NOPE), F32), cos, cos, z(pad)], axis=1)
    sa_t = jnp.concatenate([z(MLA_NOPE), -sin, z(half), z(pad)], axis=1)
    sb_t = jnp.concatenate([z(MLA_NOPE), z(half), sin, z(pad)], axis=1)

    nseq = seq // tm
    row = lambda w: pl.BlockSpec((tm, w), lambda i: (i, 0))
    col = lambda r: pl.BlockSpec((r, tm), lambda i: (0, i))
    tab = pl.BlockSpec((tm, LANES), lambda i: (i % nseq, 0))
    tabT = pl.BlockSpec((half, tm), lambda i: (0, i % nseq))
    q_scale = (MLA_QK ** -0.5) * math.log2(math.e)
    outs = pl.pallas_call(
        functools.partial(_inproj_kernel, q_scale=q_scale),
        grid=(T // tm,),
        in_specs=[row(D), _full((1, D)), _full(w_latT.shape), _full(w_kr.shape), _full(w_h.shape),
                  _full((ql, 1)), _full((kvl, 1)), _full((1, kvl)), _full(wuqT.shape), _full(wuk.shape),
                  _full(wuvT.shape), _full((LANES, 1)), _full((1, LANES)), _full((H * V_ROWS, 1)),
                  tabT, tabT, tab, tab, tab],
        out_specs=[col(H * LANES), row(H * LANES), col(H * V_ROWS), row(hw),
                   pl.BlockSpec((2, tm, hw), lambda i: (0, i, 0)), row(hw), row(hw)],
        out_shape=[jax.ShapeDtypeStruct((H * LANES, T), BF16), jax.ShapeDtypeStruct((T, H * LANES), BF16),
                   jax.ShapeDtypeStruct((H * V_ROWS, T), BF16), jax.ShapeDtypeStruct((T, hw), BF16),
                   jax.ShapeDtypeStruct((2, T, hw), BF16), jax.ShapeDtypeStruct((T, hw), BF16),
                   jax.ShapeDtypeStruct((T, hw), BF16)],
        compiler_params=_params(("parallel",)),
        name="inproj",
    )(x2, norm_mix.reshape(1, D), w_latT, w_kr, w_h, q_lat_norm.reshape(ql, 1), kv_lat_norm.reshape(kvl, 1),
      kv_lat_norm.reshape(1, kvl), wuqT, wuk, wuvT, qngc, kng, vonec, cos.T, sin.T, cos_t, sa_t, sb_t)
    return outs


def _attn_kernel(qT_ref, k_ref, vT_ref, o_ref, s00, s01, s10, s11, *, tk):
    tq = qT_ref.shape[1]
    nk = k_ref.shape[0] // tk
    qTs = [qT_ref[j * LANES:(j + 1) * LANES, :] for j in range(2)]
    s_bufs = ((s00, s01), (s10, s11))

    def scores(j, slot, c):
        r0 = pl.multiple_of(c * tk, tk)
        sT = jnp.dot(k_ref[pl.ds(r0, tk), j * LANES:(j + 1) * LANES], qTs[j], preferred_element_type=F32)
        s_bufs[j][slot][...] = sT
        return jnp.max(sT, axis=0, keepdims=True)

    def absorb(j, slot, c, m, acc, mx):
        r0 = pl.multiple_of(c * tk, tk)
        m_new = jnp.maximum(m, mx)
        pT = jnp.exp2(s_bufs[j][slot][...] - m_new).astype(BF16)
        pv = jnp.dot(vT_ref[j * V_ROWS:(j + 1) * V_ROWS, pl.ds(r0, tk)], pT, preferred_element_type=F32)
        return m_new, jnp.exp2(m - m_new) * acc + pv

    def step(c, slot, state, prefetch):
        mx_next = [scores(j, 1 - slot, c + 1) if prefetch else state[j][2] for j in range(2)]
        new = []
        for j in range(2):
            m, acc, mx = state[j]
            m, acc = absorb(j, slot, c, m, acc, mx)
            new.append((m, acc, mx_next[j]))
        return tuple(new)

    def pair(i, state):
        c = 2 * i
        return step(c + 1, 1, step(c, 0, state, True), True)

    state = tuple((jnp.full((1, tq), -jnp.inf, F32), jnp.zeros((V_ROWS, tq), F32), scores(j, 0, 0))
                  for j in range(2))
    state = lax.fori_loop(0, nk // 2 - 1, pair, state)
    state = step(nk - 1, 1, step(nk - 2, 0, state, True), False)
    acc0, acc1 = state[0][1], state[1][1]
    oT = jnp.concatenate([acc0[0:MLA_V] / acc0[MLA_V:MLA_V + 1], acc1[0:MLA_V] / acc1[MLA_V:MLA_V + 1]], axis=0)
    o_ref[...] = oT.T.astype(o_ref.dtype)


def _attn_bounded_kernel(shift_ref, qT_ref, k_ref, vT_ref, o_ref, p00, p01, p10, p11, *, tk):
    tq = qT_ref.shape[1]
    nk = k_ref.shape[0] // tk
    shift = shift_ref[0]
    qTs = [qT_ref[j * LANES:(j + 1) * LANES, :] for j in range(2)]
    p_bufs = ((p00, p01), (p10, p11))

    def probs(j, slot, c):
        r0 = pl.multiple_of(c * tk, tk)
        sT = jnp.dot(k_ref[pl.ds(r0, tk), j * LANES:(j + 1) * LANES], qTs[j], preferred_element_type=F32)
        p_bufs[j][slot][...] = jnp.exp2(sT - shift).astype(BF16)

    def absorb(j, slot, c, acc):
        r0 = pl.multiple_of(c * tk, tk)
        return acc + jnp.dot(vT_ref[j * V_ROWS:(j + 1) * V_ROWS, pl.ds(r0, tk)], p_bufs[j][slot][...],
                             preferred_element_type=F32)

    def step(c, slot, accs, prefetch):
        if prefetch:
            for j in range(2):
                probs(j, 1 - slot, c + 1)
        return tuple(absorb(j, slot, c, accs[j]) for j in range(2))

    def pair(i, accs):
        c = 2 * i
        return step(c + 1, 1, step(c, 0, accs, True), True)

    for j in range(2):
        probs(j, 0, 0)
    accs = tuple(jnp.zeros((V_ROWS, tq), F32) for _ in range(2))
    accs = lax.fori_loop(0, nk // 2 - 1, pair, accs)
    acc0, acc1 = step(nk - 1, 1, step(nk - 2, 0, accs, True), False)
    oT = jnp.concatenate([acc0[0:MLA_V] / acc0[MLA_V:MLA_V + 1], acc1[0:MLA_V] / acc1[MLA_V:MLA_V + 1]], axis=0)
    o_ref[...] = oT.T.astype(o_ref.dtype)


def _attention(qT, k3, vT, score_bound, tq=512, tk=512):
    B, S, _ = k3.shape
    tq, tk = min(tq, S), min(tk, S)
    nq = S // tq
    hp = MLA_HEADS // 2
    specs = dict(
        grid=(B, hp, nq),
        in_specs=[pl.BlockSpec((2 * LANES, tq), lambda b, h, i: (h, b * nq + i)),
                  pl.BlockSpec((None, S, 2 * LANES), lambda b, h, i: (b, 0, h)),
                  pl.BlockSpec((2 * V_ROWS, S), lambda b, h, i: (h, b))],
        out_specs=pl.BlockSpec((None, tq, 2 * MLA_V), lambda b, h, i: (b, i, h)),
        out_shape=jax.ShapeDtypeStruct((B, S, MLA_HEADS * MLA_V), BF16),
        compiler_params=_params(("parallel", "parallel", "arbitrary")),
    )

    def bounded():
        tkb = max(min(4 * tk, S // 2), tk)
        in_specs = [pl.BlockSpec(memory_space=pltpu.SMEM)] + specs["in_specs"]
        return pl.pallas_call(functools.partial(_attn_bounded_kernel, tk=tkb), name="attention_bounded",
                              scratch_shapes=[pltpu.VMEM((tkb, tq), BF16)] * 4,
                              **{**specs, "in_specs": in_specs})(score_bound.reshape(1), qT, k3, vT)

    def online():
        return pl.pallas_call(functools.partial(_attn_kernel, tk=tk), name="attention",
                              scratch_shapes=[pltpu.VMEM((tk, tq), F32)] * 4, **specs)(qT, k3, vT)

    return lax.cond(score_bound <= ATTN_MAX_FIXED_SHIFT, bounded, online)


def _hgrn_kernel(hq_ref, z_ref, hi_ref, lb_ref, tri_ref, o_ref, st_ref, kk_scr, b_scr, edge_scr, qh_scr, a_scr,
                 u_scr, qf_scr, of_scr):
    d = pl.program_id(1)
    C = HG_CHUNK
    tl = hq_ref.shape[0]
    nc = tl // C
    heads = [slice(h * HG_DK, (h + 1) * HG_DK) for h in range(HG_HEADS)]

    @pl.when(pl.program_id(2) == 0)
    def _():
        st_ref[...] = jnp.zeros_like(st_ref)

    one_m_lb = 1.0 - lb_ref[...]
    tri = tri_ref[...]
    keep = tri > 0
    fwd = d == 0

    worst = jnp.zeros_like(one_m_lb)
    for c in range(nc):
        rows = slice(c * C, (c + 1) * C)
        kk = one_m_lb * jax.nn.sigmoid(-z_ref[rows, :].astype(F32))
        g = jnp.log(1.0 - kk)
        g_hi = g.astype(BF16)
        g_lo = (g - g_hi.astype(F32)).astype(BF16)
        b = jnp.dot(tri, g_hi, preferred_element_type=F32) + jnp.dot(tri, g_lo, preferred_element_type=F32)
        b_edge = jnp.where(fwd, b[C - 1:C, :], b[0:1, :])
        kk_scr[rows, :] = kk
        b_scr[rows, :] = b
        edge_scr[c] = b_edge
        worst = jnp.maximum(worst, -b_edge)
    safe = jnp.max(worst) < HG_SAFE_EXPONENT

    @pl.when(safe)
    def _():
        for c in range(nc):
            rows = slice(c * C, (c + 1) * C)
            kk = kk_scr[rows, :]
            b = b_scr[rows, :]
            hq = hq_ref[rows, :].astype(F32)
            qh = (hq * jax.nn.sigmoid(hq) * jnp.exp(b)).astype(BF16)
            kt32 = kk * jnp.exp(-b)
            kt = kt32.astype(BF16)
            ks = (kt32 * jnp.exp(edge_scr[c])).astype(BF16)
            v = hi_ref[rows, :]
            qh_scr[rows, :] = qh
            for h, sl in enumerate(heads):
                a = lax.dot_general(qh[:, sl], kt[:, sl], NT_DIMS, preferred_element_type=F32)
                a_scr[c, h] = jnp.where(keep, a, 0.0).astype(BF16)
                u_scr[c, h] = lax.dot_general(v[:, sl], ks[:, sl], TN_DIMS, preferred_element_type=F32)
        st = [st_ref[h] for h in range(HG_HEADS)]
        for p in range(nc):
            c = jnp.where(fwd, p, nc - 1 - p)
            r0 = pl.multiple_of(c * C, C)
            dec = jnp.exp(edge_scr[c])
            qh = qh_scr[pl.ds(r0, C), :]
            v = hi_ref[pl.ds(r0, C), :]
            for h, sl in enumerate(heads):
                o = jnp.dot(a_scr[c, h], v[:, sl], preferred_element_type=F32)
                o = o + lax.dot_general(qh[:, sl], st[h].astype(BF16), NT_DIMS, preferred_element_type=F32)
                o_ref[pl.ds(r0, C), sl] = o.astype(o_ref.dtype)
                st[h] = st[h] * dec[:, sl] + u_scr[c, h]
        for h in range(HG_HEADS):
            st_ref[h] = st[h]

    @pl.when(jnp.logical_not(safe))
    def _():
        hq = hq_ref[...].astype(F32)
        qf_scr[...] = hq * jax.nn.sigmoid(hq)
        first = lax.broadcasted_iota(I32, (16, HG_DK), 0) == 0

        def row(i, carry):
            t = jnp.where(fwd, i, tl - 1 - i)
            kk = kk_scr[pl.ds(t, 1), :]
            f = 1.0 - kk
            q = qf_scr[pl.ds(t, 1), :]
            g0 = pl.multiple_of((t // 16) * 16, 16)
            v = hi_ref[pl.ds(g0, 16), :].astype(F32)
            v = jnp.sum(jnp.where(lax.broadcasted_iota(I32, v.shape, 0) == t % 16, v, 0.0), axis=0, keepdims=True)
            outs = []
            for h, sl in enumerate(heads):
                pad = lambda x: jnp.where(first, jnp.broadcast_to(x[:, sl], (16, HG_DK)), 0.0).astype(BF16)
                st = st_ref[h] * f[:, sl] + lax.dot_general(pad(v), pad(kk), TN_DIMS, preferred_element_type=F32)
                st_ref[h] = st
                outs.append(lax.dot_general(pad(q), st.astype(BF16), NT_DIMS, preferred_element_type=F32)[0:1])
            of_scr[pl.ds(t, 1), :] = jnp.concatenate(outs, axis=1)
            return carry

        lax.fori_loop(0, tl, row, 0)
        o_ref[...] = of_scr[...].astype(o_ref.dtype)


def _hgrn(hq3, fz4, hi3, lb, tl=2048):
    B, S, W = hq3.shape
    tl = min(tl, S)
    nt = S // tl
    C = HG_CHUNK
    r = lax.broadcasted_iota(I32, (C, C), 0)
    c = lax.broadcasted_iota(I32, (C, C), 1)
    tri = jnp.stack([r >= c, r <= c]).astype(BF16)
    tile = lambda b, d, i: (b, i + d * (nt - 1 - 2 * i), 0)
    return pl.pallas_call(
        _hgrn_kernel,
        grid=(B, 2, nt),
        in_specs=[pl.BlockSpec((None, tl, W), tile),
                  pl.BlockSpec((None, None, tl, W), lambda b, d, i: (d, b, i + d * (nt - 1 - 2 * i), 0)),
                  pl.BlockSpec((None, tl, W), tile),
                  pl.BlockSpec((None, 1, W), lambda b, d, i: (d, 0, 0)),
                  pl.BlockSpec((None, C, C), lambda b, d, i: (d, 0, 0))],
        out_specs=pl.BlockSpec((None, None, tl, W), lambda b, d, i: (d, b, i + d * (nt - 1 - 2 * i), 0)),
        out_shape=jax.ShapeDtypeStruct((2, B, S, W), BF16),
        scratch_shapes=[pltpu.VMEM((HG_HEADS, HG_DV, HG_DK), F32),
                        pltpu.VMEM((tl, W), F32), pltpu.VMEM((tl, W), F32),
                        pltpu.VMEM((tl // C, 1, W), F32),
                        pltpu.VMEM((tl, W), BF16),
                        pltpu.VMEM((tl // C, HG_HEADS, C, C), BF16),
                        pltpu.VMEM((tl // C, HG_HEADS, HG_DV, HG_DK), F32),
                        pltpu.VMEM((tl, W), F32), pltpu.VMEM((tl, W), F32)],
        compiler_params=_params(("parallel", "parallel", "arbitrary")),
        name="hgrn",
    )(hq3, fz4, hi3, lb, tri)


def _outproj_kernel(x_ref, a_ref, o_ref, hg_ref, ong_ref, wa_ref, wr_ref, g2_ref, wrt_ref, brt_ref,
                    h1_out, n2_out, route_out, tot_out, tot_scr):
    o = o_ref[0].astype(F32) + o_ref[1].astype(F32)
    hg = hg_ref[...].astype(F32)
    gate = hg * jax.nn.sigmoid(hg)
    ong = ong_ref[...]
    parts = []
    for h in range(HG_HEADS):
        sl = slice(h * HG_DV, (h + 1) * HG_DV)
        oh = o[:, sl]
        parts.append((oh * lax.rsqrt(jnp.mean(oh * oh, axis=-1, keepdims=True) + EPS) * ong * gate[:, sl]).astype(BF16))
    r = jnp.concatenate(parts, axis=1)
    h1 = x_ref[...] + jnp.dot(a_ref[...], wa_ref[...], preferred_element_type=F32)
    h1 = h1 + jnp.dot(r, wr_ref[...], preferred_element_type=F32)
    h1_out[...] = h1
    n2 = h1 * lax.rsqrt(jnp.mean(h1 * h1, axis=-1, keepdims=True) + EPS) * g2_ref[...]
    n2_hi = n2.astype(BF16)
    n2_out[...] = n2_hi
    n2_lo = (n2 - n2_hi.astype(F32)).astype(BF16)
    l_hi = jnp.dot(n2_hi, wrt_ref[...], preferred_element_type=F32)
    l_lo = jnp.dot(n2_lo, wrt_ref[:, 0:LANES], preferred_element_type=F32)
    logits = l_hi[:, 0:LANES] + l_hi[:, LANES:2 * LANES] + l_lo + brt_ref[...]
    tm = logits.shape[0]
    lane = lax.broadcasted_iota(I32, (tm, LANES), 1)
    ninf = -jnp.inf
    is_g = lane < N_GROUPS
    gl = jnp.where(is_g, logits, ninf)
    gmax = jnp.max(gl, axis=-1, keepdims=True)
    gidx = jnp.min(jnp.where(gl == gmax, lane, LANES), axis=-1, keepdims=True)
    g_w = 1.0 / jnp.sum(jnp.where(is_g, jnp.exp(logits - gmax), 0.0), axis=-1, keepdims=True)
    lo = N_GROUPS + EXPERTS_PER_GROUP * gidx
    el = jnp.where((lane >= lo) & (lane < lo + EXPERTS_PER_GROUP), logits, ninf)
    m1 = jnp.max(el, axis=-1, keepdims=True)
    i1 = jnp.min(jnp.where(el == m1, lane, LANES), axis=-1, keepdims=True)
    el2 = jnp.where(lane == i1, ninf, el)
    m2 = jnp.max(el2, axis=-1, keepdims=True)
    i2 = jnp.min(jnp.where(el2 == m2, lane, LANES), axis=-1, keepdims=True)
    t = jnp.exp(m2 - m1)
    w1 = 1.0 / (1.0 + t)
    w2 = t / (1.0 + t)
    e1 = (i1 - N_GROUPS).astype(F32)
    e2 = (i2 - N_GROUPS).astype(F32)
    route = jnp.where(lane == 0, e1, jnp.where(lane == 1, e2, jnp.where(lane == 2, g_w * w1,
                      jnp.where(lane == 3, g_w * w2, 0.0))))
    route_out[...] = route

    @pl.when(pl.program_id(0) == 0)
    def _():
        tot_scr[...] = jnp.zeros_like(tot_scr)

    onehot = jnp.where((lane == i1 - N_GROUPS) | (lane == i2 - N_GROUPS), 1.0, 0.0)
    tot_scr[...] += jnp.ceil(jnp.sum(onehot, axis=0, keepdims=True) * (1.0 / RUN_ROWS))
    tot_out[...] = jnp.broadcast_to(tot_scr[...], tot_out.shape)


def _outproj(x2, a2, o3, hg2, hg_out_norm, w_out, norm_ffn, w_group, b_group, w_router, b_router, tm=512):
    T, D = x2.shape
    tm = min(tm, T)
    wa = w_out[:MLA_HEADS * MLA_V].astype(BF16)
    wr = w_out[MLA_HEADS * MLA_V:].astype(BF16)
    npad = LANES - N_GROUPS - N_EXPERTS
    wrt = jnp.concatenate([w_group, w_router, jnp.zeros((D, npad), F32)], axis=1)
    wrt_hi = wrt.astype(BF16)
    wrt = jnp.concatenate([wrt_hi, (wrt - wrt_hi.astype(F32)).astype(BF16)], axis=1)
    brt = jnp.concatenate([b_group, b_router, jnp.zeros((npad,), F32)]).reshape(1, LANES)
    row = lambda w: pl.BlockSpec((tm, w), lambda i: (i, 0))
    hw = HG_HEADS * HG_DV
    return pl.pallas_call(
        _outproj_kernel,
        grid=(T // tm,),
        in_specs=[row(D), row(a2.shape[1]), pl.BlockSpec((2, tm, hw), lambda i: (0, i, 0)), row(hw),
                  _full((1, HG_DV)), _full(wa.shape), _full(wr.shape), _full((1, D)), _full(wrt.shape),
                  _full((1, LANES))],
        out_specs=[row(D), row(D), row(LANES), _full((8, LANES))],
        out_shape=[jax.ShapeDtypeStruct((T, D), F32), jax.ShapeDtypeStruct((T, D), BF16),
                   jax.ShapeDtypeStruct((T, LANES), F32), jax.ShapeDtypeStruct((8, LANES), F32)],
        scratch_shapes=[pltpu.VMEM((1, LANES), F32)],
        compiler_params=_params(("arbitrary",)),
        name="outproj",
    )(x2, a2, o3, hg2, hg_out_norm.reshape(1, HG_DV), wa, wr, norm_ffn.reshape(1, D), wrt, brt)


def _plan_kernel(route_ref, tot_ref, tri_ref, upper_ref, lpos_out, lposT_out, runs_out, glob_out, base_ref):
    i = pl.program_id(0)
    tm = route_ref.shape[0]
    lane = lax.broadcasted_iota(I32, (tm, LANES), 1)
    route = route_ref[...]
    is1 = lane == route[:, 0:1].astype(I32)
    is2 = lane == route[:, 1:2].astype(I32)
    onehot = jnp.where(is1 | is2, 1.0, 0.0)
    units = jnp.ceil(jnp.sum(onehot, axis=0, keepdims=True) * (1.0 / RUN_ROWS))
    sub = lax.broadcasted_iota(I32, (8, LANES), 0)
    rows3 = lambda a, b, c: jnp.where(sub == 0, a, jnp.where(sub == 1, b, jnp.where(sub == 2, c, 0.0)))

    @pl.when(i == 0)
    def _():
        tot = tot_ref[0:1, :]
        block_units = MOE_BLOCK // RUN_ROWS
        padded = jnp.ceil(tot * (1.0 / block_units)) * block_units
        start = jnp.dot(jnp.broadcast_to(padded, (8, LANES)), upper_ref[...], preferred_element_type=F32,
                        precision=lax.Precision.HIGHEST)[0:1]
        base_ref[...] = start
        glob_out[...] = rows3(start + padded, start + tot, padded - tot).astype(I32)

    before = jnp.dot(tri_ref[...], onehot.astype(BF16), preferred_element_type=F32)
    local = jnp.dot(jnp.broadcast_to(units, (8, LANES)).astype(BF16), upper_ref[...].astype(BF16),
                    preferred_element_type=F32)[0:1] * RUN_ROWS
    pos = local + before
    p1 = jnp.sum(jnp.where(is1, pos, 0.0), axis=-1, keepdims=True)
    p2 = jnp.sum(jnp.where(is2, pos, 0.0), axis=-1, keepdims=True)
    slab = jnp.where(lane == 0, p1, jnp.where(lane == 1, p2, 0.0))
    lpos_out[...] = slab.astype(I32)
    lposT_out[...] = slab.T[0:8].astype(I32)
    runs_out[...] = rows3(local, units, base_ref[...] * RUN_ROWS).astype(I32)
    base_ref[...] += units


def _plan(route, totals, tm):
    T = route.shape[0]
    nt = T // tm
    r = lax.broadcasted_iota(I32, (tm, tm), 0)
    c = lax.broadcasted_iota(I32, (tm, tm), 1)
    tri = (r > c).astype(BF16)
    ru = lax.broadcasted_iota(I32, (LANES, LANES), 0)
    cu = lax.broadcasted_iota(I32, (LANES, LANES), 1)
    upper = (ru < cu).astype(F32)
    return pl.pallas_call(
        _plan_kernel,
        grid=(nt,),
        in_specs=[pl.BlockSpec((tm, LANES), lambda i: (i, 0)), _full((8, LANES)), _full((tm, tm)),
                  _full((LANES, LANES))],
        out_specs=[pl.BlockSpec((tm, LANES), lambda i: (i, 0)),
                   pl.BlockSpec((8, tm), lambda i: (0, i)),
                   pl.BlockSpec((None, 8, LANES), lambda i: (i, 0, 0)),
                   _full((8, LANES))],
        out_shape=[jax.ShapeDtypeStruct((T, LANES), I32), jax.ShapeDtypeStruct((8, T), I32),
                   jax.ShapeDtypeStruct((nt, 8, LANES), I32), jax.ShapeDtypeStruct((8, LANES), I32)],
        scratch_shapes=[pltpu.VMEM((1, LANES), F32)],
        compiler_params=_params(("arbitrary",)),
        name="plan",
    )(route, totals, tri, upper)


def _run_pieces(tile, start_a_ref, units_ref, start_b_ref, fn):
    def per_expert(e, carry):
        j = tile * N_EXPERTS + e
        a0, b0, n = start_a_ref[j], start_b_ref[j], units_ref[j]
        n_long = lax.shift_right_logical(n, RUN_LONG_SHIFT)
        rest0 = n_long * (RUN_LONG_UNITS * RUN_ROWS)

        def long_piece(u, c):
            o = u * (RUN_LONG_UNITS * RUN_ROWS)
            fn(pl.multiple_of(a0 + o, RUN_ROWS), pl.multiple_of(b0 + o, RUN_ROWS), RUN_LONG_UNITS * RUN_ROWS, 0)
            return c

        def unit_piece(u, c):
            o = rest0 + u * RUN_ROWS
            fn(pl.multiple_of(a0 + o, RUN_ROWS), pl.multiple_of(b0 + o, RUN_ROWS), RUN_ROWS, 1)
            return c

        lax.fori_loop(0, n_long, long_piece, 0)
        lax.fori_loop(0, n & (RUN_LONG_UNITS - 1), unit_piece, 0)
        return carry

    lax.fori_loop(0, N_EXPERTS, per_expert, 0)


def _wait_run_pieces(tile, units_ref, wait_fn):
    def count(e, tot):
        n = units_ref[tile * N_EXPERTS + e]
        return tot[0] + lax.shift_right_logical(n, RUN_LONG_SHIFT), tot[1] + (n & (RUN_LONG_UNITS - 1))

    n_long, n_unit = lax.fori_loop(0, N_EXPERTS, count, (jnp.int32(0), jnp.int32(0)))

    def wait_long(u, c):
        wait_fn(RUN_LONG_UNITS * RUN_ROWS)
        return c

    def wait_unit(u, c):
        wait_fn(RUN_ROWS)
        return c

    lax.fori_loop(0, n_long, wait_long, 0)
    lax.fori_loop(0, n_unit, wait_unit, 0)


def _dispatch_kernel(ls_ref, un_ref, gd_ref, ts_ref, tu_ref, lposT_ref, n2_ref, xs_out, xl, sems):
    i = pl.program_id(0)
    nt = pl.num_programs(0)
    rows = xl.shape[1]
    tm = n2_ref.shape[0]
    slot = i % 2
    lp = lposT_ref[...]
    r = lax.broadcasted_iota(I32, (rows, tm), 0)
    pick = jnp.where((r == lp[0:1, :]) | (r == lp[1:2, :]), 1.0, 0.0).astype(BF16)
    xl[slot] = jnp.dot(pick, n2_ref[...], preferred_element_type=F32)

    def piece_copy(s, src, dst, n):
        return pltpu.make_async_copy(xl.at[s, pl.ds(src, n)], xs_out.at[pl.ds(dst, n)], sems.at[s])

    def wait_tile(tile, s):
        _wait_run_pieces(tile, un_ref, lambda n: piece_copy(s, 0, 0, n).wait())

    _run_pieces(i, ls_ref, un_ref, gd_ref,
                lambda src, dst, n, prio: piece_copy(slot, src, dst, n).start(priority=prio))

    @pl.when(i > 0)
    def _():
        wait_tile(i - 1, 1 - slot)

    @pl.when(i == nt - 1)
    def _():
        wait_tile(i, slot)
        xl[0] = jnp.zeros((rows, xl.shape[2]), xl.dtype)
        sem = sems.at[0]

        def zero_copy(dst, n):
            return pltpu.make_async_copy(xl.at[0, pl.ds(0, n)], xs_out.at[pl.ds(dst, n)], sem)

        def per_expert(e, total):
            d0, n = ts_ref[e], tu_ref[e]

            def per_unit(u, carry):
                zero_copy(pl.multiple_of(d0 + u * RUN_ROWS, RUN_ROWS), RUN_ROWS).start()
                return carry

            lax.fori_loop(0, n, per_unit, 0)
            return total + n

        def wait_unit_zero(u, carry):
            zero_copy(0, RUN_ROWS).wait()
            return carry

        lax.fori_loop(0, lax.fori_loop(0, N_EXPERTS, per_expert, 0), wait_unit_zero, 0)

        last = N_EXPERTS - 1
        first_unused = (ts_ref[last] + tu_ref[last] * RUN_ROWS) // MOE_BLOCK
        n_blocks = xs_out.shape[0] // MOE_BLOCK

        def start_block(b, carry):
            zero_copy(pl.multiple_of(b * MOE_BLOCK, MOE_BLOCK), MOE_BLOCK).start()
            return carry

        def wait_block(b, carry):
            zero_copy(0, MOE_BLOCK).wait()
            return carry

        lax.fori_loop(first_unused, n_blocks, start_block, 0)
        lax.fori_loop(first_unused, n_blocks, wait_block, 0)


def _dispatch(n2, lposT, tables, n_rows, tm):
    T, D = n2.shape
    local_rows = 2 * tm + N_EXPERTS * RUN_ROWS
    grid_spec = pltpu.PrefetchScalarGridSpec(
        num_scalar_prefetch=5,
        grid=(T // tm,),
        in_specs=[pl.BlockSpec((8, tm), lambda i, *_: (0, i)),
                  pl.BlockSpec((tm, D), lambda i, *_: (i, 0))],
        out_specs=pl.BlockSpec(memory_space=pl.ANY),
        scratch_shapes=[pltpu.VMEM((2, local_rows, D), F32), pltpu.SemaphoreType.DMA((2,))],
    )
    return pl.pallas_call(
        _dispatch_kernel,
        grid_spec=grid_spec,
        out_shape=jax.ShapeDtypeStruct((n_rows, D), F32),
        compiler_params=_params(("arbitrary",), has_side_effects=True),
        name="dispatch",
    )(*tables, lposT, n2)


def _expert_kernel(be_ref, nused_ref, x_ref, wgu_ref, wd_ref, y_ref):
    i = pl.program_id(0)
    de = wd_ref.shape[0]
    half = x_ref.shape[0] // 2

    @pl.when(i < nused_ref[0])
    def _():
        gu = [jnp.dot(x_ref[r * half:(r + 1) * half, :].astype(BF16), wgu_ref[...], preferred_element_type=F32)
              for r in range(2)]
        for r in range(2):
            g, u = gu[r][:, 0:de], gu[r][:, de:2 * de]
            hmid = (g * jax.nn.sigmoid(g) * u).astype(BF16)
            y_ref[r * half:(r + 1) * half, :] = jnp.dot(hmid, wd_ref[...], preferred_element_type=F32)

    @pl.when(i >= nused_ref[0])
    def _():
        y_ref[...] = jnp.zeros_like(y_ref)


def _experts(xs, block_e, nused, w_gate, w_up, w_down):
    P, D = xs.shape
    nb = P // MOE_BLOCK
    de = w_gate.shape[2]
    grid_spec = pltpu.PrefetchScalarGridSpec(
        num_scalar_prefetch=2,
        grid=(nb,),
        in_specs=[pl.BlockSpec((MOE_BLOCK, D), lambda i, be, nu: (jnp.maximum(jnp.minimum(i, nu[0] - 1), 0), 0)),
                  pl.BlockSpec((None, D, 2 * de), lambda i, be, nu: (be[i], 0, 0)),
                  pl.BlockSpec((None, de, D), lambda i, be, nu: (be[i], 0, 0))],
        out_specs=pl.BlockSpec((MOE_BLOCK, D), lambda i, be, nu: (i, 0)),
    )
    return pl.pallas_call(
        _expert_kernel,
        grid_spec=grid_spec,
        out_shape=jax.ShapeDtypeStruct((P, D), F32),
        compiler_params=_params(("arbitrary",)),
        name="experts",
    )(block_e, nused, xs, jnp.concatenate([w_gate, w_up], axis=2).astype(BF16), w_down.astype(BF16))


def _combine_kernel(ls_ref, un_ref, gd_ref, lpos_ref, route_ref, h1_ref, ys_ref, out_ref, yl, sems):
    i = pl.program_id(0)
    nt = pl.num_programs(0)
    tm = h1_ref.shape[0]
    rows = yl.shape[1]
    slot = i % 2

    def piece_copy(s, src, dst, n):
        return pltpu.make_async_copy(ys_ref.at[pl.ds(src, n)], yl.at[s, pl.ds(dst, n)], sems.at[s])

    def gather(tile, s):
        _run_pieces(tile, gd_ref, un_ref, ls_ref,
                    lambda src, dst, n, prio: piece_copy(s, src, dst, n).start(priority=prio))

    @pl.when(i == 0)
    def _():
        yl[...] = jnp.zeros_like(yl)
        gather(0, 0)

    @pl.when(i + 1 < nt)
    def _():
        gather(i + 1, 1 - slot)

    _wait_run_pieces(i, un_ref, lambda n: piece_copy(slot, 0, 0, n).wait())

    lp = lpos_ref[...]
    route = route_ref[...]
    r = lax.broadcasted_iota(I32, (tm, rows), 1)
    w = jnp.where(r == lp[:, 0:1], route[:, 2:3], 0.0) + jnp.where(r == lp[:, 1:2], route[:, 3:4], 0.0)
    out_ref[...] = h1_ref[...] + jnp.dot(w.astype(BF16), yl[slot].astype(BF16), preferred_element_type=F32)


def _combine(h1, route, lpos, tables, ys, tm):
    T, D = h1.shape
    local_rows = 2 * tm + N_EXPERTS * RUN_ROWS
    grid_spec = pltpu.PrefetchScalarGridSpec(
        num_scalar_prefetch=3,
        grid=(T // tm,),
        in_specs=[pl.BlockSpec((tm, LANES), lambda i, *_: (i, 0)),
                  pl.BlockSpec((tm, LANES), lambda i, *_: (i, 0)),
                  pl.BlockSpec((tm, D), lambda i, *_: (i, 0)),
                  pl.BlockSpec(memory_space=pl.ANY)],
        out_specs=pl.BlockSpec((tm, D), lambda i, *_: (i, 0)),
        scratch_shapes=[pltpu.VMEM((2, local_rows, D), F32), pltpu.SemaphoreType.DMA((2,))],
    )
    return pl.pallas_call(
        _combine_kernel,
        grid_spec=grid_spec,
        out_shape=jax.ShapeDtypeStruct((T, D), F32),
        compiler_params=_params(("arbitrary",)),
        name="combine",
    )(*tables, lpos, route, h1, ys)


def _mixers(x, norm_mix, w_in, q_lat_norm, kv_lat_norm, w_uq, w_ukv, q_norm, k_norm, lb_logits, layer):
    B, S, D = x.shape
    T = B * S
    q, k, v, hq, fz, hi, hg = _inproj(x.reshape(T, D), S, norm_mix, w_in, q_lat_norm, kv_lat_norm,
                                      w_uq, w_ukv, q_norm, k_norm)
    score_bound = (math.sqrt(MLA_QK) * math.log2(math.e) * 1.02) * jnp.max(jnp.abs(q_norm)) * jnp.max(jnp.abs(k_norm))
    a = _attention(q, k.reshape(B, S, -1), v, score_bound.astype(F32))
    lb = jnp.cumsum(jax.nn.softmax(lb_logits.astype(F32), axis=0), axis=0)[layer]
    hw = hq.shape[1]
    o = _hgrn(hq.reshape(B, S, hw), fz.reshape(2, B, S, hw), hi.reshape(B, S, hw), lb.reshape(2, 1, hw))
    return a.reshape(T, -1), o.reshape(2, T, hw), hg


def _moe(h1, n2, route, totals, w_gate, w_up, w_down, tm=512):
    T, D = h1.shape
    tm = min(tm, T)
    nt = T // tm
    lpos, lposT, runs, glob = _plan(route, totals, tm)
    per_run = lambda row: runs[:, row, :N_EXPERTS].reshape(-1)
    tables = (per_run(0), per_run(1), per_run(2))
    tails = (glob[1, :N_EXPERTS] * RUN_ROWS, glob[2, :N_EXPERTS])
    n_rows = -(-(2 * T + N_EXPERTS * RUN_ROWS * nt) // MOE_BLOCK) * MOE_BLOCK + N_EXPERTS * MOE_BLOCK
    nb = n_rows // MOE_BLOCK
    pend = glob[0, :N_EXPERTS] * RUN_ROWS
    block_row0 = jnp.arange(nb, dtype=I32) * MOE_BLOCK
    block_e = jnp.minimum(jnp.sum((pend[None, :] <= block_row0[:, None]).astype(I32), axis=1), N_EXPERTS - 1)
    nused = pend[N_EXPERTS - 1:] // MOE_BLOCK
    xs = _dispatch(n2, lposT, tables + tails, n_rows, tm)
    ys = _experts(xs, block_e, nused, w_gate, w_up, w_down)
    return _combine(h1, route, lpos, tables, ys, tm)


def kernel(x, norm_mix, w_in, q_lat_norm, kv_lat_norm, w_uq, w_ukv, q_norm, k_norm, lb_logits, hg_out_norm,
           w_out, norm_ffn, w_group, b_group, w_router, b_router, w_gate, w_up, w_down):
    B, S, D = x.shape
    h = x
    for l in range(norm_mix.shape[0]):
        h2 = h.reshape(B * S, D)
        a, o, hg = _mixers(h, norm_mix[l], w_in[l], q_lat_norm[l], kv_lat_norm[l], w_uq[l], w_ukv[l],
                           q_norm[l], k_norm[l], lb_logits, l)
        h1, n2, route, totals = _outproj(h2, a, o, hg, hg_out_norm[l], w_out[l], norm_ffn[l], w_group[l],
                                         b_group[l], w_router[l], b_router[l])
        h = _moe(h1, n2, route, totals, w_gate[l], w_up[l], w_down[l]).reshape(B, S, D)
    return h
```
